```python
import jax, jax.numpy as jnp
from jax import lax
import numpy as np

D_MODEL = 1024
BATCH = 4
SEQ = 4096
DEPTH = 1
DEC_BATCH = 8
DEC_SEQ = 32
PAST_LEN = 2048

CHUNK = 64
MIX_WIDTH = D_MODEL
SB_WIDTH = MIX_WIDTH // 2
SB_HEAD_DIM = 64
SB_HEADS = SB_WIDTH // SB_HEAD_DIM
POOL_WIDTH = MIX_WIDTH - SB_WIDTH
POOL_WINDOWS = (2, 4, 8, 16)
POOL_GROUPS = len(POOL_WINDOWS)
POOL_GC = POOL_WIDTH // POOL_GROUPS
POOL_HIST = max(POOL_WINDOWS) - 1
IN_WIDTH = 3 * SB_WIDTH + POOL_WIDTH
D_FF = -(-8 * D_MODEL // (3 * 256)) * 256
Q_BLOCK = 128
ALPHA = (2 * DEPTH) ** 0.25
BETA = (8 * DEPTH) ** -0.25
LN_EPS = 1e-5

kernel_name = "stickbreak_pool_hybrid_stream_step"


def _layer_norm(x, g, b):
    xf = x.astype(jnp.float32)
    mu = jnp.mean(xf, axis=-1, keepdims=True)
    var = jnp.mean(jnp.square(xf - mu), axis=-1, keepdims=True)
    return ((xf - mu) * lax.rsqrt(var + LN_EPS) * g.astype(jnp.float32) + b.astype(jnp.float32)).astype(x.dtype)


def _sb_block(qb, qpos, k, v, kpos):
    z = jnp.einsum('bqhd,bkhd->bhqk', qb.astype(jnp.float32), k.astype(jnp.float32)) * (SB_HEAD_DIM ** -0.5)
    mask = kpos[None, :] < qpos[:, None]
    log_beta = jax.nn.log_sigmoid(z)
    log_1m = jnp.where(mask, log_beta - z, 0.0)
    after = lax.cumsum(log_1m, axis=3, reverse=True) - log_1m
    a = jnp.where(mask, jnp.exp(log_beta + after), 0.0)
    return jnp.einsum('bhqk,bkhd->bqhd', a, v.astype(jnp.float32))


def _stick_breaking(q, k, v, q_start):
    B, T, H, dh = q.shape
    qbs = min(Q_BLOCK, T)
    nblk = T // qbs
    qpos = q_start + jnp.arange(T, dtype=jnp.int32)
    kpos = jnp.arange(k.shape[1], dtype=jnp.int32)
    qs = q.reshape(B, nblk, qbs, H, dh).transpose(1, 0, 2, 3, 4)
    ps = qpos.reshape(nblk, qbs)
    out = lax.map(lambda a: _sb_block(a[0], a[1], k, v, kpos), (qs, ps))
    return out.transpose(1, 0, 2, 3, 4).reshape(B, T, H * dh)


def _pool_mix(u, hist, start_pos, pool_w, pool_scale):
    B, T, C = u.shape
    ue = jnp.concatenate([hist.astype(jnp.float32), u.astype(jnp.float32)], axis=1)
    csum = jnp.concatenate([jnp.zeros((B, 1, C), jnp.float32), jnp.cumsum(ue, axis=1)], axis=1)
    pos = start_pos + jnp.arange(T, dtype=jnp.int32)
    P = POOL_HIST
    diffs = []
    for g, w in enumerate(POOL_WINDOWS):
        lo_c, hi_c = g * POOL_GC, (g + 1) * POOL_GC
        hi = csum[:, P + 1:P + 1 + T, lo_c:hi_c]
        lo = csum[:, P + 1 - w:P + 1 - w + T, lo_c:hi_c]
        cnt = jnp.minimum(w, pos + 1).astype(jnp.float32)[None, :, None]
        diffs.append((hi - lo) / cnt - ue[:, P:, lo_c:hi_c])
    d = jnp.stack(diffs, axis=2)
    y = jnp.einsum('btgc,gcd->btgd', d, pool_w.astype(jnp.float32)).reshape(B, T, C)
    y = y * pool_scale.astype(jnp.float32)
    return y.astype(u.dtype), ue[:, -P:].astype(u.dtype)


def _layer(x, k_hist, v_hist, pool_hist, start_pos, w_in, pool_w, pool_scale, w_out,
           ln1_g, ln1_b, w_gate, w_up, w_down, ln2_g, ln2_b):
    B, T, _ = x.shape
    h = x @ w_in
    q = h[..., :SB_WIDTH].reshape(B, T, SB_HEADS, SB_HEAD_DIM)
    k = h[..., SB_WIDTH:2 * SB_WIDTH].reshape(B, T, SB_HEADS, SB_HEAD_DIM)
    v = h[..., 2 * SB_WIDTH:3 * SB_WIDTH].reshape(B, T, SB_HEADS, SB_HEAD_DIM)
    u = h[..., 3 * SB_WIDTH:]
    k_all = jnp.concatenate([k_hist.astype(k.dtype), k], axis=1)
    v_all = jnp.concatenate([v_hist.astype(v.dtype), v], axis=1)
    a_out = _stick_breaking(q, k_all, v_all, start_pos).astype(x.dtype)
    p_out, new_pool = _pool_mix(u, pool_hist, start_pos, pool_w, pool_scale)
    mix = jnp.concatenate([a_out, p_out], axis=-1) @ w_out
    x1 = _layer_norm(ALPHA * x + mix, ln1_g, ln1_b)
    ffn = (jax.nn.silu(x1 @ w_gate) * (x1 @ w_up)) @ w_down
    y = _layer_norm(ALPHA * x1 + ffn, ln2_g, ln2_b)
    return y, k, v, new_pool


def setup_inputs(seed: int = 0) -> dict:
    key = jax.random.key(seed)
    ks = jax.random.split(key, 16)
    f32 = jnp.float32
    nrm = lambda k_, s: jax.random.normal(k_, s, f32)
    col_scale = jnp.concatenate([jnp.ones((2 * SB_WIDTH,), f32), jnp.full((SB_WIDTH,), BETA, f32),
                                 jnp.ones((POOL_WIDTH,), f32)])
    return {
        "x_prompt": nrm(ks[0], (BATCH, SEQ, D_MODEL)),
        "x_sample": nrm(ks[1], (DEC_BATCH, DEC_SEQ, D_MODEL)),
        "cache_k": nrm(ks[2], (DEC_BATCH, PAST_LEN, SB_HEADS, SB_HEAD_DIM)),
        "cache_v": nrm(ks[3], (DEC_BATCH, PAST_LEN, SB_HEADS, SB_HEAD_DIM)) * BETA,
        "state_pool": nrm(ks[4], (DEC_BATCH, POOL_HIST, POOL_WIDTH)),
        "w_in": nrm(ks[5], (D_MODEL, IN_WIDTH)) * (D_MODEL ** -0.5) * col_scale[None, :],
        "pool_w": nrm(ks[6], (POOL_GROUPS, POOL_GC, POOL_GC)) * (POOL_GC ** -0.5),
        "pool_scale": 1.0 + 0.02 * nrm(ks[7], (POOL_WIDTH,)),
        "w_out": nrm(ks[8], (MIX_WIDTH, D_MODEL)) * (MIX_WIDTH ** -0.5) * BETA,
        "ln1_g": 1.0 + 0.02 * nrm(ks[9], (D_MODEL,)),
        "ln1_b": 0.02 * nrm(ks[10], (D_MODEL,)),
        "w_gate": nrm(ks[11], (D_MODEL, D_FF)) * (D_MODEL ** -0.5),
        "w_up": nrm(ks[12], (D_MODEL, D_FF)) * (D_MODEL ** -0.5) * BETA,
        "w_down": nrm(ks[13], (D_FF, D_MODEL)) * (D_FF ** -0.5) * BETA,
        "ln2_g": 1.0 + 0.02 * nrm(ks[14], (D_MODEL,)),
        "ln2_b": 0.02 * nrm(ks[15], (D_MODEL,)),
    }


def reference(x_prompt, x_sample, cache_k, cache_v, state_pool, w_in, pool_w, pool_scale, w_out,
              ln1_g, ln1_b, w_gate, w_up, w_down, ln2_g, ln2_b):
    Bp = x_prompt.shape[0]
    yp = x_prompt
    ys = x_sample
    for _ in range(DEPTH):
        k_hist0 = jnp.zeros((Bp, 0, SB_HEADS, SB_HEAD_DIM), x_prompt.dtype)
        pool_hist0 = jnp.zeros((Bp, POOL_HIST, POOL_WIDTH), x_prompt.dtype)
        yp, k_p, v_p, pool_p = _layer(yp, k_hist0, k_hist0, pool_hist0, 0, w_in, pool_w, pool_scale, w_out,
                                      ln1_g, ln1_b, w_gate, w_up, w_down, ln2_g, ln2_b)
        ys, k_s, v_s, pool_s = _layer(ys, cache_k, cache_v, state_pool, cache_k.shape[1], w_in, pool_w,
                                      pool_scale, w_out, ln1_g, ln1_b, w_gate, w_up, w_down, ln2_g, ln2_b)
    return (yp, ys, k_p, v_p, pool_p, k_s, v_s, pool_s)
```

```python
import functools

import jax
import jax.numpy as jnp
from jax import lax
from jax.experimental import pallas as pl
from jax.experimental.pallas import tpu as pltpu

D_MODEL = 1024
SB_WIDTH = 512
SB_HEADS = 8
SB_HEAD_DIM = 64
POOL_WIDTH = 512
POOL_WINDOWS = (2, 4, 8, 16)
POOL_GC = 128
POOL_HIST = 15
HIST_ROWS = 16
D_FF = 2816
DEPTH = 1
ALPHA = (2 * DEPTH) ** 0.25
LN_EPS = 1e-5
LANES = 128
ATTN_BLOCK = 256
FF_CHUNK = 256
VMEM_LIMIT = 56 * 1024 * 1024

F32 = jnp.float32
BF16 = jnp.bfloat16


def _dot(a, b):
    return jnp.dot(a, b, preferred_element_type=F32)


def _dot_nt(a, b):
    return lax.dot_general(a, b, (((1,), (1,)), ((), ())), preferred_element_type=F32)


def _in_proj_kernel(x_ref, hist_ref, w_in_ref, pool_w_ref, pool_scale_ref,
                    k_ref, v_ref, qkv_ref, p_ref, pool_ref, u_ext, *, tm, start_pos, n_t):
    t = pl.program_id(1)
    x = x_ref[0].astype(BF16)

    q = _dot(x, w_in_ref[:, 0:SB_WIDTH])
    qkv_ref[0, :, 0:SB_WIDTH] = (q * (SB_HEAD_DIM ** -0.5)).astype(BF16)
    k = _dot(x, w_in_ref[:, SB_WIDTH:2 * SB_WIDTH])
    k_ref[0] = k
    qkv_ref[0, :, SB_WIDTH:2 * SB_WIDTH] = k.astype(BF16)
    v = _dot(x, w_in_ref[:, 2 * SB_WIDTH:3 * SB_WIDTH])
    v_ref[0] = v
    qkv_ref[0, :, 2 * SB_WIDTH:3 * SB_WIDTH] = v.astype(BF16)
    u = _dot(x, w_in_ref[:, 3 * SB_WIDTH:])

    @pl.when(t == 0)
    def _():
        u_ext[0:HIST_ROWS, :] = hist_ref[0]

    @pl.when(t > 0)
    def _():
        u_ext[0:HIST_ROWS, :] = u_ext[tm:tm + HIST_ROWS, :]

    u_ext[HIST_ROWS:HIST_ROWS + tm, :] = u

    pos = start_pos + t * tm + lax.broadcasted_iota(jnp.int32, (tm, 1), 0)
    for g, w in enumerate(POOL_WINDOWS):
        cols = slice(g * POOL_GC, (g + 1) * POOL_GC)
        s = u_ext[:, cols]
        span = 1
        while span < w:
            s = s + pltpu.roll(s, span, 0)
            span *= 2
        cnt = jnp.minimum(w, pos + 1).astype(F32)
        d = s[HIST_ROWS:, :] / cnt - u[:, cols]
        y = _dot(d.astype(BF16), pool_w_ref[g]) * pool_scale_ref[:, cols]
        p_ref[0, :, cols] = y.astype(BF16)

    @pl.when(t == n_t - 1)
    def _():
        pool_ref[0] = u_ext[tm + HIST_ROWS - POOL_HIST:tm + HIST_ROWS, :]


def _in_proj(x, hist, w_in_b, pool_w_b, pool_scale2, *, tm, start_pos):
    B, T, _ = x.shape
    assert T % tm == 0 and tm >= HIST_ROWS and T >= POOL_HIST
    n_t = T // tm
    kern = functools.partial(_in_proj_kernel, tm=tm, start_pos=start_pos, n_t=n_t)
    const = lambda *shape: pl.BlockSpec(shape, lambda b, t: (0,) * len(shape))
    return pl.pallas_call(
        kern,
        grid=(B, n_t),
        in_specs=[
            pl.BlockSpec((1, tm, D_MODEL), lambda b, t: (b, t, 0)),
            pl.BlockSpec((1, HIST_ROWS, POOL_WIDTH), lambda b, t: (b, 0, 0)),
            const(D_MODEL, 3 * SB_WIDTH + POOL_WIDTH),
            const(len(POOL_WINDOWS), POOL_GC, POOL_GC),
            const(1, POOL_WIDTH),
        ],
        out_specs=[
            pl.BlockSpec((1, tm, SB_WIDTH), lambda b, t: (b, t, 0)),
            pl.BlockSpec((1, tm, SB_WIDTH), lambda b, t: (b, t, 0)),
            pl.BlockSpec((1, tm, 3 * SB_WIDTH), lambda b, t: (b, t, 0)),
            pl.BlockSpec((1, tm, POOL_WIDTH), lambda b, t: (b, t, 0)),
            pl.BlockSpec((1, POOL_HIST, POOL_WIDTH), lambda b, t: (b, 0, 0)),
        ],
        out_shape=[
            jax.ShapeDtypeStruct((B, T, SB_WIDTH), F32),
            jax.ShapeDtypeStruct((B, T, SB_WIDTH), F32),
            jax.ShapeDtypeStruct((B, T, 3 * SB_WIDTH), BF16),
            jax.ShapeDtypeStruct((B, T, POOL_WIDTH), BF16),
            jax.ShapeDtypeStruct((B, POOL_HIST, POOL_WIDTH), F32),
        ],
        scratch_shapes=[pltpu.VMEM((HIST_ROWS + tm, POOL_WIDTH), F32)],
        compiler_params=pltpu.CompilerParams(
            dimension_semantics=("arbitrary", "arbitrary"), vmem_limit_bytes=VMEM_LIMIT),
        name="in_proj_pool",
    )(x, hist, w_in_b, pool_w_b, pool_scale2)


def _strict_upper(n):
    r = lax.broadcasted_iota(jnp.int32, (n, n), 0)
    c = lax.broadcasted_iota(jnp.int32, (n, n), 1)
    return (r > c).astype(BF16)


def _sb_block(qh, kblk, vblk, upper, carry, mask):
    z = _dot_nt(qh, kblk)
    sp = jnp.maximum(z, 0.0) + jnp.log(1.0 + jnp.exp(-jnp.abs(z)))
    log_1m = -sp
    if mask is not None:
        log_1m = jnp.where(mask, log_1m, 0.0)
    hi = log_1m.astype(BF16)
    lo = (log_1m - hi.astype(F32)).astype(BF16)
    after = _dot(hi, upper) + _dot(lo, upper) + carry
    a = jnp.exp(z - sp + after)
    if mask is not None:
        a = jnp.where(mask, a, 0.0)
    pv = _dot(a.astype(BF16), vblk)
    return pv, carry + jnp.sum(log_1m, axis=1, keepdims=True)


def _split_heads(q2):
    lane = lax.broadcasted_iota(jnp.int32, q2.shape, 1)
    zero = jnp.zeros_like(q2)
    return jnp.where(lane < SB_HEAD_DIM, q2, zero), jnp.where(lane >= SB_HEAD_DIM, q2, zero)


def _merge_heads(pv_a, pv_b):
    lane = lax.broadcasted_iota(jnp.int32, pv_a.shape, 1)
    return jnp.where(lane < SB_HEAD_DIM, pv_a, pv_b)


def _causal_mask(n):
    r = lax.broadcasted_iota(jnp.int32, (n, n), 0)
    c = lax.broadcasted_iota(jnp.int32, (n, n), 1)
    return c < r


def _attn_prompt_kernel(q_ref, k_ref, v_ref, o_ref, acc_ref, *, blk):
    qi = pl.program_id(2)
    qa, qb = _split_heads(q_ref[0])
    upper = _strict_upper(blk)
    zero_carry = jnp.zeros((blk, 1), F32)

    start = pl.multiple_of(qi * blk, blk)
    kblk = k_ref[0, pl.ds(start, blk), :]
    vblk = v_ref[0, pl.ds(start, blk), :]
    mask = _causal_mask(blk)
    pv_a, carry_a = _sb_block(qa, kblk, vblk, upper, zero_carry, mask)
    pv_b, carry_b = _sb_block(qb, kblk, vblk, upper, zero_carry, mask)
    acc_ref[...] = _merge_heads(pv_a, pv_b)

    def body(i, carries):
        carry_a, carry_b = carries
        start = pl.multiple_of((qi - 1 - i) * blk, blk)
        kblk = k_ref[0, pl.ds(start, blk), :]
        vblk = v_ref[0, pl.ds(start, blk), :]
        pv_a, carry_a = _sb_block(qa, kblk, vblk, upper, carry_a, None)
        pv_b, carry_b = _sb_block(qb, kblk, vblk, upper, carry_b, None)
        acc_ref[...] += _merge_heads(pv_a, pv_b)
        return carry_a, carry_b

    lax.fori_loop(0, qi, body, (carry_a, carry_b))
    o_ref[0] = acc_ref[...].astype(BF16)


def _attn_prompt(qkv):
    B, T, _ = qkv.shape
    blk = ATTN_BLOCK
    assert T % blk == 0
    n_hp = SB_WIDTH // LANES
    kern = functools.partial(_attn_prompt_kernel, blk=blk)
    return pl.pallas_call(
        kern,
        grid=(B, n_hp, T // blk),
        in_specs=[
            pl.BlockSpec((1, blk, LANES), lambda b, h, i: (b, i, h)),
            pl.BlockSpec((1, T, LANES), lambda b, h, i: (b, 0, n_hp + h)),
            pl.BlockSpec((1, T, LANES), lambda b, h, i: (b, 0, 2 * n_hp + h)),
        ],
        out_specs=pl.BlockSpec((1, blk, LANES), lambda b, h, i: (b, i, h)),
        out_shape=jax.ShapeDtypeStruct((B, T, SB_WIDTH), BF16),
        scratch_shapes=[pltpu.VMEM((blk, LANES), F32)],
        compiler_params=pltpu.CompilerParams(
            dimension_semantics=("arbitrary", "arbitrary", "arbitrary"), vmem_limit_bytes=VMEM_LIMIT),
        name="sb_attn_prompt",
    )(qkv, qkv, qkv)


def _attn_sample_kernel(q_ref, kn_ref, vn_ref, ck_ref, cv_ref, o_ref, *, t_new, past, blk):
    qa, qb = _split_heads(q_ref[0])
    zero_carry = jnp.zeros((t_new, 1), F32)
    mask = _causal_mask(t_new)
    upper_new = _strict_upper(t_new)
    pv_a, carry_a = _sb_block(qa, kn_ref[0], vn_ref[0], upper_new, zero_carry, mask)
    pv_b, carry_b = _sb_block(qb, kn_ref[0], vn_ref[0], upper_new, zero_carry, mask)
    acc = _merge_heads(pv_a, pv_b)

    upper = _strict_upper(blk)
    for j in reversed(range(past // blk)):
        kblk = ck_ref[0, j * blk:(j + 1) * blk, :].astype(BF16)
        vblk = cv_ref[0, j * blk:(j + 1) * blk, :].astype(BF16)
        pv_a, carry_a = _sb_block(qa, kblk, vblk, upper, carry_a, None)
        pv_b, carry_b = _sb_block(qb, kblk, vblk, upper, carry_b, None)
        acc = acc + _merge_heads(pv_a, pv_b)
    o_ref[0] = acc.astype(BF16)


def _attn_sample(qkv, cache_k, cache_v):
    B, t_new, _ = qkv.shape
    past = cache_k.shape[1]
    blk = ATTN_BLOCK
    assert past % blk == 0
    n_hp = SB_WIDTH // LANES
    kern = functools.partial(_attn_sample_kernel, t_new=t_new, past=past, blk=blk)
    return pl.pallas_call(
        kern,
        grid=(B, n_hp),
        in_specs=[
            pl.BlockSpec((1, t_new, LANES), lambda b, h: (b, 0, h)),
            pl.BlockSpec((1, t_new, LANES), lambda b, h: (b, 0, n_hp + h)),
            pl.BlockSpec((1, t_new, LANES), lambda b, h: (b, 0, 2 * n_hp + h)),
            pl.BlockSpec((1, past, LANES), lambda b, h: (b, 0, h)),
            pl.BlockSpec((1, past, LANES), lambda b, h: (b, 0, h)),
        ],
        out_specs=pl.BlockSpec((1, t_new, LANES), lambda b, h: (b, 0, h)),
        out_shape=jax.ShapeDtypeStruct((B, t_new, SB_WIDTH), BF16),
        compiler_params=pltpu.CompilerParams(
            dimension_semantics=("arbitrary", "arbitrary"), vmem_limit_bytes=VMEM_LIMIT),
        name="sb_attn_sample",
    )(qkv, qkv, qkv, cache_k, cache_v)


def _layer_norm(x, g, b):
    mu = jnp.mean(x, axis=-1, keepdims=True)
    xc = x - mu
    var = jnp.mean(xc * xc, axis=-1, keepdims=True)
    return xc * lax.rsqrt(var + LN_EPS) * g + b


def _out_ffn_kernel(x_ref, a_ref, p_ref, wo_ref, g1_ref, b1_ref, wg_ref, wu_ref, wd_ref, g2_ref, b2_ref,
                    y_ref, acc_ref):
    mix = _dot(a_ref[...], wo_ref[0:SB_WIDTH, :]) + _dot(p_ref[...], wo_ref[SB_WIDTH:, :])
    x1 = _layer_norm(ALPHA * x_ref[...] + mix, g1_ref[...], b1_ref[...])
    x1b = x1.astype(BF16)
    for c in range(0, D_FF, FF_CHUNK):
        gate = _dot(x1b, wg_ref[:, c:c + FF_CHUNK])
        up = _dot(x1b, wu_ref[:, c:c + FF_CHUNK])
        hmid = (gate * jax.nn.sigmoid(gate) * up).astype(BF16)
        part = _dot(hmid, wd_ref[c:c + FF_CHUNK, :])
        if c == 0:
            acc_ref[...] = part
        else:
            acc_ref[...] += part
    y_ref[...] = _layer_norm(ALPHA * x1 + acc_ref[...], g2_ref[...], b2_ref[...])


def _out_ffn(x, a, p, wo_b, g1, b1, wg_b, wu_b, wd_b, g2, b2, *, tm):
    N = x.shape[0]
    assert N % tm == 0 and D_FF % FF_CHUNK == 0
    row = lambda width: pl.BlockSpec((tm, width), lambda i: (i, 0))
    const = lambda *shape: pl.BlockSpec(shape, lambda i: (0,) * len(shape), pipeline_mode=pl.Buffered(1))
    return pl.pallas_call(
        _out_ffn_kernel,
        grid=(N // tm,),
        in_specs=[
            row(D_MODEL), row(SB_WIDTH), row(POOL_WIDTH),
            const(SB_WIDTH + POOL_WIDTH, D_MODEL), const(1, D_MODEL), const(1, D_MODEL),
            const(D_MODEL, D_FF), const(D_MODEL, D_FF), const(D_FF, D_MODEL),
            const(1, D_MODEL), const(1, D_MODEL),
        ],
        out_specs=row(D_MODEL),
        out_shape=jax.ShapeDtypeStruct((N, D_MODEL), F32),
        scratch_shapes=[pltpu.VMEM((tm, D_MODEL), F32)],
        compiler_params=pltpu.CompilerParams(
            dimension_semantics=("arbitrary",), vmem_limit_bytes=VMEM_LIMIT),
        name="out_proj_ffn",
    )(x, a, p, wo_b, g1, b1, wg_b, wu_b, wd_b, g2, b2)


def kernel(x_prompt, x_sample, cache_k, cache_v, state_pool, w_in, pool_w, pool_scale, w_out,
           ln1_g, ln1_b, w_gate, w_up, w_down, ln2_g, ln2_b):
    Bp, Tp, _ = x_prompt.shape
    Bs, Ts, _ = x_sample.shape
    past = cache_k.shape[1]

    w_in_b = w_in.astype(BF16)
    pool_w_b = pool_w.astype(BF16)
    wo_b = w_out.astype(BF16)
    wg_b = w_gate.astype(BF16)
    wu_b = w_up.astype(BF16)
    wd_b = w_down.astype(BF16)
    row2 = lambda a: a.reshape(1, -1).astype(F32)
    pool_scale2 = row2(pool_scale)
    ffn_args = (wo_b, row2(ln1_g), row2(ln1_b), wg_b, wu_b, wd_b, row2(ln2_g), row2(ln2_b))

    hist_p = jnp.zeros((Bp, HIST_ROWS, POOL_WIDTH), F32)
    k_p, v_p, qkv_p, mix_p, pool_p = _in_proj(x_prompt, hist_p, w_in_b, pool_w_b, pool_scale2, tm=512, start_pos=0)
    a_p = _attn_prompt(qkv_p)
    y_p = _out_ffn(x_prompt.reshape(Bp * Tp, D_MODEL), a_p.reshape(Bp * Tp, SB_WIDTH),
                   mix_p.reshape(Bp * Tp, POOL_WIDTH), *ffn_args, tm=512)

    hist_s = jnp.concatenate([jnp.zeros((Bs, HIST_ROWS - POOL_HIST, POOL_WIDTH), F32), state_pool.astype(F32)], axis=1)
    k_s, v_s, qkv_s, mix_s, pool_s = _in_proj(x_sample, hist_s, w_in_b, pool_w_b, pool_scale2, tm=Ts, start_pos=past)
    a_s = _attn_sample(qkv_s, cache_k.reshape(Bs, past, SB_WIDTH), cache_v.reshape(Bs, past, SB_WIDTH))
    y_s = _out_ffn(x_sample.reshape(Bs * Ts, D_MODEL), a_s.reshape(Bs * Ts, SB_WIDTH),
                   mix_s.reshape(Bs * Ts, POOL_WIDTH), *ffn_args, tm=Bs * Ts)

    heads = lambda a: a.reshape(a.shape[0], a.shape[1], SB_HEADS, SB_HEAD_DIM)
    return (y_p.reshape(Bp, Tp, D_MODEL), y_s.reshape(Bs, Ts, D_MODEL),
            heads(k_p), heads(v_p), pool_p, heads(k_s), heads(v_s), pool_s)
```

```python
import functools

import jax
import jax.numpy as jnp
from jax import lax
from jax.experimental import pallas as pl
from jax.experimental.pallas import tpu as pltpu

D_MODEL = 1024
SB_WIDTH = 512
SB_HEADS = 8
SB_HEAD_DIM = 64
POOL_WIDTH = 512
POOL_WINDOWS = (2, 4, 8, 16)
POOL_GC = 128
POOL_HIST = 15
HIST_ROWS = 16
D_FF = 2816
DEPTH = 1
ALPHA = (2 * DEPTH) ** 0.25
LN_EPS = 1e-5
LANES = 128
HEAD_PAIRS = SB_WIDTH // LANES
LOG2E = 1.4426950408889634
Q_SCALE = SB_HEAD_DIM ** -0.5 * LOG2E
SKIP_LOG2 = 160.0
ATTN_BLOCK = 256
FF_CHUNK = 256
VMEM_LIMIT = 56 * 1024 * 1024

F32 = jnp.float32
BF16 = jnp.bfloat16


def _dot(a, b):
    return jnp.dot(a, b, preferred_element_type=F32)


def _dot_nt(a, b):
    return lax.dot_general(a, b, (((1,), (1,)), ((), ())), preferred_element_type=F32)


def _in_proj_kernel(x_ref, hist_ref, w_in_ref, pool_w_ref, pool_scale_ref,
                    k_ref, v_ref, qkv_ref, p_ref, pool_ref, u_ext, *, tm, start_pos, n_t):
    t = pl.program_id(1)
    x = x_ref[0].astype(BF16)

    q = _dot(x, w_in_ref[:, 0:SB_WIDTH])
    qkv_ref[0, :, 0:SB_WIDTH] = (q * Q_SCALE).astype(BF16)
    k = _dot(x, w_in_ref[:, SB_WIDTH:2 * SB_WIDTH])
    k_ref[0] = k
    qkv_ref[0, :, SB_WIDTH:2 * SB_WIDTH] = k.astype(BF16)
    v = _dot(x, w_in_ref[:, 2 * SB_WIDTH:3 * SB_WIDTH])
    v_ref[0] = v
    qkv_ref[0, :, 2 * SB_WIDTH:3 * SB_WIDTH] = v.astype(BF16)
    u = _dot(x, w_in_ref[:, 3 * SB_WIDTH:])

    @pl.when(t == 0)
    def _():
        u_ext[0:HIST_ROWS, :] = hist_ref[0]

    @pl.when(t > 0)
    def _():
        u_ext[0:HIST_ROWS, :] = u_ext[tm:tm + HIST_ROWS, :]

    u_ext[HIST_ROWS:HIST_ROWS + tm, :] = u

    pos = start_pos + t * tm + lax.broadcasted_iota(jnp.int32, (tm, 1), 0)
    for g, w in enumerate(POOL_WINDOWS):
        cols = slice(g * POOL_GC, (g + 1) * POOL_GC)
        s = u_ext[:, cols]
        span = 1
        while span < w:
            s = s + pltpu.roll(s, span, 0)
            span *= 2
        cnt = jnp.minimum(w, pos + 1).astype(F32)
        d = s[HIST_ROWS:, :] / cnt - u[:, cols]
        y = _dot(d.astype(BF16), pool_w_ref[g]) * pool_scale_ref[:, cols]
        p_ref[0, :, cols] = y.astype(BF16)

    @pl.when(t == n_t - 1)
    def _():
        pool_ref[0] = u_ext[tm + HIST_ROWS - POOL_HIST:tm + HIST_ROWS, :]


def _in_proj(x, hist, w_in_b, pool_w_b, pool_scale2, *, tm, start_pos):
    B, T, _ = x.shape
    assert T % tm == 0 and tm >= HIST_ROWS and T >= POOL_HIST
    n_t = T // tm
    kern = functools.partial(_in_proj_kernel, tm=tm, start_pos=start_pos, n_t=n_t)
    const = lambda *shape: pl.BlockSpec(shape, lambda b, t: (0,) * len(shape))
    return pl.pallas_call(
        kern,
        grid=(B, n_t),
        in_specs=[
            pl.BlockSpec((1, tm, D_MODEL), lambda b, t: (b, t, 0)),
            pl.BlockSpec((1, HIST_ROWS, POOL_WIDTH), lambda b, t: (b, 0, 0)),
            const(D_MODEL, 3 * SB_WIDTH + POOL_WIDTH),
            const(len(POOL_WINDOWS), POOL_GC, POOL_GC),
            const(1, POOL_WIDTH),
        ],
        out_specs=[
            pl.BlockSpec((1, tm, SB_WIDTH), lambda b, t: (b, t, 0)),
            pl.BlockSpec((1, tm, SB_WIDTH), lambda b, t: (b, t, 0)),
            pl.BlockSpec((1, tm, 3 * SB_WIDTH), lambda b, t: (b, t, 0)),
            pl.BlockSpec((1, tm, POOL_WIDTH), lambda b, t: (b, t, 0)),
            pl.BlockSpec((1, POOL_HIST, POOL_WIDTH), lambda b, t: (b, 0, 0)),
        ],
        out_shape=[
            jax.ShapeDtypeStruct((B, T, SB_WIDTH), F32),
            jax.ShapeDtypeStruct((B, T, SB_WIDTH), F32),
            jax.ShapeDtypeStruct((B, T, 3 * SB_WIDTH), BF16),
            jax.ShapeDtypeStruct((B, T, POOL_WIDTH), BF16),
            jax.ShapeDtypeStruct((B, POOL_HIST, POOL_WIDTH), F32),
        ],
        scratch_shapes=[pltpu.VMEM((HIST_ROWS + tm, POOL_WIDTH), F32)],
        compiler_params=pltpu.CompilerParams(
            dimension_semantics=("arbitrary", "arbitrary"), vmem_limit_bytes=VMEM_LIMIT),
        name="in_proj_pool",
    )(x, hist, w_in_b, pool_w_b, pool_scale2)


def _strict_upper(n):
    r = lax.broadcasted_iota(jnp.int32, (n, n), 0)
    c = lax.broadcasted_iota(jnp.int32, (n, n), 1)
    return (r > c).astype(BF16)


def _causal_mask(n):
    r = lax.broadcasted_iota(jnp.int32, (n, n), 0)
    c = lax.broadcasted_iota(jnp.int32, (n, n), 1)
    return c < r


def _store_masked_queries(q_ref, qm_ref):
    m = q_ref.shape[1]
    lower_half = lax.broadcasted_iota(jnp.int32, (m, LANES), 1) < SB_HEAD_DIM
    for p in range(HEAD_PAIRS):
        q2 = q_ref[0, :, p * LANES:(p + 1) * LANES]
        zero = jnp.zeros_like(q2)
        qm_ref[p, 0:m, :] = jnp.where(lower_half, q2, zero)
        qm_ref[p, m:2 * m, :] = jnp.where(lower_half, zero, q2)


def _sb_all_heads(qm_ref, kv_block, decay_ref, acc_ref, upper, mask, first):
    m = qm_ref.shape[1] // 2
    lower_half = lax.broadcasted_iota(jnp.int32, (m, LANES), 1) < SB_HEAD_DIM
    mask2 = None if mask is None else jnp.concatenate([mask, mask], axis=0)
    kv = [kv_block(p) for p in range(HEAD_PAIRS)]

    logit, his, los, rowsums = [], [], [], []
    for p in range(HEAD_PAIRS):
        z = _dot_nt(qm_ref[p], kv[p][0])
        sp = jnp.maximum(z, 0.0) + jnp.log(1.0 + jnp.exp2(-jnp.abs(z))) * LOG2E
        spm = sp if mask2 is None else jnp.where(mask2, sp, 0.0)
        hi = spm.astype(BF16)
        his.append(hi)
        los.append((spm - hi.astype(F32)).astype(BF16))
        logit.append(z - sp)
        rowsums.append(jnp.sum(spm, axis=1, keepdims=True))

    rows = 2 * m * HEAD_PAIRS
    later2 = _dot(jnp.concatenate(his + los, axis=0), upper)
    later = later2[0:rows] + later2[rows:2 * rows]

    least = None
    for p in range(HEAD_PAIRS):
        a = jnp.exp2(logit[p] - later[2 * m * p:2 * m * (p + 1)])
        if mask2 is not None:
            a = jnp.where(mask2, a, 0.0)
        pv = _dot(a.astype(BF16), kv[p][1])
        merged = jnp.where(lower_half, pv[0:m], pv[m:2 * m])
        rowsum = jnp.where(lower_half, rowsums[p][0:m], rowsums[p][m:2 * m])
        if first:
            decay = rowsum
            acc_ref[p] = merged
        else:
            prev = decay_ref[p]
            decay = prev + rowsum
            acc_ref[p] += merged * jnp.exp2(-prev)
        decay_ref[p] = decay
        least = decay if least is None else jnp.minimum(least, decay)
    return jnp.min(least)


def _more_blocks(state):
    j, least = state
    return jnp.logical_and(j >= 0, least < SKIP_LOG2)


def _store_heads(acc_ref, o_ref):
    for p in range(HEAD_PAIRS):
        o_ref[0, :, p * LANES:(p + 1) * LANES] = acc_ref[p].astype(BF16)


def _attn_prompt_kernel(q_ref, k_ref, v_ref, o_ref, qm_ref, decay_ref, acc_ref, *, blk):
    qi = pl.program_id(1)
    _store_masked_queries(q_ref, qm_ref)
    upper = _strict_upper(blk)

    def kv_block(j):
        start = pl.multiple_of(j * blk, blk)
        return lambda p: (k_ref[0, pl.ds(start, blk), p * LANES:(p + 1) * LANES],
                          v_ref[0, pl.ds(start, blk), p * LANES:(p + 1) * LANES])

    least = _sb_all_heads(qm_ref, kv_block(qi), decay_ref, acc_ref, upper, _causal_mask(blk), True)

    def body(state):
        j, _ = state
        return j - 1, _sb_all_heads(qm_ref, kv_block(j), decay_ref, acc_ref, upper, None, False)

    lax.while_loop(_more_blocks, body, (qi - 1, least))
    _store_heads(acc_ref, o_ref)


def _attn_scratch(m):
    return [pltpu.VMEM((HEAD_PAIRS, 2 * m, LANES), BF16), pltpu.VMEM((HEAD_PAIRS, m, LANES), F32),
            pltpu.VMEM((HEAD_PAIRS, m, LANES), F32)]


def _attn_prompt(qkv):
    B, T, _ = qkv.shape
    blk = ATTN_BLOCK
    assert T % blk == 0
    kern = functools.partial(_attn_prompt_kernel, blk=blk)
    return pl.pallas_call(
        kern,
        grid=(B, T // blk),
        in_specs=[
            pl.BlockSpec((1, blk, SB_WIDTH), lambda b, i: (b, i, 0)),
            pl.BlockSpec((1, T, SB_WIDTH), lambda b, i: (b, 0, 1)),
            pl.BlockSpec((1, T, SB_WIDTH), lambda b, i: (b, 0, 2)),
        ],
        out_specs=pl.BlockSpec((1, blk, SB_WIDTH), lambda b, i: (b, i, 0)),
        out_shape=jax.ShapeDtypeStruct((B, T, SB_WIDTH), BF16),
        scratch_shapes=_attn_scratch(blk),
        compiler_params=pltpu.CompilerParams(
            dimension_semantics=("arbitrary", "arbitrary"), vmem_limit_bytes=VMEM_LIMIT),
        name="sb_attn_prompt",
    )(qkv, qkv, qkv)


def _attn_sample_kernel(q_ref, kn_ref, vn_ref, ck_ref, cv_ref, o_ref, qm_ref, decay_ref, acc_ref,
                        *, t_new, past, blk):
    _store_masked_queries(q_ref, qm_ref)
    new_kv = lambda p: (kn_ref[0, :, p * LANES:(p + 1) * LANES], vn_ref[0, :, p * LANES:(p + 1) * LANES])
    least = _sb_all_heads(qm_ref, new_kv, decay_ref, acc_ref, _strict_upper(t_new), _causal_mask(t_new), True)
    upper = _strict_upper(blk)

    def cache_kv(j):
        start = pl.multiple_of(j * blk, blk)
        return lambda p: (ck_ref[0, pl.ds(start, blk), p * LANES:(p + 1) * LANES].astype(BF16),
                          cv_ref[0, pl.ds(start, blk), p * LANES:(p + 1) * LANES].astype(BF16))

    def body(state):
        j, _ = state
        return j - 1, _sb_all_heads(qm_ref, cache_kv(j), decay_ref, acc_ref, upper, None, False)

    lax.while_loop(_more_blocks, body, (past // blk - 1, least))
    _store_heads(acc_ref, o_ref)


def _attn_sample(qkv, cache_k, cache_v):
    B, t_new, _ = qkv.shape
    past = cache_k.shape[1]
    blk = ATTN_BLOCK
    assert past % blk == 0
    kern = functools.partial(_attn_sample_kernel, t_new=t_new, past=past, blk=blk)
    return pl.pallas_call(
        kern,
        grid=(B,),
        in_specs=[
            pl.BlockSpec((1, t_new, SB_WIDTH), lambda b: (b, 0, 0)),
            pl.BlockSpec((1, t_new, SB_WIDTH), lambda b: (b, 0, 1)),
            pl.BlockSpec((1, t_new, SB_WIDTH), lambda b: (b, 0, 2)),
            pl.BlockSpec((1, past, SB_WIDTH), lambda b: (b, 0, 0)),
            pl.BlockSpec((1, past, SB_WIDTH), lambda b: (b, 0, 0)),
        ],
        out_specs=pl.BlockSpec((1, t_new, SB_WIDTH), lambda b: (b, 0, 0)),
        out_shape=jax.ShapeDtypeStruct((B, t_new, SB_WIDTH), BF16),
        scratch_shapes=_attn_scratch(t_new),
        compiler_params=pltpu.CompilerParams(
            dimension_semantics=("arbitrary",), vmem_limit_bytes=VMEM_LIMIT),
        name="sb_attn_sample",
    )(qkv, qkv, qkv, cache_k, cache_v)


def _layer_norm(x, g, b):
    mu = jnp.mean(x, axis=-1, keepdims=True)
    xc = x - mu
    var = jnp.mean(xc * xc, axis=-1, keepdims=True)
    return xc * lax.rsqrt(var + LN_EPS) * g + b


def _out_ffn_kernel(x_ref, a_ref, p_ref, wo_ref, g1_ref, b1_ref, wg_ref, wu_ref, wd_ref, g2_ref, b2_ref,
                    y_ref, acc_ref):
    mix = _dot(a_ref[...], wo_ref[0:SB_WIDTH, :]) + _dot(p_ref[...], wo_ref[SB_WIDTH:, :])
    x1 = _layer_norm(ALPHA * x_ref[...] + mix, g1_ref[...], b1_ref[...])
    x1b = x1.astype(BF16)
    for c in range(0, D_FF, FF_CHUNK):
        gate = _dot(x1b, wg_ref[:, c:c + FF_CHUNK])
        up = _dot(x1b, wu_ref[:, c:c + FF_CHUNK])
        hmid = (gate * jax.nn.sigmoid(gate) * up).astype(BF16)
        part = _dot(hmid, wd_ref[c:c + FF_CHUNK, :])
        if c == 0:
            acc_ref[...] = part
        else:
            acc_ref[...] += part
    y_ref[...] = _layer_norm(ALPHA * x1 + acc_ref[...], g2_ref[...], b2_ref[...])


def _out_ffn(x, a, p, wo_b, g1, b1, wg_b, wu_b, wd_b, g2, b2, *, tm):
    N = x.shape[0]
    assert N % tm == 0 and D_FF % FF_CHUNK == 0
    row = lambda width: pl.BlockSpec((tm, width), lambda i: (i, 0))
    const = lambda *shape: pl.BlockSpec(shape, lambda i: (0,) * len(shape), pipeline_mode=pl.Buffered(1))
    return pl.pallas_call(
        _out_ffn_kernel,
        grid=(N // tm,),
        in_specs=[
            row(D_MODEL), row(SB_WIDTH), row(POOL_WIDTH),
            const(SB_WIDTH + POOL_WIDTH, D_MODEL), const(1, D_MODEL), const(1, D_MODEL),
            const(D_MODEL, D_FF), const(D_MODEL, D_FF), const(D_FF, D_MODEL),
            const(1, D_MODEL), const(1, D_MODEL),
        ],
        out_specs=row(D_MODEL),
        out_shape=jax.ShapeDtypeStruct((N, D_MODEL), F32),
        scratch_shapes=[pltpu.VMEM((tm, D_MODEL), F32)],
        compiler_params=pltpu.CompilerParams(
            dimension_semantics=("arbitrary",), vmem_limit_bytes=VMEM_LIMIT),
        name="out_proj_ffn",
    )(x, a, p, wo_b, g1, b1, wg_b, wu_b, wd_b, g2, b2)


def kernel(x_prompt, x_sample, cache_k, cache_v, state_pool, w_in, pool_w, pool_scale, w_out,
           ln1_g, ln1_b, w_gate, w_up, w_down, ln2_g, ln2_b):
    Bp, Tp, _ = x_prompt.shape
    Bs, Ts, _ = x_sample.shape
    past = cache_k.shape[1]

    w_in_b = w_in.astype(BF16)
    pool_w_b = pool_w.astype(BF16)
    wo_b = w_out.astype(BF16)
    wg_b = w_gate.astype(BF16)
    wu_b = w_up.astype(BF16)
    wd_b = w_down.astype(BF16)
    row2 = lambda a: a.reshape(1, -1).astype(F32)
    pool_scale2 = row2(pool_scale)
    ffn_args = (wo_b, row2(ln1_g), row2(ln1_b), wg_b, wu_b, wd_b, row2(ln2_g), row2(ln2_b))

    hist_p = jnp.zeros((Bp, HIST_ROWS, POOL_WIDTH), F32)
    k_p, v_p, qkv_p, mix_p, pool_p = _in_proj(x_prompt, hist_p, w_in_b, pool_w_b, pool_scale2, tm=512, start_pos=0)
    a_p = _attn_prompt(qkv_p)
    y_p = _out_ffn(x_prompt.reshape(Bp * Tp, D_MODEL), a_p.reshape(Bp * Tp, SB_WIDTH),
                   mix_p.reshape(Bp * Tp, POOL_WIDTH), *ffn_args, tm=512)

    hist_s = jnp.concatenate([jnp.zeros((Bs, HIST_ROWS - POOL_HIST, POOL_WIDTH), F32), state_pool.astype(F32)], axis=1)
    k_s, v_s, qkv_s, mix_s, pool_s = _in_proj(x_sample, hist_s, w_in_b, pool_w_b, pool_scale2, tm=Ts, start_pos=past)
    a_s = _attn_sample(qkv_s, cache_k.reshape(Bs, past, SB_WIDTH), cache_v.reshape(Bs, past, SB_WIDTH))
    y_s = _out_ffn(x_sample.reshape(Bs * Ts, D_MODEL), a_s.reshape(Bs * Ts, SB_WIDTH),
                   mix_s.reshape(Bs * Ts, POOL_WIDTH), *ffn_args, tm=Bs * Ts)

    heads = lambda a: a.reshape(a.shape[0], a.shape[1], SB_HEADS, SB_HEAD_DIM)
    return (y_p.reshape(Bp, Tp, D_MODEL), y_s.reshape(Bs, Ts, D_MODEL),
            heads(k_p), heads(v_p), pool_p, heads(k_s), heads(v_s), pool_s)
```

```python
import functools

import jax
import jax.numpy as jnp
from jax import lax
from jax.experimental import pallas as pl
from jax.experimental.pallas import tpu as pltpu

D_MODEL = 1024
SB_WIDTH = 512
SB_HEADS = 8
SB_HEAD_DIM = 64
POOL_WIDTH = 512
POOL_WINDOWS = (2, 4, 8, 16)
POOL_GC = 128
POOL_HIST = 15
HIST_ROWS = 16
D_FF = 2816
DEPTH = 1
ALPHA = (2 * DEPTH) ** 0.25
LN_EPS = 1e-5
LANES = 128
HEAD_PAIRS = SB_WIDTH // LANES
LOG2E = 1.4426950408889634
Q_SCALE = SB_HEAD_DIM ** -0.5 * LOG2E
SKIP_LOG2 = 160.0
ATTN_BLOCK = 256
FF_CHUNK = 256
VMEM_LIMIT = 56 * 1024 * 1024

F32 = jnp.float32
BF16 = jnp.bfloat16


def _dot(a, b):
    return jnp.dot(a, b, preferred_element_type=F32)


def _dot_nt(a, b):
    return lax.dot_general(a, b, (((1,), (1,)), ((), ())), preferred_element_type=F32)


def _in_proj_kernel(x_ref, hist_ref, w_in_ref, pool_w_ref, pool_scale_ref,
                    k_ref, v_ref, qkv_ref, p_ref, pool_ref, u_ext, *, tm, start_pos, n_t):
    t = pl.program_id(1)
    x = x_ref[0].astype(BF16)

    q = _dot(x, w_in_ref[:, 0:SB_WIDTH])
    qkv_ref[0, :, 0:SB_WIDTH] = (q * Q_SCALE).astype(BF16)
    k = _dot(x, w_in_ref[:, SB_WIDTH:2 * SB_WIDTH])
    k_ref[0] = k
    qkv_ref[0, :, SB_WIDTH:2 * SB_WIDTH] = k.astype(BF16)
    v = _dot(x, w_in_ref[:, 2 * SB_WIDTH:3 * SB_WIDTH])
    v_ref[0] = v
    qkv_ref[0, :, 2 * SB_WIDTH:3 * SB_WIDTH] = v.astype(BF16)
    u = _dot(x, w_in_ref[:, 3 * SB_WIDTH:])

    @pl.when(t == 0)
    def _():
        u_ext[0:HIST_ROWS, :] = hist_ref[0]

    @pl.when(t > 0)
    def _():
        u_ext[0:HIST_ROWS, :] = u_ext[tm:tm + HIST_ROWS, :]

    u_ext[HIST_ROWS:HIST_ROWS + tm, :] = u

    pos = start_pos + t * tm + lax.broadcasted_iota(jnp.int32, (tm, 1), 0)
    for g, w in enumerate(POOL_WINDOWS):
        cols = slice(g * POOL_GC, (g + 1) * POOL_GC)
        s = u_ext[:, cols]
        span = 1
        while span < w:
            s = s + pltpu.roll(s, span, 0)
            span *= 2
        cnt = jnp.minimum(w, pos + 1).astype(F32)
        d = s[HIST_ROWS:, :] / cnt - u[:, cols]
        y = _dot(d.astype(BF16), pool_w_ref[g]) * pool_scale_ref[:, cols]
        p_ref[0, :, cols] = y.astype(BF16)

    @pl.when(t == n_t - 1)
    def _():
        pool_ref[0] = u_ext[tm + HIST_ROWS - POOL_HIST:tm + HIST_ROWS, :]


def _in_proj(x, hist, w_in_b, pool_w_b, pool_scale2, *, tm, start_pos):
    B, T, _ = x.shape
    assert T % tm == 0 and tm >= HIST_ROWS and T >= POOL_HIST
    n_t = T // tm
    kern = functools.partial(_in_proj_kernel, tm=tm, start_pos=start_pos, n_t=n_t)
    const = lambda *shape: pl.BlockSpec(shape, lambda b, t: (0,) * len(shape))
    return pl.pallas_call(
        kern,
        grid=(B, n_t),
        in_specs=[
            pl.BlockSpec((1, tm, D_MODEL), lambda b, t: (b, t, 0)),
            pl.BlockSpec((1, HIST_ROWS, POOL_WIDTH), lambda b, t: (b, 0, 0)),
            const(D_MODEL, 3 * SB_WIDTH + POOL_WIDTH),
            const(len(POOL_WINDOWS), POOL_GC, POOL_GC),
            const(1, POOL_WIDTH),
        ],
        out_specs=[
            pl.BlockSpec((1, tm, SB_WIDTH), lambda b, t: (b, t, 0)),
            pl.BlockSpec((1, tm, SB_WIDTH), lambda b, t: (b, t, 0)),
            pl.BlockSpec((1, tm, 3 * SB_WIDTH), lambda b, t: (b, t, 0)),
            pl.BlockSpec((1, tm, POOL_WIDTH), lambda b, t: (b, t, 0)),
            pl.BlockSpec((1, POOL_HIST, POOL_WIDTH), lambda b, t: (b, 0, 0)),
        ],
        out_shape=[
            jax.ShapeDtypeStruct((B, T, SB_WIDTH), F32),
            jax.ShapeDtypeStruct((B, T, SB_WIDTH), F32),
            jax.ShapeDtypeStruct((B, T, 3 * SB_WIDTH), BF16),
            jax.ShapeDtypeStruct((B, T, POOL_WIDTH), BF16),
            jax.ShapeDtypeStruct((B, POOL_HIST, POOL_WIDTH), F32),
        ],
        scratch_shapes=[pltpu.VMEM((HIST_ROWS + tm, POOL_WIDTH), F32)],
        compiler_params=pltpu.CompilerParams(
            dimension_semantics=("arbitrary", "arbitrary"), vmem_limit_bytes=VMEM_LIMIT),
        name="in_proj_pool",
    )(x, hist, w_in_b, pool_w_b, pool_scale2)


def _strict_upper(n):
    r = lax.broadcasted_iota(jnp.int32, (n, n), 0)
    c = lax.broadcasted_iota(jnp.int32, (n, n), 1)
    return (r > c).astype(BF16)


def _causal_mask(n):
    r = lax.broadcasted_iota(jnp.int32, (n, n), 0)
    c = lax.broadcasted_iota(jnp.int32, (n, n), 1)
    return c < r


def _store_masked_queries(q_ref, qm_ref):
    m = q_ref.shape[1]
    lower_half = lax.broadcasted_iota(jnp.int32, (m, LANES), 1) < SB_HEAD_DIM
    for p in range(HEAD_PAIRS):
        q2 = q_ref[0, :, p * LANES:(p + 1) * LANES]
        zero = jnp.zeros_like(q2)
        qm_ref[p, 0:m, :] = jnp.where(lower_half, q2, zero)
        qm_ref[p, m:2 * m, :] = jnp.where(lower_half, zero, q2)


def _sb_all_heads(qm_ref, kv_block, decay_ref, acc_ref, upper, mask, first, key_minor=False):
    m = qm_ref.shape[1] // 2
    lower_half = lax.broadcasted_iota(jnp.int32, (m, LANES), 1) < SB_HEAD_DIM
    mask2 = None if mask is None else jnp.concatenate([mask, mask], axis=0)
    kv = [kv_block(p) for p in range(HEAD_PAIRS)]

    logit, sps, rowsums = [], [], []
    for p in range(HEAD_PAIRS):
        z = (_dot if key_minor else _dot_nt)(qm_ref[p], kv[p][0])
        sp = jnp.maximum(z, 0.0) + jnp.log(1.0 + jnp.exp2(-jnp.abs(z))) * LOG2E
        spm = sp if mask2 is None else jnp.where(mask2, sp, 0.0)
        sps.append(spm.astype(BF16))
        logit.append(z - sp)
        rowsums.append(jnp.sum(spm, axis=1, keepdims=True))

    later = _dot(jnp.concatenate(sps, axis=0), upper)

    least = None
    for p in range(HEAD_PAIRS):
        a = jnp.exp2(logit[p] - later[2 * m * p:2 * m * (p + 1)])
        if mask2 is not None:
            a = jnp.where(mask2, a, 0.0)
        pv = (_dot_nt if key_minor else _dot)(a.astype(BF16), kv[p][1])
        merged = jnp.where(lower_half, pv[0:m], pv[m:2 * m])
        rowsum = jnp.where(lower_half, rowsums[p][0:m], rowsums[p][m:2 * m])
        if first:
            decay = rowsum
            acc_ref[p] = merged
        else:
            prev = decay_ref[p]
            decay = prev + rowsum
            acc_ref[p] += merged * jnp.exp2(-prev)
        decay_ref[p] = decay
        least = decay if least is None else jnp.minimum(least, decay)
    return jnp.min(least)


def _more_blocks(state):
    j, least = state
    return jnp.logical_and(j >= 0, least < SKIP_LOG2)


def _store_heads(acc_ref, o_ref):
    for p in range(HEAD_PAIRS):
        o_ref[0, :, p * LANES:(p + 1) * LANES] = acc_ref[p].astype(BF16)


def _attn_prompt_kernel(q_ref, k_ref, v_ref, o_ref, qm_ref, decay_ref, acc_ref, *, blk):
    qi = pl.program_id(1)
    _store_masked_queries(q_ref, qm_ref)
    upper = _strict_upper(blk)

    def kv_block(j):
        start = pl.multiple_of(j * blk, blk)
        return lambda p: (k_ref[0, pl.ds(start, blk), p * LANES:(p + 1) * LANES],
                          v_ref[0, pl.ds(start, blk), p * LANES:(p + 1) * LANES])

    least = _sb_all_heads(qm_ref, kv_block(qi), decay_ref, acc_ref, upper, _causal_mask(blk), True)

    def body(state):
        j, _ = state
        return j - 1, _sb_all_heads(qm_ref, kv_block(j), decay_ref, acc_ref, upper, None, False)

    lax.while_loop(_more_blocks, body, (qi - 1, least))
    _store_heads(acc_ref, o_ref)


def _attn_scratch(m):
    return [pltpu.VMEM((HEAD_PAIRS, 2 * m, LANES), BF16), pltpu.VMEM((HEAD_PAIRS, m, LANES), F32),
            pltpu.VMEM((HEAD_PAIRS, m, LANES), F32)]


def _attn_prompt(qkv):
    B, T, _ = qkv.shape
    blk = ATTN_BLOCK
    assert T % blk == 0
    kern = functools.partial(_attn_prompt_kernel, blk=blk)
    return pl.pallas_call(
        kern,
        grid=(B, T // blk),
        in_specs=[
            pl.BlockSpec((1, blk, SB_WIDTH), lambda b, i: (b, i, 0)),
            pl.BlockSpec((1, T, SB_WIDTH), lambda b, i: (b, 0, 1)),
            pl.BlockSpec((1, T, SB_WIDTH), lambda b, i: (b, 0, 2)),
        ],
        out_specs=pl.BlockSpec((1, blk, SB_WIDTH), lambda b, i: (b, i, 0)),
        out_shape=jax.ShapeDtypeStruct((B, T, SB_WIDTH), BF16),
        scratch_shapes=_attn_scratch(blk),
        compiler_params=pltpu.CompilerParams(
            dimension_semantics=("arbitrary", "arbitrary"), vmem_limit_bytes=VMEM_LIMIT),
        name="sb_attn_prompt",
    )(qkv, qkv, qkv)


def _attn_sample_kernel(q_ref, kn_ref, vn_ref, ck_ref, cv_ref, o_ref, qm_ref, decay_ref, acc_ref,
                        *, t_new, past, blk):
    _store_masked_queries(q_ref, qm_ref)
    new_kv = lambda p: (kn_ref[0, :, p * LANES:(p + 1) * LANES], vn_ref[0, :, p * LANES:(p + 1) * LANES])
    least = _sb_all_heads(qm_ref, new_kv, decay_ref, acc_ref, _strict_upper(t_new), _causal_mask(t_new), True)
    upper = _strict_upper(blk)

    def cache_kv(j):
        start = pl.multiple_of(j * blk, blk)
        return lambda p: (ck_ref[0, p * LANES:(p + 1) * LANES, pl.ds(start, blk)].astype(BF16),
                          cv_ref[0, p * LANES:(p + 1) * LANES, pl.ds(start, blk)].astype(BF16))

    def body(state):
        j, _ = state
        return j - 1, _sb_all_heads(qm_ref, cache_kv(j), decay_ref, acc_ref, upper, None, False, key_minor=True)

    lax.while_loop(_more_blocks, body, (past // blk - 1, least))
    _store_heads(acc_ref, o_ref)


def _attn_sample(qkv, cache_kt, cache_vt):
    B, t_new, _ = qkv.shape
    past = cache_kt.shape[2]
    blk = ATTN_BLOCK
    assert past % blk == 0
    kern = functools.partial(_attn_sample_kernel, t_new=t_new, past=past, blk=blk)
    return pl.pallas_call(
        kern,
        grid=(B,),
        in_specs=[
            pl.BlockSpec((1, t_new, SB_WIDTH), lambda b: (b, 0, 0)),
            pl.BlockSpec((1, t_new, SB_WIDTH), lambda b: (b, 0, 1)),
            pl.BlockSpec((1, t_new, SB_WIDTH), lambda b: (b, 0, 2)),
            pl.BlockSpec((1, SB_WIDTH, past), lambda b: (b, 0, 0)),
            pl.BlockSpec((1, SB_WIDTH, past), lambda b: (b, 0, 0)),
        ],
        out_specs=pl.BlockSpec((1, t_new, SB_WIDTH), lambda b: (b, 0, 0)),
        out_shape=jax.ShapeDtypeStruct((B, t_new, SB_WIDTH), BF16),
        scratch_shapes=_attn_scratch(t_new),
        compiler_params=pltpu.CompilerParams(
            dimension_semantics=("arbitrary",), vmem_limit_bytes=VMEM_LIMIT),
        name="sb_attn_sample",
    )(qkv, qkv, qkv, cache_kt, cache_vt)


def _layer_norm(x, g, b):
    mu = jnp.mean(x, axis=-1, keepdims=True)
    xc = x - mu
    var = jnp.mean(xc * xc, axis=-1, keepdims=True)
    return xc * lax.rsqrt(var + LN_EPS) * g + b


def _out_ffn_kernel(x_ref, a_ref, p_ref, wo_ref, g1_ref, b1_ref, wg_ref, wu_ref, wd_ref, g2_ref, b2_ref,
                    y_ref, acc_ref):
    mix = _dot(a_ref[...], wo_ref[0:SB_WIDTH, :]) + _dot(p_ref[...], wo_ref[SB_WIDTH:, :])
    x1 = _layer_norm(ALPHA * x_ref[...] + mix, g1_ref[...], b1_ref[...])
    x1b = x1.astype(BF16)
    for c in range(0, D_FF, FF_CHUNK):
        gate = _dot(x1b, wg_ref[:, c:c + FF_CHUNK])
        up = _dot(x1b, wu_ref[:, c:c + FF_CHUNK])
        hmid = (gate * jax.nn.sigmoid(gate) * up).astype(BF16)
        part = _dot(hmid, wd_ref[c:c + FF_CHUNK, :])
        if c == 0:
            acc_ref[...] = part
        else:
            acc_ref[...] += part
    y_ref[...] = _layer_norm(ALPHA * x1 + acc_ref[...], g2_ref[...], b2_ref[...])


def _out_ffn(x, a, p, wo_b, g1, b1, wg_b, wu_b, wd_b, g2, b2, *, tm):
    N = x.shape[0]
    assert N % tm == 0 and D_FF % FF_CHUNK == 0
    row = lambda width: pl.BlockSpec((tm, width), lambda i: (i, 0))
    const = lambda *shape: pl.BlockSpec(shape, lambda i: (0,) * len(shape), pipeline_mode=pl.Buffered(1))
    return pl.pallas_call(
        _out_ffn_kernel,
        grid=(N // tm,),
        in_specs=[
            row(D_MODEL), row(SB_WIDTH), row(POOL_WIDTH),
            const(SB_WIDTH + POOL_WIDTH, D_MODEL), const(1, D_MODEL), const(1, D_MODEL),
            const(D_MODEL, D_FF), const(D_MODEL, D_FF), const(D_FF, D_MODEL),
            const(1, D_MODEL), const(1, D_MODEL),
        ],
        out_specs=row(D_MODEL),
        out_shape=jax.ShapeDtypeStruct((N, D_MODEL), F32),
        scratch_shapes=[pltpu.VMEM((tm, D_MODEL), F32)],
        compiler_params=pltpu.CompilerParams(
            dimension_semantics=("arbitrary",), vmem_limit_bytes=VMEM_LIMIT),
        name="out_proj_ffn",
    )(x, a, p, wo_b, g1, b1, wg_b, wu_b, wd_b, g2, b2)


def kernel(x_prompt, x_sample, cache_k, cache_v, state_pool, w_in, pool_w, pool_scale, w_out,
           ln1_g, ln1_b, w_gate, w_up, w_down, ln2_g, ln2_b):
    Bp, Tp, _ = x_prompt.shape
    Bs, Ts, _ = x_sample.shape
    past = cache_k.shape[1]

    w_in_b = w_in.astype(BF16)
    pool_w_b = pool_w.astype(BF16)
    wo_b = w_out.astype(BF16)
    wg_b = w_gate.astype(BF16)
    wu_b = w_up.astype(BF16)
    wd_b = w_down.astype(BF16)
    row2 = lambda a: a.reshape(1, -1).astype(F32)
    pool_scale2 = row2(pool_scale)
    ffn_args = (wo_b, row2(ln1_g), row2(ln1_b), wg_b, wu_b, wd_b, row2(ln2_g), row2(ln2_b))

    hist_p = jnp.zeros((Bp, HIST_ROWS, POOL_WIDTH), F32)
    k_p, v_p, qkv_p, mix_p, pool_p = _in_proj(x_prompt, hist_p, w_in_b, pool_w_b, pool_scale2, tm=512, start_pos=0)
    a_p = _attn_prompt(qkv_p)
    y_p = _out_ffn(x_prompt.reshape(Bp * Tp, D_MODEL), a_p.reshape(Bp * Tp, SB_WIDTH),
                   mix_p.reshape(Bp * Tp, POOL_WIDTH), *ffn_args, tm=512)

    hist_s = jnp.concatenate([jnp.zeros((Bs, HIST_ROWS - POOL_HIST, POOL_WIDTH), F32), state_pool.astype(F32)], axis=1)
    k_s, v_s, qkv_s, mix_s, pool_s = _in_proj(x_sample, hist_s, w_in_b, pool_w_b, pool_scale2, tm=Ts, start_pos=past)
    feature_major = lambda c: jnp.transpose(c, (0, 2, 3, 1)).reshape(Bs, SB_WIDTH, past)
    a_s = _attn_sample(qkv_s, feature_major(cache_k), feature_major(cache_v))
    y_s = _out_ffn(x_sample.reshape(Bs * Ts, D_MODEL), a_s.reshape(Bs * Ts, SB_WIDTH),
                   mix_s.reshape(Bs * Ts, POOL_WIDTH), *ffn_args, tm=Bs * Ts)

    heads = lambda a: a.reshape(a.shape[0], a.shape[1], SB_HEADS, SB_HEAD_DIM)
    return (y_p.reshape(Bp, Tp, D_MODEL), y_s.reshape(Bs, Ts, D_MODEL),
            heads(k_p), heads(v_p), pool_p, heads(k_s), heads(v_s), pool_s)
```

```python
import functools

import jax
import jax.numpy as jnp
from jax import lax
from jax.experimental import pallas as pl
from jax.experimental.pallas import tpu as pltpu

D_MODEL = 1024
SB_WIDTH = 512
SB_HEADS = 8
SB_HEAD_DIM = 64
POOL_WIDTH = 512
POOL_WINDOWS = (2, 4, 8, 16)
POOL_GC = 128
POOL_HIST = 15
HIST_ROWS = 16
D_FF = 2816
DEPTH = 1
ALPHA = (2 * DEPTH) ** 0.25
LN_EPS = 1e-5
LANES = 128
HEAD_PAIRS = SB_WIDTH // LANES
LOG2E = 1.4426950408889634
Q_SCALE = SB_HEAD_DIM ** -0.5 * LOG2E
SKIP_LOG2 = 160.0
ATTN_BLOCK = 256
FF_CHUNK = 256
VMEM_LIMIT = 56 * 1024 * 1024

F32 = jnp.float32
BF16 = jnp.bfloat16


def _dot(a, b):
    return jnp.dot(a, b, preferred_element_type=F32)


def _dot_nt(a, b):
    return lax.dot_general(a, b, (((1,), (1,)), ((), ())), preferred_element_type=F32)


def _in_proj_kernel(x_ref, hist_ref, w_in_f32_ref, pool_w_ref, pool_scale_ref,
                    k_ref, v_ref, qkv_ref, p_ref, pool_ref, w_in_ref, u_ext, *, tm, start_pos, n_t):
    t = pl.program_id(1)

    @pl.when(jnp.logical_and(pl.program_id(0) == 0, t == 0))
    def _():
        w_in_ref[...] = w_in_f32_ref[...].astype(BF16)

    x = x_ref[0].astype(BF16)

    q = _dot(x, w_in_ref[:, 0:SB_WIDTH])
    qkv_ref[0, :, 0:SB_WIDTH] = (q * Q_SCALE).astype(BF16)
    k = _dot(x, w_in_ref[:, SB_WIDTH:2 * SB_WIDTH])
    k_ref[0] = k
    qkv_ref[0, :, SB_WIDTH:2 * SB_WIDTH] = k.astype(BF16)
    v = _dot(x, w_in_ref[:, 2 * SB_WIDTH:3 * SB_WIDTH])
    v_ref[0] = v
    qkv_ref[0, :, 2 * SB_WIDTH:3 * SB_WIDTH] = v.astype(BF16)
    u = _dot(x, w_in_ref[:, 3 * SB_WIDTH:])

    @pl.when(t == 0)
    def _():
        u_ext[0:HIST_ROWS, :] = hist_ref[0]

    @pl.when(t > 0)
    def _():
        u_ext[0:HIST_ROWS, :] = u_ext[tm:tm + HIST_ROWS, :]

    u_ext[HIST_ROWS:HIST_ROWS + tm, :] = u

    pos = start_pos + t * tm + lax.broadcasted_iota(jnp.int32, (tm, 1), 0)
    for g, w in enumerate(POOL_WINDOWS):
        cols = slice(g * POOL_GC, (g + 1) * POOL_GC)
        s = u_ext[:, cols]
        span = 1
        while span < w:
            s = s + pltpu.roll(s, span, 0)
            span *= 2
        cnt = jnp.minimum(w, pos + 1).astype(F32)
        d = s[HIST_ROWS:, :] / cnt - u[:, cols]
        y = _dot(d.astype(BF16), pool_w_ref[g].astype(BF16)) * pool_scale_ref[:, cols]
        p_ref[0, :, cols] = y.astype(BF16)

    @pl.when(t == n_t - 1)
    def _():
        pool_ref[0] = u_ext[tm + HIST_ROWS - POOL_HIST:tm + HIST_ROWS, :]


def _in_proj(x, hist, w_in, pool_w, pool_scale2, *, tm, start_pos):
    B, T, _ = x.shape
    assert T % tm == 0 and tm >= HIST_ROWS and T >= POOL_HIST
    n_t = T // tm
    kern = functools.partial(_in_proj_kernel, tm=tm, start_pos=start_pos, n_t=n_t)
    const = lambda *shape: pl.BlockSpec(shape, lambda b, t: (0,) * len(shape), pipeline_mode=pl.Buffered(1))
    return pl.pallas_call(
        kern,
        grid=(B, n_t),
        in_specs=[
            pl.BlockSpec((1, tm, D_MODEL), lambda b, t: (b, t, 0)),
            pl.BlockSpec((1, HIST_ROWS, POOL_WIDTH), lambda b, t: (b, 0, 0)),
            const(D_MODEL, 3 * SB_WIDTH + POOL_WIDTH),
            const(len(POOL_WINDOWS), POOL_GC, POOL_GC),
            const(1, POOL_WIDTH),
        ],
        out_specs=[
            pl.BlockSpec((1, tm, SB_WIDTH), lambda b, t: (b, t, 0)),
            pl.BlockSpec((1, tm, SB_WIDTH), lambda b, t: (b, t, 0)),
            pl.BlockSpec((1, tm, 3 * SB_WIDTH), lambda b, t: (b, t, 0)),
            pl.BlockSpec((1, tm, POOL_WIDTH), lambda b, t: (b, t, 0)),
            pl.BlockSpec((1, POOL_HIST, POOL_WIDTH), lambda b, t: (b, 0, 0)),
        ],
        out_shape=[
            jax.ShapeDtypeStruct((B, T, SB_WIDTH), F32),
            jax.ShapeDtypeStruct((B, T, SB_WIDTH), F32),
            jax.ShapeDtypeStruct((B, T, 3 * SB_WIDTH), BF16),
            jax.ShapeDtypeStruct((B, T, POOL_WIDTH), BF16),
            jax.ShapeDtypeStruct((B, POOL_HIST, POOL_WIDTH), F32),
        ],
        scratch_shapes=[pltpu.VMEM(w_in.shape, BF16), pltpu.VMEM((HIST_ROWS + tm, POOL_WIDTH), F32)],
        compiler_params=pltpu.CompilerParams(
            dimension_semantics=("arbitrary", "arbitrary"), vmem_limit_bytes=VMEM_LIMIT),
        name="in_proj_pool",
    )(x, hist, w_in, pool_w, pool_scale2)


def _strict_upper(n):
    r = lax.broadcasted_iota(jnp.int32, (n, n), 0)
    c = lax.broadcasted_iota(jnp.int32, (n, n), 1)
    return (r > c).astype(BF16)


def _causal_mask(n):
    r = lax.broadcasted_iota(jnp.int32, (n, n), 0)
    c = lax.broadcasted_iota(jnp.int32, (n, n), 1)
    return c < r


def _store_masked_queries(q_ref, qm_ref):
    m = q_ref.shape[1]
    lower_half = lax.broadcasted_iota(jnp.int32, (m, LANES), 1) < SB_HEAD_DIM
    for p in range(HEAD_PAIRS):
        q2 = q_ref[0, :, p * LANES:(p + 1) * LANES]
        zero = jnp.zeros_like(q2)
        qm_ref[p, 0:m, :] = jnp.where(lower_half, q2, zero)
        qm_ref[p, m:2 * m, :] = jnp.where(lower_half, zero, q2)


def _sb_all_heads(qm_ref, kv_block, decay_ref, acc_ref, upper, mask, first, key_minor=False):
    m = qm_ref.shape[1] // 2
    lower_half = lax.broadcasted_iota(jnp.int32, (m, LANES), 1) < SB_HEAD_DIM
    mask2 = None if mask is None else jnp.concatenate([mask, mask], axis=0)
    kv = [kv_block(p) for p in range(HEAD_PAIRS)]

    logit, sps, rowsums = [], [], []
    for p in range(HEAD_PAIRS):
        z = (_dot if key_minor else _dot_nt)(qm_ref[p], kv[p][0])
        sp = jnp.maximum(z, 0.0) + jnp.log(1.0 + jnp.exp2(-jnp.abs(z))) * LOG2E
        spm = sp if mask2 is None else jnp.where(mask2, sp, 0.0)
        sps.append(spm.astype(BF16))
        logit.append(z - sp)
        rowsums.append(jnp.sum(spm, axis=1, keepdims=True))

    later = _dot(jnp.concatenate(sps, axis=0), upper)

    least = None
    for p in range(HEAD_PAIRS):
        a = jnp.exp2(logit[p] - later[2 * m * p:2 * m * (p + 1)])
        if mask2 is not None:
            a = jnp.where(mask2, a, 0.0)
        pv = (_dot_nt if key_minor else _dot)(a.astype(BF16), kv[p][1])
        merged = jnp.where(lower_half, pv[0:m], pv[m:2 * m])
        rowsum = jnp.where(lower_half, rowsums[p][0:m], rowsums[p][m:2 * m])
        if first:
            decay = rowsum
            acc_ref[p] = merged
        else:
            prev = decay_ref[p]
            decay = prev + rowsum
            acc_ref[p] += merged * jnp.exp2(-prev)
        decay_ref[p] = decay
        least = decay if least is None else jnp.minimum(least, decay)
    return jnp.min(least)


def _more_blocks(state):
    j, least = state
    return jnp.logical_and(j >= 0, least < SKIP_LOG2)


def _store_heads(acc_ref, o_ref):
    for p in range(HEAD_PAIRS):
        o_ref[0, :, p * LANES:(p + 1) * LANES] = acc_ref[p].astype(BF16)


def _attn_prompt_kernel(q_ref, k_ref, v_ref, o_ref, qm_ref, decay_ref, acc_ref, *, blk):
    qi = pl.program_id(1)
    _store_masked_queries(q_ref, qm_ref)
    upper = _strict_upper(blk)

    def kv_block(j):
        start = pl.multiple_of(j * blk, blk)
        return lambda p: (k_ref[0, pl.ds(start, blk), p * LANES:(p + 1) * LANES],
                          v_ref[0, pl.ds(start, blk), p * LANES:(p + 1) * LANES])

    least = _sb_all_heads(qm_ref, kv_block(qi), decay_ref, acc_ref, upper, _causal_mask(blk), True)

    def body(state):
        j, _ = state
        return j - 1, _sb_all_heads(qm_ref, kv_block(j), decay_ref, acc_ref, upper, None, False)

    lax.while_loop(_more_blocks, body, (qi - 1, least))
    _store_heads(acc_ref, o_ref)


def _attn_scratch(m):
    return [pltpu.VMEM((HEAD_PAIRS, 2 * m, LANES), BF16), pltpu.VMEM((HEAD_PAIRS, m, LANES), F32),
            pltpu.VMEM((HEAD_PAIRS, m, LANES), F32)]


def _attn_prompt(qkv):
    B, T, _ = qkv.shape
    blk = ATTN_BLOCK
    assert T % blk == 0
    kern = functools.partial(_attn_prompt_kernel, blk=blk)
    return pl.pallas_call(
        kern,
        grid=(B, T // blk),
        in_specs=[
            pl.BlockSpec((1, blk, SB_WIDTH), lambda b, i: (b, i, 0)),
            pl.BlockSpec((1, T, SB_WIDTH), lambda b, i: (b, 0, 1)),
            pl.BlockSpec((1, T, SB_WIDTH), lambda b, i: (b, 0, 2)),
        ],
        out_specs=pl.BlockSpec((1, blk, SB_WIDTH), lambda b, i: (b, i, 0)),
        out_shape=jax.ShapeDtypeStruct((B, T, SB_WIDTH), BF16),
        scratch_shapes=_attn_scratch(blk),
        compiler_params=pltpu.CompilerParams(
            dimension_semantics=("arbitrary", "arbitrary"), vmem_limit_bytes=VMEM_LIMIT),
        name="sb_attn_prompt",
    )(qkv, qkv, qkv)


def _attn_sample_kernel(q_ref, kn_ref, vn_ref, ck_ref, cv_ref, o_ref, qm_ref, decay_ref, acc_ref,
                        *, t_new, past, blk):
    _store_masked_queries(q_ref, qm_ref)
    new_kv = lambda p: (kn_ref[0, :, p * LANES:(p + 1) * LANES], vn_ref[0, :, p * LANES:(p + 1) * LANES])
    least = _sb_all_heads(qm_ref, new_kv, decay_ref, acc_ref, _strict_upper(t_new), _causal_mask(t_new), True)
    upper = _strict_upper(blk)

    def cache_kv(j):
        start = pl.multiple_of(j * blk, blk)
        return lambda p: (ck_ref[0, p * LANES:(p + 1) * LANES, pl.ds(start, blk)].astype(BF16),
                          cv_ref[0, p * LANES:(p + 1) * LANES, pl.ds(start, blk)].astype(BF16))

    def body(state):
        j, _ = state
        return j - 1, _sb_all_heads(qm_ref, cache_kv(j), decay_ref, acc_ref, upper, None, False, key_minor=True)

    lax.while_loop(_more_blocks, body, (past // blk - 1, least))
    _store_heads(acc_ref, o_ref)


def _attn_sample(qkv, cache_kt, cache_vt):
    B, t_new, _ = qkv.shape
    past = cache_kt.shape[2]
    blk = ATTN_BLOCK
    assert past % blk == 0
    kern = functools.partial(_attn_sample_kernel, t_new=t_new, past=past, blk=blk)
    return pl.pallas_call(
        kern,
        grid=(B,),
        in_specs=[
            pl.BlockSpec((1, t_new, SB_WIDTH), lambda b: (b, 0, 0)),
            pl.BlockSpec((1, t_new, SB_WIDTH), lambda b: (b, 0, 1)),
            pl.BlockSpec((1, t_new, SB_WIDTH), lambda b: (b, 0, 2)),
            pl.BlockSpec((1, SB_WIDTH, past), lambda b: (b, 0, 0)),
            pl.BlockSpec((1, SB_WIDTH, past), lambda b: (b, 0, 0)),
        ],
        out_specs=pl.BlockSpec((1, t_new, SB_WIDTH), lambda b: (b, 0, 0)),
        out_shape=jax.ShapeDtypeStruct((B, t_new, SB_WIDTH), BF16),
        scratch_shapes=_attn_scratch(t_new),
        compiler_params=pltpu.CompilerParams(
            dimension_semantics=("arbitrary",), vmem_limit_bytes=VMEM_LIMIT),
        name="sb_attn_sample",
    )(qkv, qkv, qkv, cache_kt, cache_vt)


def _layer_norm(x, g, b):
    mu = jnp.mean(x, axis=-1, keepdims=True)
    xc = x - mu
    var = jnp.mean(xc * xc, axis=-1, keepdims=True)
    return xc * lax.rsqrt(var + LN_EPS) * g + b


def _ffn_rows(x, a, p, wo_ref, g1, b1, wg_ref, wu_ref, wd_ref, g2, b2, acc_ref):
    mix = _dot(a, wo_ref[0:SB_WIDTH, :]) + _dot(p, wo_ref[SB_WIDTH:, :])
    x1 = _layer_norm(ALPHA * x + mix, g1, b1)
    x1b = x1.astype(BF16)
    for c in range(0, D_FF, FF_CHUNK):
        gate = _dot(x1b, wg_ref[:, c:c + FF_CHUNK])
        up = _dot(x1b, wu_ref[:, c:c + FF_CHUNK])
        hmid = (gate * jax.nn.sigmoid(gate) * up).astype(BF16)
        part = _dot(hmid, wd_ref[c:c + FF_CHUNK, :])
        if c == 0:
            acc_ref[...] = part
        else:
            acc_ref[...] += part
    return _layer_norm(ALPHA * x1 + acc_ref[...], g2, b2)


def _out_ffn_kernel(xp_ref, ap_ref, pp_ref, xs_ref, as_ref, ps_ref, wo_c, wg_c, wu_c, wd_c,
                    g1_ref, b1_ref, g2_ref, b2_ref, yp_ref, ys_ref,
                    wo_s, wg_s, wu_s, wd_s, accp_ref, accs_ref, *, n_conv, n_tiles):
    i = pl.program_id(0)

    @pl.when(i < n_conv)
    def _():
        for c_ref, s_ref in ((wo_c, wo_s), (wg_c, wg_s), (wu_c, wu_s), (wd_c, wd_s)):
            rows = c_ref.shape[0]
            s_ref[pl.ds(pl.multiple_of(i * rows, rows), rows), :] = c_ref[...].astype(BF16)

    norms = (g1_ref, b1_ref, g2_ref, b2_ref)

    def rows(x_ref, a_ref, p_ref, acc_ref):
        g1, b1, g2, b2 = (r[...] for r in norms)
        return _ffn_rows(x_ref[...], a_ref[...], p_ref[...], wo_s, g1, b1, wg_s, wu_s, wd_s, g2, b2, acc_ref)

    @pl.when(jnp.logical_and(i >= n_conv, i < n_conv + n_tiles))
    def _():
        yp_ref[...] = rows(xp_ref, ap_ref, pp_ref, accp_ref)

    @pl.when(i == n_conv + n_tiles)
    def _():
        ys_ref[...] = rows(xs_ref, as_ref, ps_ref, accs_ref)


def _out_ffn(xp, ap, pp, xs, a_s, ps, w_out, w_gate, w_up, w_down, g1, b1, g2, b2, *, tm, n_conv):
    Np, Ns = xp.shape[0], xs.shape[0]
    assert Np % tm == 0 and D_FF % FF_CHUNK == 0
    n_tiles = Np // tm
    tile = lambda i: jnp.clip(i - n_conv, 0, n_tiles - 1)
    row = lambda width: pl.BlockSpec((tm, width), lambda i: (tile(i), 0))
    once = lambda *shape: pl.BlockSpec(shape, lambda i: (0,) * len(shape), pipeline_mode=pl.Buffered(1))

    def chunked(w):
        assert w.shape[0] % (n_conv * 16) == 0
        return pl.BlockSpec((w.shape[0] // n_conv, w.shape[1]), lambda i: (jnp.minimum(i, n_conv - 1), 0))

    kern = functools.partial(_out_ffn_kernel, n_conv=n_conv, n_tiles=n_tiles)
    return pl.pallas_call(
        kern,
        grid=(n_conv + n_tiles + 1,),
        in_specs=[
            row(D_MODEL), row(SB_WIDTH), row(POOL_WIDTH),
            once(Ns, D_MODEL), once(Ns, SB_WIDTH), once(Ns, POOL_WIDTH),
            chunked(w_out), chunked(w_gate), chunked(w_up), chunked(w_down),
            once(1, D_MODEL), once(1, D_MODEL), once(1, D_MODEL), once(1, D_MODEL),
        ],
        out_specs=[row(D_MODEL), pl.BlockSpec((Ns, D_MODEL), lambda i: (0, 0))],
        out_shape=[jax.ShapeDtypeStruct((Np, D_MODEL), F32), jax.ShapeDtypeStruct((Ns, D_MODEL), F32)],
        scratch_shapes=[pltpu.VMEM(w.shape, BF16) for w in (w_out, w_gate, w_up, w_down)]
        + [pltpu.VMEM((tm, D_MODEL), F32), pltpu.VMEM((Ns, D_MODEL), F32)],
        compiler_params=pltpu.CompilerParams(
            dimension_semantics=("arbitrary",), vmem_limit_bytes=VMEM_LIMIT),
        name="out_proj_ffn",
    )(xp, ap, pp, xs, a_s, ps, w_out, w_gate, w_up, w_down, g1, b1, g2, b2)


def kernel(x_prompt, x_sample, cache_k, cache_v, state_pool, w_in, pool_w, pool_scale, w_out,
           ln1_g, ln1_b, w_gate, w_up, w_down, ln2_g, ln2_b):
    Bp, Tp, _ = x_prompt.shape
    Bs, Ts, _ = x_sample.shape
    past = cache_k.shape[1]

    row2 = lambda a: a.reshape(1, -1).astype(F32)
    pool_scale2 = row2(pool_scale)

    hist_p = jnp.zeros((Bp, HIST_ROWS, POOL_WIDTH), F32)
    k_p, v_p, qkv_p, mix_p, pool_p = _in_proj(x_prompt, hist_p, w_in, pool_w, pool_scale2, tm=512, start_pos=0)
    a_p = _attn_prompt(qkv_p)

    hist_s = jnp.concatenate([jnp.zeros((Bs, HIST_ROWS - POOL_HIST, POOL_WIDTH), F32), state_pool.astype(F32)], axis=1)
    k_s, v_s, qkv_s, mix_s, pool_s = _in_proj(x_sample, hist_s, w_in, pool_w, pool_scale2, tm=Ts, start_pos=past)
    feature_major = lambda c: jnp.transpose(c, (0, 2, 3, 1)).reshape(Bs, SB_WIDTH, past)
    a_s = _attn_sample(qkv_s, feature_major(cache_k), feature_major(cache_v))

    y_p, y_s = _out_ffn(
        x_prompt.reshape(Bp * Tp, D_MODEL), a_p.reshape(Bp * Tp, SB_WIDTH), mix_p.reshape(Bp * Tp, POOL_WIDTH),
        x_sample.reshape(Bs * Ts, D_MODEL), a_s.reshape(Bs * Ts, SB_WIDTH), mix_s.reshape(Bs * Ts, POOL_WIDTH),
        w_out, w_gate, w_up, w_down, row2(ln1_g), row2(ln1_b), row2(ln2_g), row2(ln2_b), tm=512, n_conv=16)

    heads = lambda a: a.reshape(a.shape[0], a.shape[1], SB_HEADS, SB_HEAD_DIM)
    return (y_p.reshape(Bp, Tp, D_MODEL), y_s.reshape(Bs, Ts, D_MODEL),
            heads(k_p), heads(v_p), pool_p, heads(k_s), heads(v_s), pool_s)
```

```python
import functools

import jax
import jax.numpy as jnp
from jax import lax
from jax.experimental import pallas as pl
from jax.experimental.pallas import tpu as pltpu

D_MODEL = 1024
SB_WIDTH = 512
SB_HEADS = 8
SB_HEAD_DIM = 64
POOL_WIDTH = 512
POOL_WINDOWS = (2, 4, 8, 16)
POOL_GC = 128
POOL_HIST = 15
HIST_ROWS = 16
D_FF = 2816
DEPTH = 1
ALPHA = (2 * DEPTH) ** 0.25
LN_EPS = 1e-5
LANES = 128
HEAD_PAIRS = SB_WIDTH // LANES
LOG2E = 1.4426950408889634
Q_SCALE = SB_HEAD_DIM ** -0.5 * LOG2E
SKIP_LOG2 = 160.0
ATTN_BLOCK = 256
FF_CHUNK = 256
VMEM_LIMIT = 56 * 1024 * 1024

F32 = jnp.float32
BF16 = jnp.bfloat16


def _dot(a, b):
    return jnp.dot(a, b, preferred_element_type=F32)


def _dot_nt(a, b):
    return lax.dot_general(a, b, (((1,), (1,)), ((), ())), preferred_element_type=F32)


Q_COLS, K_COLS, V_COLS, U_COLS = (SB_WIDTH * i for i in range(4))


def _proj_cols(x, w_in_ref, start, width):
    return _dot(x, w_in_ref[:, start:start + width])


def _pool_mix(u, hist, pool_w_ref, pool_scale_ref, u_ext, t, *, tm, start_pos):
    u_ext[0:HIST_ROWS, :] = jnp.where(t == 0, hist, u_ext[tm:tm + HIST_ROWS, :])
    u_ext[HIST_ROWS:HIST_ROWS + tm, :] = u

    pos = start_pos + t * tm + lax.broadcasted_iota(jnp.int32, (tm, 1), 0)
    pooled = []
    for g, w in enumerate(POOL_WINDOWS):
        cols = slice(g * POOL_GC, (g + 1) * POOL_GC)
        s = u_ext[:, cols]
        span = 1
        while span < w:
            s = s + pltpu.roll(s, span, 0)
            span *= 2
        cnt = jnp.minimum(w, pos + 1).astype(F32)
        d = s[HIST_ROWS:, :] / cnt - u[:, cols]
        y = _dot(d.astype(BF16), pool_w_ref[g].astype(BF16)) * pool_scale_ref[:, cols]
        pooled.append(y.astype(BF16))
    return pooled


def _pool_state(u_ext, tm):
    return u_ext[tm + HIST_ROWS - POOL_HIST:tm + HIST_ROWS, :]


def _in_proj_kernel(x_ref, hist_ref, w_in_f32_ref, pool_w_ref, pool_scale_ref,
                    k_ref, v_ref, qkv_ref, p_ref, pool_ref, w_in_ref, u_ext, *, tm, start_pos):
    t = pl.program_id(1)

    @pl.when(jnp.logical_and(pl.program_id(0) == 0, t == 0))
    def _():
        w_in_ref[...] = w_in_f32_ref[...].astype(BF16)
        u_ext[...] = jnp.zeros_like(u_ext)

    x = x_ref[0].astype(BF16)
    qkv_ref[0, :, 0:SB_WIDTH] = (_proj_cols(x, w_in_ref, Q_COLS, SB_WIDTH) * Q_SCALE).astype(BF16)
    k = _proj_cols(x, w_in_ref, K_COLS, SB_WIDTH)
    k_ref[0] = k
    qkv_ref[0, :, SB_WIDTH:2 * SB_WIDTH] = k.astype(BF16)
    v = _proj_cols(x, w_in_ref, V_COLS, SB_WIDTH)
    v_ref[0] = v
    qkv_ref[0, :, 2 * SB_WIDTH:3 * SB_WIDTH] = v.astype(BF16)
    u = _proj_cols(x, w_in_ref, U_COLS, POOL_WIDTH)
    pooled = _pool_mix(u, hist_ref[0], pool_w_ref, pool_scale_ref, u_ext, t, tm=tm, start_pos=start_pos)
    for g, y in enumerate(pooled):
        p_ref[0, :, g * POOL_GC:(g + 1) * POOL_GC] = y
    pool_ref[0] = _pool_state(u_ext, tm)


def _in_proj(x, hist, w_in, pool_w, pool_scale2, *, tm, start_pos):
    B, T, _ = x.shape
    assert T % tm == 0 and tm >= HIST_ROWS and T >= POOL_HIST
    kern = functools.partial(_in_proj_kernel, tm=tm, start_pos=start_pos)
    const = lambda *shape: pl.BlockSpec(shape, lambda b, t: (0,) * len(shape), pipeline_mode=pl.Buffered(1))
    return pl.pallas_call(
        kern,
        grid=(B, T // tm),
        in_specs=[
            pl.BlockSpec((1, tm, D_MODEL), lambda b, t: (b, t, 0)),
            pl.BlockSpec((1, HIST_ROWS, POOL_WIDTH), lambda b, t: (b, 0, 0)),
            const(D_MODEL, 3 * SB_WIDTH + POOL_WIDTH),
            const(len(POOL_WINDOWS), POOL_GC, POOL_GC),
            const(1, POOL_WIDTH),
        ],
        out_specs=[
            pl.BlockSpec((1, tm, SB_WIDTH), lambda b, t: (b, t, 0)),
            pl.BlockSpec((1, tm, SB_WIDTH), lambda b, t: (b, t, 0)),
            pl.BlockSpec((1, tm, 3 * SB_WIDTH), lambda b, t: (b, t, 0)),
            pl.BlockSpec((1, tm, POOL_WIDTH), lambda b, t: (b, t, 0)),
            pl.BlockSpec((1, POOL_HIST, POOL_WIDTH), lambda b, t: (b, 0, 0)),
        ],
        out_shape=[
            jax.ShapeDtypeStruct((B, T, SB_WIDTH), F32),
            jax.ShapeDtypeStruct((B, T, SB_WIDTH), F32),
            jax.ShapeDtypeStruct((B, T, 3 * SB_WIDTH), BF16),
            jax.ShapeDtypeStruct((B, T, POOL_WIDTH), BF16),
            jax.ShapeDtypeStruct((B, POOL_HIST, POOL_WIDTH), F32),
        ],
        scratch_shapes=[pltpu.VMEM(w_in.shape, BF16), pltpu.VMEM((HIST_ROWS + tm, POOL_WIDTH), F32)],
        compiler_params=pltpu.CompilerParams(
            dimension_semantics=("arbitrary", "arbitrary"), vmem_limit_bytes=VMEM_LIMIT),
        name="in_proj_pool",
    )(x, hist, w_in, pool_w, pool_scale2)


def _strict_upper(n):
    r = lax.broadcasted_iota(jnp.int32, (n, n), 0)
    c = lax.broadcasted_iota(jnp.int32, (n, n), 1)
    return (r > c).astype(BF16)


def _causal_mask(n):
    r = lax.broadcasted_iota(jnp.int32, (n, n), 0)
    c = lax.broadcasted_iota(jnp.int32, (n, n), 1)
    return c < r


def _store_masked_queries(q, qm_ref):
    m = q.shape[0]
    lower_half = lax.broadcasted_iota(jnp.int32, (m, LANES), 1) < SB_HEAD_DIM
    for p in range(HEAD_PAIRS):
        q2 = q[:, p * LANES:(p + 1) * LANES]
        zero = jnp.zeros_like(q2)
        qm_ref[p, 0:m, :] = jnp.where(lower_half, q2, zero)
        qm_ref[p, m:2 * m, :] = jnp.where(lower_half, zero, q2)


def _sb_all_heads(qm_ref, kv_block, decay_ref, acc_ref, upper, mask, first, key_minor=False, extra_decay=None,
                  fillers=()):
    fillers = list(fillers)
    m = qm_ref.shape[1] // 2
    lower_half = lax.broadcasted_iota(jnp.int32, (m, LANES), 1) < SB_HEAD_DIM
    mask2 = None if mask is None else jnp.concatenate([mask, mask], axis=0)
    kv = [kv_block(p) for p in range(HEAD_PAIRS)]

    logit, sps, rowsums = [], [], []
    for p in range(HEAD_PAIRS):
        z = (_dot if key_minor else _dot_nt)(qm_ref[p], kv[p][0])
        sp = jnp.maximum(z, 0.0) + jnp.log(1.0 + jnp.exp2(-jnp.abs(z))) * LOG2E
        spm = sp if mask2 is None else jnp.where(mask2, sp, 0.0)
        sps.append(spm.astype(BF16))
        logit.append(z - sp)
        rowsums.append(jnp.sum(spm, axis=1, keepdims=True))
        if fillers:
            fillers.pop(0)()

    later = _dot(jnp.concatenate(sps, axis=0), upper)

    least = None
    for p in range(HEAD_PAIRS):
        a = jnp.exp2(logit[p] - later[2 * m * p:2 * m * (p + 1)])
        if mask2 is not None:
            a = jnp.where(mask2, a, 0.0)
        pv = (_dot_nt if key_minor else _dot)(a.astype(BF16), kv[p][1])
        merged = jnp.where(lower_half, pv[0:m], pv[m:2 * m])
        rowsum = jnp.where(lower_half, rowsums[p][0:m], rowsums[p][m:2 * m])
        if first:
            decay = rowsum
            acc_ref[p] = merged
        else:
            prev = decay_ref[p]
            if extra_decay is not None:
                prev = prev + extra_decay
            decay = prev + rowsum
            acc_ref[p] += merged * jnp.exp2(-prev)
        decay_ref[p] = decay
        least = decay if least is None else jnp.minimum(least, decay)
    return jnp.min(least)


def _more_blocks(state):
    j, least = state
    return jnp.logical_and(j >= 0, least < SKIP_LOG2)


def _store_heads(acc_ref, o_ref):
    for p in range(HEAD_PAIRS):
        o_ref[0, :, p * LANES:(p + 1) * LANES] = acc_ref[p].astype(BF16)


def _attn_scratch(m):
    return [pltpu.VMEM((HEAD_PAIRS, 2 * m, LANES), BF16), pltpu.VMEM((HEAD_PAIRS, m, LANES), F32),
            pltpu.VMEM((HEAD_PAIRS, m, LANES), F32)]


def _proj_attn_kernel(x_ref, hist_ref, w_in_f32_ref, pool_w_ref, pool_scale_ref,
                      k_ref, v_ref, p_ref, pool_ref, a_ref,
                      w_in_ref, u_ext, xb_ref, stage_q, stage_kv, kv_all, qm_ref, decay_ref, acc_ref,
                      *, tm, n_t, n_tiles):
    s = pl.program_id(0)

    @pl.when(s == 0)
    def _():
        w_in_ref[...] = w_in_f32_ref[...].astype(BF16)
        stage_q[...] = jnp.zeros_like(stage_q)
        stage_kv[...] = jnp.zeros_like(stage_kv)
        u_ext[...] = jnp.zeros_like(u_ext)

    upper = _strict_upper(tm)
    qi = lax.rem(jnp.maximum(s - 1, 0), n_t)

    def kv_block(j):
        start = pl.multiple_of(j * tm, tm)
        return lambda p: (kv_all[pl.ds(start, tm), p * LANES:(p + 1) * LANES],
                          kv_all[pl.ds(start, tm), SB_WIDTH + p * LANES:SB_WIDTH + (p + 1) * LANES])

    def take_stage():
        kv_all[pl.ds(pl.multiple_of(qi * tm, tm), tm), :] = stage_kv[...]
        _store_masked_queries(stage_q[...], qm_ref)

    def attend_near(fillers):
        _sb_all_heads(qm_ref, kv_block(qi), decay_ref, acc_ref, upper, _causal_mask(tm), True,
                      fillers=fillers[0:HEAD_PAIRS])
        void = jnp.where(qi == 0, 2.0 * SKIP_LOG2, 0.0)
        return _sb_all_heads(qm_ref, kv_block(jnp.maximum(qi - 1, 0)), decay_ref, acc_ref, upper, None, False,
                             extra_decay=void, fillers=fillers[HEAD_PAIRS:2 * HEAD_PAIRS])

    def attend_far(least):
        def body(state):
            j, _ = state
            return j - 1, _sb_all_heads(qm_ref, kv_block(j), decay_ref, acc_ref, upper, None, False)

        lax.while_loop(_more_blocks, body, (qi - 2, least))
        _store_heads(acc_ref, a_ref)

    half = SB_WIDTH // 2

    def project_q(c):
        def piece():
            q = _proj_cols(xb_ref[...], w_in_ref, Q_COLS + c, half) * Q_SCALE
            stage_q[:, c:c + half] = q.astype(BF16)
        return piece

    def project_kv(out_ref, w_cols, stage_cols, c):
        def piece():
            y = _proj_cols(xb_ref[...], w_in_ref, w_cols + c, half)
            out_ref[0, :, c:c + half] = y
            stage_kv[:, stage_cols + c:stage_cols + c + half] = y.astype(BF16)
        return piece

    u_halves = []

    def project_u0():
        u_halves.append(_proj_cols(xb_ref[...], w_in_ref, U_COLS, half))

    def project_u1_pool():
        u_halves.append(_proj_cols(xb_ref[...], w_in_ref, U_COLS + half, half))
        pooled = _pool_mix(jnp.concatenate(u_halves, axis=1), hist_ref[0], pool_w_ref, pool_scale_ref, u_ext,
                           lax.rem(s, n_t), tm=tm, start_pos=0)
        for g, y in enumerate(pooled):
            p_ref[0, :, g * POOL_GC:(g + 1) * POOL_GC] = y
        pool_ref[0] = _pool_state(u_ext, tm)

    @pl.when(s < n_tiles)
    def _():
        take_stage()
        xb_ref[...] = x_ref[0].astype(BF16)
        attend_far(attend_near([
            project_q(0), project_q(half),
            project_kv(k_ref, K_COLS, 0, 0), project_kv(k_ref, K_COLS, 0, half),
            project_kv(v_ref, V_COLS, SB_WIDTH, 0), project_kv(v_ref, V_COLS, SB_WIDTH, half),
            project_u0, project_u1_pool]))

    @pl.when(s == n_tiles)
    def _():
        take_stage()
        attend_far(attend_near([]))


def _proj_attn_prompt(x, hist, w_in, pool_w, pool_scale2):
    B, T, _ = x.shape
    tm = ATTN_BLOCK
    assert T % tm == 0 and T >= POOL_HIST
    n_t = T // tm
    n_tiles = B * n_t
    kern = functools.partial(_proj_attn_kernel, tm=tm, n_t=n_t, n_tiles=n_tiles)
    proj = lambda s: jnp.minimum(s, n_tiles - 1)
    attn = lambda s: jnp.maximum(s - 1, 0)
    const = lambda *shape: pl.BlockSpec(shape, lambda s: (0,) * len(shape), pipeline_mode=pl.Buffered(1))
    proj_rows = lambda width: pl.BlockSpec((1, tm, width), lambda s: (proj(s) // n_t, proj(s) % n_t, 0))
    return pl.pallas_call(
        kern,
        grid=(n_tiles + 1,),
        in_specs=[
            proj_rows(D_MODEL),
            pl.BlockSpec((1, HIST_ROWS, POOL_WIDTH), lambda s: (proj(s) // n_t, 0, 0)),
            const(D_MODEL, 3 * SB_WIDTH + POOL_WIDTH),
            const(len(POOL_WINDOWS), POOL_GC, POOL_GC),
            const(1, POOL_WIDTH),
        ],
        out_specs=[
            proj_rows(SB_WIDTH),
            proj_rows(SB_WIDTH),
            proj_rows(POOL_WIDTH),
            pl.BlockSpec((1, POOL_HIST, POOL_WIDTH), lambda s: (proj(s) // n_t, 0, 0)),
            pl.BlockSpec((1, tm, SB_WIDTH), lambda s: (attn(s) // n_t, attn(s) % n_t, 0)),
        ],
        out_shape=[
            jax.ShapeDtypeStruct((B, T, SB_WIDTH), F32),
            jax.ShapeDtypeStruct((B, T, SB_WIDTH), F32),
            jax.ShapeDtypeStruct((B, T, POOL_WIDTH), BF16),
            jax.ShapeDtypeStruct((B, POOL_HIST, POOL_WIDTH), F32),
            jax.ShapeDtypeStruct((B, T, SB_WIDTH), BF16),
        ],
        scratch_shapes=[
            pltpu.VMEM(w_in.shape, BF16),
            pltpu.VMEM((HIST_ROWS + tm, POOL_WIDTH), F32),
            pltpu.VMEM((tm, D_MODEL), BF16),
            pltpu.VMEM((tm, SB_WIDTH), BF16),
            pltpu.VMEM((tm, 2 * SB_WIDTH), BF16),
            pltpu.VMEM((T, 2 * SB_WIDTH), BF16),
        ] + _attn_scratch(tm),
        compiler_params=pltpu.CompilerParams(
            dimension_semantics=("arbitrary",), vmem_limit_bytes=VMEM_LIMIT),
        name="proj_attn_prompt",
    )(x, hist, w_in, pool_w, pool_scale2)


def _attn_sample_kernel(q_ref, kn_ref, vn_ref, ck_ref, cv_ref, o_ref, qm_ref, decay_ref, acc_ref,
                        *, t_new, past, blk):
    _store_masked_queries(q_ref[0], qm_ref)
    new_kv = lambda p: (kn_ref[0, :, p * LANES:(p + 1) * LANES], vn_ref[0, :, p * LANES:(p + 1) * LANES])
    least = _sb_all_heads(qm_ref, new_kv, decay_ref, acc_ref, _strict_upper(t_new), _causal_mask(t_new), True)
    upper = _strict_upper(blk)

    def cache_kv(j):
        start = pl.multiple_of(j * blk, blk)
        return lambda p: (ck_ref[0, p * LANES:(p + 1) * LANES, pl.ds(start, blk)].astype(BF16),
                          cv_ref[0, p * LANES:(p + 1) * LANES, pl.ds(start, blk)].astype(BF16))

    def body(state):
        j, _ = state
        return j - 1, _sb_all_heads(qm_ref, cache_kv(j), decay_ref, acc_ref, upper, None, False, key_minor=True)

    lax.while_loop(_more_blocks, body, (past // blk - 1, least))
    _store_heads(acc_ref, o_ref)


def _attn_sample(qkv, cache_kt, cache_vt):
    B, t_new, _ = qkv.shape
    past = cache_kt.shape[2]
    blk = ATTN_BLOCK
    assert past % blk == 0
    kern = functools.partial(_attn_sample_kernel, t_new=t_new, past=past, blk=blk)
    return pl.pallas_call(
        kern,
        grid=(B,),
        in_specs=[
            pl.BlockSpec((1, t_new, SB_WIDTH), lambda b: (b, 0, 0)),
            pl.BlockSpec((1, t_new, SB_WIDTH), lambda b: (b, 0, 1)),
            pl.BlockSpec((1, t_new, SB_WIDTH), lambda b: (b, 0, 2)),
            pl.BlockSpec((1, SB_WIDTH, past), lambda b: (b, 0, 0)),
            pl.BlockSpec((1, SB_WIDTH, past), lambda b: (b, 0, 0)),
        ],
        out_specs=pl.BlockSpec((1, t_new, SB_WIDTH), lambda b: (b, 0, 0)),
        out_shape=jax.ShapeDtypeStruct((B, t_new, SB_WIDTH), BF16),
        scratch_shapes=_attn_scratch(t_new),
        compiler_params=pltpu.CompilerParams(
            dimension_semantics=("arbitrary",), vmem_limit_bytes=VMEM_LIMIT),
        name="sb_attn_sample",
    )(qkv, qkv, qkv, cache_kt, cache_vt)


def _layer_norm(x, g, b):
    mu = jnp.mean(x, axis=-1, keepdims=True)
    xc = x - mu
    var = jnp.mean(xc * xc, axis=-1, keepdims=True)
    return xc * lax.rsqrt(var + LN_EPS) * g + b


def _ffn_rows(x, a, p, wo_ref, g1, b1, wg_ref, wu_ref, wd_ref, g2, b2, acc_ref):
    mix = _dot(a, wo_ref[0:SB_WIDTH, :]) + _dot(p, wo_ref[SB_WIDTH:, :])
    x1 = _layer_norm(ALPHA * x + mix, g1, b1)
    x1b = x1.astype(BF16)
    for c in range(0, D_FF, FF_CHUNK):
        gate = _dot(x1b, wg_ref[:, c:c + FF_CHUNK])
        up = _dot(x1b, wu_ref[:, c:c + FF_CHUNK])
        hmid = (gate * jax.nn.sigmoid(gate) * up).astype(BF16)
        part = _dot(hmid, wd_ref[c:c + FF_CHUNK, :])
        if c == 0:
            acc_ref[...] = part
        else:
            acc_ref[...] += part
    return _layer_norm(ALPHA * x1 + acc_ref[...], g2, b2)


def _out_ffn_kernel(xp_ref, ap_ref, pp_ref, xs_ref, as_ref, ps_ref, wo_c, wg_c, wu_c, wd_c,
                    g1_ref, b1_ref, g2_ref, b2_ref, yp_ref, ys_ref,
                    wo_s, wg_s, wu_s, wd_s, accp_ref, accs_ref, *, n_conv, n_tiles):
    i = pl.program_id(0)

    @pl.when(i < n_conv)
    def _():
        for c_ref, s_ref in ((wo_c, wo_s), (wg_c, wg_s), (wu_c, wu_s), (wd_c, wd_s)):
            rows = c_ref.shape[0]
            s_ref[pl.ds(pl.multiple_of(i * rows, rows), rows), :] = c_ref[...].astype(BF16)

    norms = (g1_ref, b1_ref, g2_ref, b2_ref)

    def rows(x_ref, a_ref, p_ref, acc_ref):
        g1, b1, g2, b2 = (r[...] for r in norms)
        return _ffn_rows(x_ref[...], a_ref[...], p_ref[...], wo_s, g1, b1, wg_s, wu_s, wd_s, g2, b2, acc_ref)

    @pl.when(jnp.logical_and(i >= n_conv, i < n_conv + n_tiles))
    def _():
        yp_ref[...] = rows(xp_ref, ap_ref, pp_ref, accp_ref)

    @pl.when(i == n_conv + n_tiles)
    def _():
        ys_ref[...] = rows(xs_ref, as_ref, ps_ref, accs_ref)


def _out_ffn(xp, ap, pp, xs, a_s, ps, w_out, w_gate, w_up, w_down, g1, b1, g2, b2, *, tm, n_conv):
    Np, Ns = xp.shape[0], xs.shape[0]
    assert Np % tm == 0 and D_FF % FF_CHUNK == 0
    n_tiles = Np // tm
    tile = lambda i: jnp.clip(i - n_conv, 0, n_tiles - 1)
    row = lambda width: pl.BlockSpec((tm, width), lambda i: (tile(i), 0))
    once = lambda *shape: pl.BlockSpec(shape, lambda i: (0,) * len(shape), pipeline_mode=pl.Buffered(1))

    def chunked(w):
        assert w.shape[0] % (n_conv * 16) == 0
        return pl.BlockSpec((w.shape[0] // n_conv, w.shape[1]), lambda i: (jnp.minimum(i, n_conv - 1), 0))

    kern = functools.partial(_out_ffn_kernel, n_conv=n_conv, n_tiles=n_tiles)
    return pl.pallas_call(
        kern,
        grid=(n_conv + n_tiles + 1,),
        in_specs=[
            row(D_MODEL), row(SB_WIDTH), row(POOL_WIDTH),
            once(Ns, D_MODEL), once(Ns, SB_WIDTH), once(Ns, POOL_WIDTH),
            chunked(w_out), chunked(w_gate), chunked(w_up), chunked(w_down),
            once(1, D_MODEL), once(1, D_MODEL), once(1, D_MODEL), once(1, D_MODEL),
        ],
        out_specs=[row(D_MODEL), pl.BlockSpec((Ns, D_MODEL), lambda i: (0, 0))],
        out_shape=[jax.ShapeDtypeStruct((Np, D_MODEL), F32), jax.ShapeDtypeStruct((Ns, D_MODEL), F32)],
        scratch_shapes=[pltpu.VMEM(w.shape, BF16) for w in (w_out, w_gate, w_up, w_down)]
        + [pltpu.VMEM((tm, D_MODEL), F32), pltpu.VMEM((Ns, D_MODEL), F32)],
        compiler_params=pltpu.CompilerParams(
            dimension_semantics=("arbitrary",), vmem_limit_bytes=VMEM_LIMIT),
        name="out_proj_ffn",
    )(xp, ap, pp, xs, a_s, ps, w_out, w_gate, w_up, w_down, g1, b1, g2, b2)


def kernel(x_prompt, x_sample, cache_k, cache_v, state_pool, w_in, pool_w, pool_scale, w_out,
           ln1_g, ln1_b, w_gate, w_up, w_down, ln2_g, ln2_b):
    Bp, Tp, _ = x_prompt.shape
    Bs, Ts, _ = x_sample.shape
    past = cache_k.shape[1]

    row2 = lambda a: a.reshape(1, -1).astype(F32)
    pool_scale2 = row2(pool_scale)

    hist_p = jnp.zeros((Bp, HIST_ROWS, POOL_WIDTH), F32)
    k_p, v_p, mix_p, pool_p, a_p = _proj_attn_prompt(x_prompt, hist_p, w_in, pool_w, pool_scale2)

    hist_s = jnp.concatenate([jnp.zeros((Bs, HIST_ROWS - POOL_HIST, POOL_WIDTH), F32), state_pool.astype(F32)], axis=1)
    k_s, v_s, qkv_s, mix_s, pool_s = _in_proj(x_sample, hist_s, w_in, pool_w, pool_scale2, tm=Ts, start_pos=past)
    feature_major = lambda c: jnp.transpose(c, (0, 2, 3, 1)).reshape(Bs, SB_WIDTH, past)
    a_s = _attn_sample(qkv_s, feature_major(cache_k), feature_major(cache_v))

    y_p, y_s = _out_ffn(
        x_prompt.reshape(Bp * Tp, D_MODEL), a_p.reshape(Bp * Tp, SB_WIDTH), mix_p.reshape(Bp * Tp, POOL_WIDTH),
        x_sample.reshape(Bs * Ts, D_MODEL), a_s.reshape(Bs * Ts, SB_WIDTH), mix_s.reshape(Bs * Ts, POOL_WIDTH),
        w_out, w_gate, w_up, w_down, row2(ln1_g), row2(ln1_b), row2(ln2_g), row2(ln2_b), tm=512, n_conv=16)

    heads = lambda a: a.reshape(a.shape[0], a.shape[1], SB_HEADS, SB_HEAD_DIM)
    return (y_p.reshape(Bp, Tp, D_MODEL), y_s.reshape(Bs, Ts, D_MODEL),
            heads(k_p), heads(v_p), pool_p, heads(k_s), heads(v_s), pool_s)
```

```python
import functools

import jax
import jax.numpy as jnp
from jax import lax
from jax.experimental import pallas as pl
from jax.experimental.pallas import tpu as pltpu

D_MODEL = 1024
SB_WIDTH = 512
SB_HEADS = 8
SB_HEAD_DIM = 64
POOL_WIDTH = 512
POOL_WINDOWS = (2, 4, 8, 16)
POOL_GC = 128
POOL_HIST = 15
HIST_ROWS = 16
D_FF = 2816
DEPTH = 1
ALPHA = (2 * DEPTH) ** 0.25
LN_EPS = 1e-5
LANES = 128
HEAD_PAIRS = SB_WIDTH // LANES
LOG2E = 1.4426950408889634
Q_SCALE = SB_HEAD_DIM ** -0.5 * LOG2E
SKIP_LOG2 = 160.0
ATTN_BLOCK = 256
FF_CHUNK = 256
VMEM_LIMIT = 56 * 1024 * 1024

F32 = jnp.float32
BF16 = jnp.bfloat16


def _dot(a, b):
    return jnp.dot(a, b, preferred_element_type=F32)


def _dot_nt(a, b):
    return lax.dot_general(a, b, (((1,), (1,)), ((), ())), preferred_element_type=F32)


Q_COLS, K_COLS, V_COLS, U_COLS = (SB_WIDTH * i for i in range(4))


def _proj_cols(x, w_in_ref, start, width):
    return _dot(x, w_in_ref[:, start:start + width])


def _pool_mix(u, hist, pool_w_ref, pool_scale_ref, u_ext, t, *, tm, start_pos):
    u_ext[0:HIST_ROWS, :] = jnp.where(t == 0, hist, u_ext[tm:tm + HIST_ROWS, :])
    u_ext[HIST_ROWS:HIST_ROWS + tm, :] = u

    pos = start_pos + t * tm + lax.broadcasted_iota(jnp.int32, (tm, 1), 0)
    pooled = []
    for g, w in enumerate(POOL_WINDOWS):
        cols = slice(g * POOL_GC, (g + 1) * POOL_GC)
        s = u_ext[:, cols]
        span = 1
        while span < w:
            s = s + pltpu.roll(s, span, 0)
            span *= 2
        cnt = jnp.minimum(w, pos + 1).astype(F32)
        d = s[HIST_ROWS:, :] / cnt - u[:, cols]
        y = _dot(d.astype(BF16), pool_w_ref[g].astype(BF16)) * pool_scale_ref[:, cols]
        pooled.append(y.astype(BF16))
    return pooled


def _pool_state(u_ext, tm):
    return u_ext[tm + HIST_ROWS - POOL_HIST:tm + HIST_ROWS, :]


def _in_proj_kernel(x_ref, hist_ref, w_in_f32_ref, pool_w_ref, pool_scale_ref,
                    k_ref, v_ref, qkv_ref, p_ref, pool_ref, w_in_ref, u_ext, *, tm, start_pos):
    t = pl.program_id(1)

    @pl.when(jnp.logical_and(pl.program_id(0) == 0, t == 0))
    def _():
        w_in_ref[...] = w_in_f32_ref[...].astype(BF16)
        u_ext[...] = jnp.zeros_like(u_ext)

    x = x_ref[0].astype(BF16)
    qkv_ref[0, :, 0:SB_WIDTH] = (_proj_cols(x, w_in_ref, Q_COLS, SB_WIDTH) * Q_SCALE).astype(BF16)
    k = _proj_cols(x, w_in_ref, K_COLS, SB_WIDTH)
    k_ref[0] = k
    qkv_ref[0, :, SB_WIDTH:2 * SB_WIDTH] = k.astype(BF16)
    v = _proj_cols(x, w_in_ref, V_COLS, SB_WIDTH)
    v_ref[0] = v
    qkv_ref[0, :, 2 * SB_WIDTH:3 * SB_WIDTH] = v.astype(BF16)
    u = _proj_cols(x, w_in_ref, U_COLS, POOL_WIDTH)
    pooled = _pool_mix(u, hist_ref[0], pool_w_ref, pool_scale_ref, u_ext, t, tm=tm, start_pos=start_pos)
    for g, y in enumerate(pooled):
        p_ref[0, :, g * POOL_GC:(g + 1) * POOL_GC] = y
    pool_ref[0] = _pool_state(u_ext, tm)


def _in_proj(x, hist, w_in, pool_w, pool_scale2, *, tm, start_pos):
    B, T, _ = x.shape
    assert T % tm == 0 and tm >= HIST_ROWS and T >= POOL_HIST
    kern = functools.partial(_in_proj_kernel, tm=tm, start_pos=start_pos)
    const = lambda *shape: pl.BlockSpec(shape, lambda b, t: (0,) * len(shape), pipeline_mode=pl.Buffered(1))
    return pl.pallas_call(
        kern,
        grid=(B, T // tm),
        in_specs=[
            pl.BlockSpec((1, tm, D_MODEL), lambda b, t: (b, t, 0)),
            pl.BlockSpec((1, HIST_ROWS, POOL_WIDTH), lambda b, t: (b, 0, 0)),
            const(D_MODEL, 3 * SB_WIDTH + POOL_WIDTH),
            const(len(POOL_WINDOWS), POOL_GC, POOL_GC),
            const(1, POOL_WIDTH),
        ],
        out_specs=[
            pl.BlockSpec((1, tm, SB_WIDTH), lambda b, t: (b, t, 0)),
            pl.BlockSpec((1, tm, SB_WIDTH), lambda b, t: (b, t, 0)),
            pl.BlockSpec((1, tm, 3 * SB_WIDTH), lambda b, t: (b, t, 0)),
            pl.BlockSpec((1, tm, POOL_WIDTH), lambda b, t: (b, t, 0)),
            pl.BlockSpec((1, POOL_HIST, POOL_WIDTH), lambda b, t: (b, 0, 0)),
        ],
        out_shape=[
            jax.ShapeDtypeStruct((B, T, SB_WIDTH), F32),
            jax.ShapeDtypeStruct((B, T, SB_WIDTH), F32),
            jax.ShapeDtypeStruct((B, T, 3 * SB_WIDTH), BF16),
            jax.ShapeDtypeStruct((B, T, POOL_WIDTH), BF16),
            jax.ShapeDtypeStruct((B, POOL_HIST, POOL_WIDTH), F32),
        ],
        scratch_shapes=[pltpu.VMEM(w_in.shape, BF16), pltpu.VMEM((HIST_ROWS + tm, POOL_WIDTH), F32)],
        compiler_params=pltpu.CompilerParams(
            dimension_semantics=("arbitrary", "arbitrary"), vmem_limit_bytes=VMEM_LIMIT),
        name="in_proj_pool",
    )(x, hist, w_in, pool_w, pool_scale2)


def _strict_upper(n):
    r = lax.broadcasted_iota(jnp.int32, (n, n), 0)
    c = lax.broadcasted_iota(jnp.int32, (n, n), 1)
    return (r > c).astype(BF16)


def _causal_mask(n):
    r = lax.broadcasted_iota(jnp.int32, (n, n), 0)
    c = lax.broadcasted_iota(jnp.int32, (n, n), 1)
    return c < r


def _store_masked_queries(q, qm_ref):
    m = q.shape[0]
    lower_half = lax.broadcasted_iota(jnp.int32, (m, LANES), 1) < SB_HEAD_DIM
    for p in range(HEAD_PAIRS):
        q2 = q[:, p * LANES:(p + 1) * LANES]
        zero = jnp.zeros_like(q2)
        qm_ref[p, 0:m, :] = jnp.where(lower_half, q2, zero)
        qm_ref[p, m:2 * m, :] = jnp.where(lower_half, zero, q2)


def _sb_all_heads(qm_ref, kv_block, decay_ref, acc_ref, upper, mask, first, key_minor=False, extra_decay=None,
                  fillers=()):
    fillers = list(fillers)
    m = qm_ref.shape[1] // 2
    lower_half = lax.broadcasted_iota(jnp.int32, (m, LANES), 1) < SB_HEAD_DIM
    mask2 = None if mask is None else jnp.concatenate([mask, mask], axis=0)
    kv = [kv_block(p) for p in range(HEAD_PAIRS)]

    logit, sps, rowsums = [], [], []
    for p in range(HEAD_PAIRS):
        z = (_dot if key_minor else _dot_nt)(qm_ref[p], kv[p][0])
        sp = jnp.maximum(z, 0.0) + jnp.log(1.0 + jnp.exp2(-jnp.abs(z))) * LOG2E
        spm = sp if mask2 is None else jnp.where(mask2, sp, 0.0)
        sps.append(spm.astype(BF16))
        logit.append(z - sp)
        rowsums.append(jnp.sum(spm, axis=1, keepdims=True))
        if fillers:
            fillers.pop(0)()

    later = _dot(jnp.concatenate(sps, axis=0), upper)

    least = None
    for p in range(HEAD_PAIRS):
        a = jnp.exp2(logit[p] - later[2 * m * p:2 * m * (p + 1)])
        if mask2 is not None:
            a = jnp.where(mask2, a, 0.0)
        pv = (_dot_nt if key_minor else _dot)(a.astype(BF16), kv[p][1])
        merged = jnp.where(lower_half, pv[0:m], pv[m:2 * m])
        rowsum = jnp.where(lower_half, rowsums[p][0:m], rowsums[p][m:2 * m])
        if first:
            decay = rowsum
            acc_ref[p] = merged
        else:
            prev = decay_ref[p]
            if extra_decay is not None:
                prev = prev + extra_decay
            decay = prev + rowsum
            acc_ref[p] += merged * jnp.exp2(-prev)
        decay_ref[p] = decay
        least = decay if least is None else jnp.minimum(least, decay)
    return jnp.min(least)


def _more_blocks(state):
    j, least = state
    return jnp.logical_and(j >= 0, least < SKIP_LOG2)


def _store_heads(acc_ref, o_ref):
    for p in range(HEAD_PAIRS):
        o_ref[0, :, p * LANES:(p + 1) * LANES] = acc_ref[p].astype(BF16)


def _attn_scratch(m):
    return [pltpu.VMEM((HEAD_PAIRS, 2 * m, LANES), BF16), pltpu.VMEM((HEAD_PAIRS, m, LANES), F32),
            pltpu.VMEM((HEAD_PAIRS, m, LANES), F32)]


def _proj_attn_kernel(x_ref, hist_ref, w_in_f32_ref, pool_w_ref, pool_scale_ref, after_ref,
                      k_ref, v_ref, p_ref, pool_ref, a_ref,
                      w_in_ref, u_ext, xb_ref, stage_q, stage_kv, kv_all, qm_ref, decay_ref, acc_ref,
                      *, tm, n_t, n_tiles):
    s = pl.program_id(0)

    @pl.when(s == 0)
    def _():
        w_in_ref[...] = w_in_f32_ref[...].astype(BF16)
        stage_q[...] = jnp.zeros_like(stage_q)
        stage_kv[...] = jnp.zeros_like(stage_kv)
        u_ext[...] = jnp.zeros_like(u_ext)

    upper = _strict_upper(tm)
    qi = lax.rem(jnp.maximum(s - 1, 0), n_t)

    def kv_block(j):
        start = pl.multiple_of(j * tm, tm)
        return lambda p: (kv_all[pl.ds(start, tm), p * LANES:(p + 1) * LANES],
                          kv_all[pl.ds(start, tm), SB_WIDTH + p * LANES:SB_WIDTH + (p + 1) * LANES])

    def take_stage():
        kv_all[pl.ds(pl.multiple_of(qi * tm, tm), tm), :] = stage_kv[...]
        _store_masked_queries(stage_q[...], qm_ref)

    def attend_near(fillers):
        _sb_all_heads(qm_ref, kv_block(qi), decay_ref, acc_ref, upper, _causal_mask(tm), True,
                      fillers=fillers[0:HEAD_PAIRS])
        void = jnp.where(qi == 0, 2.0 * SKIP_LOG2, 0.0)
        return _sb_all_heads(qm_ref, kv_block(jnp.maximum(qi - 1, 0)), decay_ref, acc_ref, upper, None, False,
                             extra_decay=void, fillers=fillers[HEAD_PAIRS:2 * HEAD_PAIRS])

    def attend_far(least):
        def body(state):
            j, _ = state
            return j - 1, _sb_all_heads(qm_ref, kv_block(j), decay_ref, acc_ref, upper, None, False)

        lax.while_loop(_more_blocks, body, (qi - 2, least))
        _store_heads(acc_ref, a_ref)

    half = SB_WIDTH // 2

    def project_q(c):
        def piece():
            q = _proj_cols(xb_ref[...], w_in_ref, Q_COLS + c, half) * Q_SCALE
            stage_q[:, c:c + half] = q.astype(BF16)
        return piece

    def project_kv(out_ref, w_cols, stage_cols, c):
        def piece():
            y = _proj_cols(xb_ref[...], w_in_ref, w_cols + c, half)
            out_ref[0, :, c:c + half] = y
            stage_kv[:, stage_cols + c:stage_cols + c + half] = y.astype(BF16)
        return piece

    u_halves = []

    def project_u0():
        u_halves.append(_proj_cols(xb_ref[...], w_in_ref, U_COLS, half))

    def project_u1_pool():
        u_halves.append(_proj_cols(xb_ref[...], w_in_ref, U_COLS + half, half))
        pooled = _pool_mix(jnp.concatenate(u_halves, axis=1), hist_ref[0], pool_w_ref, pool_scale_ref, u_ext,
                           lax.rem(s, n_t), tm=tm, start_pos=0)
        for g, y in enumerate(pooled):
            p_ref[0, :, g * POOL_GC:(g + 1) * POOL_GC] = y
        pool_ref[0] = _pool_state(u_ext, tm)

    @pl.when(s < n_tiles)
    def _():
        take_stage()
        xb_ref[...] = x_ref[0].astype(BF16)
        attend_far(attend_near([
            project_q(0), project_q(half),
            project_kv(k_ref, K_COLS, 0, 0), project_kv(k_ref, K_COLS, 0, half),
            project_kv(v_ref, V_COLS, SB_WIDTH, 0), project_kv(v_ref, V_COLS, SB_WIDTH, half),
            project_u0, project_u1_pool]))

    @pl.when(s == n_tiles)
    def _():
        take_stage()
        attend_far(attend_near([]))


def _proj_attn_prompt(x, hist, w_in, pool_w, pool_scale2, after):
    B, T, _ = x.shape
    tm = ATTN_BLOCK
    assert T % tm == 0 and T >= POOL_HIST
    n_t = T // tm
    n_tiles = B * n_t
    kern = functools.partial(_proj_attn_kernel, tm=tm, n_t=n_t, n_tiles=n_tiles)
    proj = lambda s: jnp.minimum(s, n_tiles - 1)
    attn = lambda s: jnp.maximum(s - 1, 0)
    const = lambda *shape: pl.BlockSpec(shape, lambda s: (0,) * len(shape), pipeline_mode=pl.Buffered(1))
    proj_rows = lambda width: pl.BlockSpec((1, tm, width), lambda s: (proj(s) // n_t, proj(s) % n_t, 0))
    return pl.pallas_call(
        kern,
        grid=(n_tiles + 1,),
        in_specs=[
            proj_rows(D_MODEL),
            pl.BlockSpec((1, HIST_ROWS, POOL_WIDTH), lambda s: (proj(s) // n_t, 0, 0)),
            const(D_MODEL, 3 * SB_WIDTH + POOL_WIDTH),
            const(len(POOL_WINDOWS), POOL_GC, POOL_GC),
            const(1, POOL_WIDTH),
            pl.BlockSpec(memory_space=pl.ANY),
        ],
        out_specs=[
            proj_rows(SB_WIDTH),
            proj_rows(SB_WIDTH),
            proj_rows(POOL_WIDTH),
            pl.BlockSpec((1, POOL_HIST, POOL_WIDTH), lambda s: (proj(s) // n_t, 0, 0)),
            pl.BlockSpec((1, tm, SB_WIDTH), lambda s: (attn(s) // n_t, attn(s) % n_t, 0)),
        ],
        out_shape=[
            jax.ShapeDtypeStruct((B, T, SB_WIDTH), F32),
            jax.ShapeDtypeStruct((B, T, SB_WIDTH), F32),
            jax.ShapeDtypeStruct((B, T, POOL_WIDTH), BF16),
            jax.ShapeDtypeStruct((B, POOL_HIST, POOL_WIDTH), F32),
            jax.ShapeDtypeStruct((B, T, SB_WIDTH), BF16),
        ],
        scratch_shapes=[
            pltpu.VMEM(w_in.shape, BF16),
            pltpu.VMEM((HIST_ROWS + tm, POOL_WIDTH), F32),
            pltpu.VMEM((tm, D_MODEL), BF16),
            pltpu.VMEM((tm, SB_WIDTH), BF16),
            pltpu.VMEM((tm, 2 * SB_WIDTH), BF16),
            pltpu.VMEM((T, 2 * SB_WIDTH), BF16),
        ] + _attn_scratch(tm),
        compiler_params=pltpu.CompilerParams(
            dimension_semantics=("arbitrary",), vmem_limit_bytes=VMEM_LIMIT),
        name="proj_attn_prompt",
    )(x, hist, w_in, pool_w, pool_scale2, after)


def _attn_sample_kernel(q_ref, kn_ref, vn_ref, ck_ref, cv_ref, o_ref, qm_ref, decay_ref, acc_ref,
                        *, t_new, past, blk):
    _store_masked_queries(q_ref[0], qm_ref)
    new_kv = lambda p: (kn_ref[0, :, p * LANES:(p + 1) * LANES], vn_ref[0, :, p * LANES:(p + 1) * LANES])
    least = _sb_all_heads(qm_ref, new_kv, decay_ref, acc_ref, _strict_upper(t_new), _causal_mask(t_new), True)
    upper = _strict_upper(blk)

    def cache_kv(j):
        start = pl.multiple_of(j * blk, blk)
        return lambda p: (ck_ref[0, p * LANES:(p + 1) * LANES, pl.ds(start, blk)].astype(BF16),
                          cv_ref[0, p * LANES:(p + 1) * LANES, pl.ds(start, blk)].astype(BF16))

    def body(state):
        j, _ = state
        return j - 1, _sb_all_heads(qm_ref, cache_kv(j), decay_ref, acc_ref, upper, None, False, key_minor=True)

    lax.while_loop(_more_blocks, body, (past // blk - 1, least))
    _store_heads(acc_ref, o_ref)


def _attn_sample(qkv, cache_kt, cache_vt):
    B, t_new, _ = qkv.shape
    past = cache_kt.shape[2]
    blk = ATTN_BLOCK
    assert past % blk == 0
    kern = functools.partial(_attn_sample_kernel, t_new=t_new, past=past, blk=blk)
    return pl.pallas_call(
        kern,
        grid=(B,),
        in_specs=[
            pl.BlockSpec((1, t_new, SB_WIDTH), lambda b: (b, 0, 0)),
            pl.BlockSpec((1, t_new, SB_WIDTH), lambda b: (b, 0, 1)),
            pl.BlockSpec((1, t_new, SB_WIDTH), lambda b: (b, 0, 2)),
            pl.BlockSpec((1, SB_WIDTH, past), lambda b: (b, 0, 0)),
            pl.BlockSpec((1, SB_WIDTH, past), lambda b: (b, 0, 0)),
        ],
        out_specs=pl.BlockSpec((1, t_new, SB_WIDTH), lambda b: (b, 0, 0)),
        out_shape=jax.ShapeDtypeStruct((B, t_new, SB_WIDTH), BF16),
        scratch_shapes=_attn_scratch(t_new),
        compiler_params=pltpu.CompilerParams(
            dimension_semantics=("arbitrary",), vmem_limit_bytes=VMEM_LIMIT),
        name="sb_attn_sample",
    )(qkv, qkv, qkv, cache_kt, cache_vt)


def _layer_norm(x, g, b):
    mu = jnp.mean(x, axis=-1, keepdims=True)
    xc = x - mu
    var = jnp.mean(xc * xc, axis=-1, keepdims=True)
    return xc * lax.rsqrt(var + LN_EPS) * g + b


def _ffn_rows(x, a, p, wo_ref, g1, b1, wg_ref, wu_ref, wd_ref, g2, b2, acc_ref):
    mix = _dot(a, wo_ref[0:SB_WIDTH, :]) + _dot(p, wo_ref[SB_WIDTH:, :])
    x1 = _layer_norm(ALPHA * x + mix, g1, b1)
    x1b = x1.astype(BF16)
    for c in range(0, D_FF, FF_CHUNK):
        gate = _dot(x1b, wg_ref[:, c:c + FF_CHUNK])
        up = _dot(x1b, wu_ref[:, c:c + FF_CHUNK])
        hmid = (gate * jax.nn.sigmoid(gate) * up).astype(BF16)
        part = _dot(hmid, wd_ref[c:c + FF_CHUNK, :])
        if c == 0:
            acc_ref[...] = part
        else:
            acc_ref[...] += part
    return _layer_norm(ALPHA * x1 + acc_ref[...], g2, b2)


def _out_ffn_kernel(xp_ref, ap_ref, pp_ref, xs_ref, as_ref, ps_ref, wo_c, wg_c, wu_c, wd_c,
                    g1_ref, b1_ref, g2_ref, b2_ref, yp_ref, ys_ref,
                    wo_s, wg_s, wu_s, wd_s, accp_ref, accs_ref, *, n_conv, n_tiles):
    i = pl.program_id(0)

    @pl.when(i < n_conv)
    def _():
        for c_ref, s_ref in ((wo_c, wo_s), (wg_c, wg_s), (wu_c, wu_s), (wd_c, wd_s)):
            rows = c_ref.shape[0]
            s_ref[pl.ds(pl.multiple_of(i * rows, rows), rows), :] = c_ref[...].astype(BF16)

    norms = (g1_ref, b1_ref, g2_ref, b2_ref)

    def rows(x_ref, a_ref, p_ref, acc_ref):
        g1, b1, g2, b2 = (r[...] for r in norms)
        return _ffn_rows(x_ref[...], a_ref[...], p_ref[...], wo_s, g1, b1, wg_s, wu_s, wd_s, g2, b2, acc_ref)

    @pl.when(jnp.logical_and(i >= n_conv, i < n_conv + n_tiles))
    def _():
        yp_ref[...] = rows(xp_ref, ap_ref, pp_ref, accp_ref)

    @pl.when(i == n_conv + n_tiles)
    def _():
        ys_ref[...] = rows(xs_ref, as_ref, ps_ref, accs_ref)


def _out_ffn(xp, ap, pp, xs, a_s, ps, w_out, w_gate, w_up, w_down, g1, b1, g2, b2, *, tm, n_conv):
    Np, Ns = xp.shape[0], xs.shape[0]
    assert Np % tm == 0 and D_FF % FF_CHUNK == 0
    n_tiles = Np // tm
    tile = lambda i: jnp.clip(i - n_conv, 0, n_tiles - 1)
    row = lambda width: pl.BlockSpec((tm, width), lambda i: (tile(i), 0))
    once = lambda *shape: pl.BlockSpec(shape, lambda i: (0,) * len(shape), pipeline_mode=pl.Buffered(1))

    def chunked(w):
        assert w.shape[0] % (n_conv * 16) == 0
        return pl.BlockSpec((w.shape[0] // n_conv, w.shape[1]), lambda i: (jnp.minimum(i, n_conv - 1), 0))

    kern = functools.partial(_out_ffn_kernel, n_conv=n_conv, n_tiles=n_tiles)
    return pl.pallas_call(
        kern,
        grid=(n_conv + n_tiles + 1,),
        in_specs=[
            row(D_MODEL), row(SB_WIDTH), row(POOL_WIDTH),
            once(Ns, D_MODEL), once(Ns, SB_WIDTH), once(Ns, POOL_WIDTH),
            chunked(w_out), chunked(w_gate), chunked(w_up), chunked(w_down),
            once(1, D_MODEL), once(1, D_MODEL), once(1, D_MODEL), once(1, D_MODEL),
        ],
        out_specs=[row(D_MODEL), pl.BlockSpec((Ns, D_MODEL), lambda i: (0, 0))],
        out_shape=[jax.ShapeDtypeStruct((Np, D_MODEL), F32), jax.ShapeDtypeStruct((Ns, D_MODEL), F32)],
        scratch_shapes=[pltpu.VMEM(w.shape, BF16) for w in (w_out, w_gate, w_up, w_down)]
        + [pltpu.VMEM((tm, D_MODEL), F32), pltpu.VMEM((Ns, D_MODEL), F32)],
        compiler_params=pltpu.CompilerParams(
            dimension_semantics=("arbitrary",), vmem_limit_bytes=VMEM_LIMIT),
        name="out_proj_ffn",
    )(xp, ap, pp, xs, a_s, ps, w_out, w_gate, w_up, w_down, g1, b1, g2, b2)


def kernel(x_prompt, x_sample, cache_k, cache_v, state_pool, w_in, pool_w, pool_scale, w_out,
           ln1_g, ln1_b, w_gate, w_up, w_down, ln2_g, ln2_b):
    Bp, Tp, _ = x_prompt.shape
    Bs, Ts, _ = x_sample.shape
    past = cache_k.shape[1]

    row2 = lambda a: a.reshape(1, -1).astype(F32)
    pool_scale2 = row2(pool_scale)

    hist_s = jnp.concatenate([jnp.zeros((Bs, HIST_ROWS - POOL_HIST, POOL_WIDTH), F32), state_pool.astype(F32)], axis=1)
    k_s, v_s, qkv_s, mix_s, pool_s = _in_proj(x_sample, hist_s, w_in, pool_w, pool_scale2, tm=Ts, start_pos=past)
    feature_major = lambda c: jnp.transpose(c, (0, 2, 3, 1)).reshape(Bs, SB_WIDTH, past)
    a_s = _attn_sample(qkv_s, feature_major(cache_k), feature_major(cache_v))

    hist_p = jnp.zeros((Bp, HIST_ROWS, POOL_WIDTH), F32)
    k_p, v_p, mix_p, pool_p, a_p = _proj_attn_prompt(x_prompt, hist_p, w_in, pool_w, pool_scale2, after=a_s)

    y_p, y_s = _out_ffn(
        x_prompt.reshape(Bp * Tp, D_MODEL), a_p.reshape(Bp * Tp, SB_WIDTH), mix_p.reshape(Bp * Tp, POOL_WIDTH),
        x_sample.reshape(Bs * Ts, D_MODEL), a_s.reshape(Bs * Ts, SB_WIDTH), mix_s.reshape(Bs * Ts, POOL_WIDTH),
        w_out, w_gate, w_up, w_down, row2(ln1_g), row2(ln1_b), row2(ln2_g), row2(ln2_b), tm=512, n_conv=16)

    heads = lambda a: a.reshape(a.shape[0], a.shape[1], SB_HEADS, SB_HEAD_DIM)
    return (y_p.reshape(Bp, Tp, D_MODEL), y_s.reshape(Bs, Ts, D_MODEL),
            heads(k_p), heads(v_p), pool_p, heads(k_s), heads(v_s), pool_s)
```

```python
import functools

import jax
import jax.numpy as jnp
from jax import lax
from jax.experimental import pallas as pl
from jax.experimental.pallas import tpu as pltpu

D_MODEL = 1024
SB_WIDTH = 512
SB_HEADS = 8
SB_HEAD_DIM = 64
POOL_WIDTH = 512
POOL_WINDOWS = (2, 4, 8, 16)
POOL_GC = 128
POOL_HIST = 15
HIST_ROWS = 16
D_FF = 2816
DEPTH = 1
ALPHA = (2 * DEPTH) ** 0.25
LN_EPS = 1e-5
LANES = 128
HEAD_PAIRS = SB_WIDTH // LANES
LOG2E = 1.4426950408889634
Q_SCALE = SB_HEAD_DIM ** -0.5 * LOG2E
SKIP_LOG2 = 160.0
ATTN_BLOCK = 256
FF_CHUNK = 256
FFN_PARTS = 2
VMEM_LIMIT = 56 * 1024 * 1024

F32 = jnp.float32
BF16 = jnp.bfloat16


def _dot(a, b):
    return jnp.dot(a, b, preferred_element_type=F32)


def _dot_nt(a, b):
    return lax.dot_general(a, b, (((1,), (1,)), ((), ())), preferred_element_type=F32)


Q_COLS, K_COLS, V_COLS, U_COLS = (SB_WIDTH * i for i in range(4))


def _proj_cols(x, w_in_ref, start, width):
    return _dot(x, w_in_ref[:, start:start + width])


def _pool_mix(u, hist, pool_w_ref, pool_scale_ref, u_ext, t, *, tm, start_pos):
    u_ext[0:HIST_ROWS, :] = jnp.where(t == 0, hist, u_ext[tm:tm + HIST_ROWS, :])
    u_ext[HIST_ROWS:HIST_ROWS + tm, :] = u

    pos = start_pos + t * tm + lax.broadcasted_iota(jnp.int32, (tm, 1), 0)
    pooled = []
    for g, w in enumerate(POOL_WINDOWS):
        cols = slice(g * POOL_GC, (g + 1) * POOL_GC)
        s = u_ext[:, cols]
        span = 1
        while span < w:
            s = s + pltpu.roll(s, span, 0)
            span *= 2
        cnt = jnp.minimum(w, pos + 1).astype(F32)
        d = s[HIST_ROWS:, :] / cnt - u[:, cols]
        y = _dot(d.astype(BF16), pool_w_ref[g].astype(BF16)) * pool_scale_ref[:, cols]
        pooled.append(y.astype(BF16))
    return pooled


def _pool_state(u_ext, tm):
    return u_ext[tm + HIST_ROWS - POOL_HIST:tm + HIST_ROWS, :]


def _in_proj_kernel(x_ref, hist_ref, w_in_f32_ref, pool_w_ref, pool_scale_ref,
                    k_ref, v_ref, qkv_ref, p_ref, pool_ref, w_in_ref, u_ext, *, tm, start_pos):
    t = pl.program_id(1)

    @pl.when(jnp.logical_and(pl.program_id(0) == 0, t == 0))
    def _():
        w_in_ref[...] = w_in_f32_ref[...].astype(BF16)
        u_ext[...] = jnp.zeros_like(u_ext)

    x = x_ref[0].astype(BF16)
    qkv_ref[0, :, 0:SB_WIDTH] = (_proj_cols(x, w_in_ref, Q_COLS, SB_WIDTH) * Q_SCALE).astype(BF16)
    k = _proj_cols(x, w_in_ref, K_COLS, SB_WIDTH)
    k_ref[0] = k
    qkv_ref[0, :, SB_WIDTH:2 * SB_WIDTH] = k.astype(BF16)
    v = _proj_cols(x, w_in_ref, V_COLS, SB_WIDTH)
    v_ref[0] = v
    qkv_ref[0, :, 2 * SB_WIDTH:3 * SB_WIDTH] = v.astype(BF16)
    u = _proj_cols(x, w_in_ref, U_COLS, POOL_WIDTH)
    pooled = _pool_mix(u, hist_ref[0], pool_w_ref, pool_scale_ref, u_ext, t, tm=tm, start_pos=start_pos)
    for g, y in enumerate(pooled):
        p_ref[0, :, g * POOL_GC:(g + 1) * POOL_GC] = y
    pool_ref[0] = _pool_state(u_ext, tm)


def _in_proj(x, hist, w_in, pool_w, pool_scale2, *, tm, start_pos):
    B, T, _ = x.shape
    assert T % tm == 0 and tm >= HIST_ROWS and T >= POOL_HIST
    kern = functools.partial(_in_proj_kernel, tm=tm, start_pos=start_pos)
    const = lambda *shape: pl.BlockSpec(shape, lambda b, t: (0,) * len(shape), pipeline_mode=pl.Buffered(1))
    return pl.pallas_call(
        kern,
        grid=(B, T // tm),
        in_specs=[
            pl.BlockSpec((1, tm, D_MODEL), lambda b, t: (b, t, 0)),
            pl.BlockSpec((1, HIST_ROWS, POOL_WIDTH), lambda b, t: (b, 0, 0)),
            const(D_MODEL, 3 * SB_WIDTH + POOL_WIDTH),
            const(len(POOL_WINDOWS), POOL_GC, POOL_GC),
            const(1, POOL_WIDTH),
        ],
        out_specs=[
            pl.BlockSpec((1, tm, SB_WIDTH), lambda b, t: (b, t, 0)),
            pl.BlockSpec((1, tm, SB_WIDTH), lambda b, t: (b, t, 0)),
            pl.BlockSpec((1, tm, 3 * SB_WIDTH), lambda b, t: (b, t, 0)),
            pl.BlockSpec((1, tm, POOL_WIDTH), lambda b, t: (b, t, 0)),
            pl.BlockSpec((1, POOL_HIST, POOL_WIDTH), lambda b, t: (b, 0, 0)),
        ],
        out_shape=[
            jax.ShapeDtypeStruct((B, T, SB_WIDTH), F32),
            jax.ShapeDtypeStruct((B, T, SB_WIDTH), F32),
            jax.ShapeDtypeStruct((B, T, 3 * SB_WIDTH), BF16),
            jax.ShapeDtypeStruct((B, T, POOL_WIDTH), BF16),
            jax.ShapeDtypeStruct((B, POOL_HIST, POOL_WIDTH), F32),
        ],
        scratch_shapes=[pltpu.VMEM(w_in.shape, BF16), pltpu.VMEM((HIST_ROWS + tm, POOL_WIDTH), F32)],
        compiler_params=pltpu.CompilerParams(
            dimension_semantics=("arbitrary", "arbitrary"), vmem_limit_bytes=VMEM_LIMIT),
        name="in_proj_pool",
    )(x, hist, w_in, pool_w, pool_scale2)


def _strict_upper(n):
    r = lax.broadcasted_iota(jnp.int32, (n, n), 0)
    c = lax.broadcasted_iota(jnp.int32, (n, n), 1)
    return (r > c).astype(BF16)


def _causal_mask(n):
    r = lax.broadcasted_iota(jnp.int32, (n, n), 0)
    c = lax.broadcasted_iota(jnp.int32, (n, n), 1)
    return c < r


def _store_masked_queries(q, qm_ref):
    m = q.shape[0]
    lower_half = lax.broadcasted_iota(jnp.int32, (m, LANES), 1) < SB_HEAD_DIM
    for p in range(HEAD_PAIRS):
        q2 = q[:, p * LANES:(p + 1) * LANES]
        zero = jnp.zeros_like(q2)
        qm_ref[p, 0:m, :] = jnp.where(lower_half, q2, zero)
        qm_ref[p, m:2 * m, :] = jnp.where(lower_half, zero, q2)


def _sb_all_heads(qm_ref, kv_block, decay_ref, acc_ref, upper, mask, first, key_minor=False, extra_decay=None,
                  fillers=()):
    fillers = list(fillers)
    m = qm_ref.shape[1] // 2
    lower_half = lax.broadcasted_iota(jnp.int32, (m, LANES), 1) < SB_HEAD_DIM
    mask2 = None if mask is None else jnp.concatenate([mask, mask], axis=0)
    kv = [kv_block(p) for p in range(HEAD_PAIRS)]

    logit, sps, rowsums = [], [], []
    for p in range(HEAD_PAIRS):
        z = (_dot if key_minor else _dot_nt)(qm_ref[p], kv[p][0])
        sp = jnp.maximum(z, 0.0) + jnp.log(1.0 + jnp.exp2(-jnp.abs(z))) * LOG2E
        spm = sp if mask2 is None else jnp.where(mask2, sp, 0.0)
        sps.append(spm.astype(BF16))
        logit.append(z - sp)
        rowsums.append(jnp.sum(spm, axis=1, keepdims=True))
        if fillers:
            fillers.pop(0)()

    later = _dot(jnp.concatenate(sps, axis=0), upper)

    least = None
    for p in range(HEAD_PAIRS):
        a = jnp.exp2(logit[p] - later[2 * m * p:2 * m * (p + 1)])
        if mask2 is not None:
            a = jnp.where(mask2, a, 0.0)
        pv = (_dot_nt if key_minor else _dot)(a.astype(BF16), kv[p][1])
        merged = jnp.where(lower_half, pv[0:m], pv[m:2 * m])
        rowsum = jnp.where(lower_half, rowsums[p][0:m], rowsums[p][m:2 * m])
        if first:
            decay = rowsum
            acc_ref[p] = merged
        else:
            prev = decay_ref[p]
            if extra_decay is not None:
                prev = prev + extra_decay
            decay = prev + rowsum
            acc_ref[p] += merged * jnp.exp2(-prev)
        decay_ref[p] = decay
        least = decay if least is None else jnp.minimum(least, decay)
    return jnp.min(least)


def _more_blocks(state):
    j, least = state
    return jnp.logical_and(j >= 0, least < SKIP_LOG2)


def _store_heads(acc_ref, o_ref):
    for p in range(HEAD_PAIRS):
        o_ref[0, :, p * LANES:(p + 1) * LANES] = acc_ref[p].astype(BF16)


def _attn_scratch(m):
    return [pltpu.VMEM((HEAD_PAIRS, 2 * m, LANES), BF16), pltpu.VMEM((HEAD_PAIRS, m, LANES), F32),
            pltpu.VMEM((HEAD_PAIRS, m, LANES), F32)]


def _proj_attn_kernel(x_ref, hist_ref, w_in_f32_ref, pool_w_ref, pool_scale_ref, after_ref,
                      k_ref, v_ref, p_ref, pool_ref, a_ref,
                      w_in_ref, w_kvt_ref, u_ext, xb_ref, stage_q, stage_kvt, kvt_all, qm_ref, decay_ref, acc_ref,
                      *, tm, n_t, n_tiles):
    s = pl.program_id(0)

    @pl.when(s == 0)
    def _():
        w_in_ref[...] = w_in_f32_ref[...].astype(BF16)
        w_kvt_ref[...] = w_in_f32_ref[:, K_COLS:U_COLS].T.astype(BF16)
        stage_q[...] = jnp.zeros_like(stage_q)
        stage_kvt[...] = jnp.zeros_like(stage_kvt)
        u_ext[...] = jnp.zeros_like(u_ext)

    upper = _strict_upper(tm)
    qi = lax.rem(jnp.maximum(s - 1, 0), n_t)

    def kv_block(j):
        start = pl.multiple_of(j * tm, tm)
        return lambda p: (kvt_all[p * LANES:(p + 1) * LANES, pl.ds(start, tm)],
                          kvt_all[SB_WIDTH + p * LANES:SB_WIDTH + (p + 1) * LANES, pl.ds(start, tm)])

    def take_stage():
        kvt_all[:, pl.ds(pl.multiple_of(qi * tm, tm), tm)] = stage_kvt[...]
        _store_masked_queries(stage_q[...], qm_ref)

    def attend_near(fillers):
        _sb_all_heads(qm_ref, kv_block(qi), decay_ref, acc_ref, upper, _causal_mask(tm), True, key_minor=True,
                      fillers=fillers[0:HEAD_PAIRS])
        void = jnp.where(qi == 0, 2.0 * SKIP_LOG2, 0.0)
        return _sb_all_heads(qm_ref, kv_block(jnp.maximum(qi - 1, 0)), decay_ref, acc_ref, upper, None, False,
                             key_minor=True, extra_decay=void, fillers=fillers[HEAD_PAIRS:2 * HEAD_PAIRS])

    def attend_far(least):
        def body(state):
            j, _ = state
            return j - 1, _sb_all_heads(qm_ref, kv_block(j), decay_ref, acc_ref, upper, None, False, key_minor=True)

        lax.while_loop(_more_blocks, body, (qi - 2, least))
        _store_heads(acc_ref, a_ref)

    half = SB_WIDTH // 2

    def project_q(c):
        def piece():
            q = _proj_cols(xb_ref[...], w_in_ref, Q_COLS + c, half) * Q_SCALE
            stage_q[:, c:c + half] = q.astype(BF16)
        return piece

    def project_kv(out_ref, first_row, c):
        def piece():
            yt = _dot_nt(w_kvt_ref[first_row + c:first_row + c + half, :], xb_ref[...])
            out_ref[0, c:c + half, :] = yt
            stage_kvt[first_row + c:first_row + c + half, :] = yt.astype(BF16)
        return piece

    u_halves = []

    def project_u0():
        u_halves.append(_proj_cols(xb_ref[...], w_in_ref, U_COLS, half))

    def project_u1_pool():
        u_halves.append(_proj_cols(xb_ref[...], w_in_ref, U_COLS + half, half))
        pooled = _pool_mix(jnp.concatenate(u_halves, axis=1), hist_ref[0], pool_w_ref, pool_scale_ref, u_ext,
                           lax.rem(s, n_t), tm=tm, start_pos=0)
        for g, y in enumerate(pooled):
            p_ref[0, :, g * POOL_GC:(g + 1) * POOL_GC] = y
        pool_ref[0] = _pool_state(u_ext, tm)

    @pl.when(s < n_tiles)
    def _():
        take_stage()
        xb_ref[...] = x_ref[0].astype(BF16)
        attend_far(attend_near([
            project_q(0), project_q(half),
            project_kv(k_ref, 0, 0), project_kv(k_ref, 0, half),
            project_kv(v_ref, SB_WIDTH, 0), project_kv(v_ref, SB_WIDTH, half),
            project_u0, project_u1_pool]))

    @pl.when(s == n_tiles)
    def _():
        take_stage()
        attend_far(attend_near([]))


def _proj_attn_prompt(x, hist, w_in, pool_w, pool_scale2, after):
    B, T, _ = x.shape
    tm = ATTN_BLOCK
    assert T % tm == 0 and T >= POOL_HIST
    n_t = T // tm
    n_tiles = B * n_t
    kern = functools.partial(_proj_attn_kernel, tm=tm, n_t=n_t, n_tiles=n_tiles)
    proj = lambda s: jnp.minimum(s, n_tiles - 1)
    attn = lambda s: jnp.maximum(s - 1, 0)
    const = lambda *shape: pl.BlockSpec(shape, lambda s: (0,) * len(shape), pipeline_mode=pl.Buffered(1))
    proj_rows = lambda width: pl.BlockSpec((1, tm, width), lambda s: (proj(s) // n_t, proj(s) % n_t, 0))
    proj_cols = pl.BlockSpec((1, SB_WIDTH, tm), lambda s: (proj(s) // n_t, 0, proj(s) % n_t))
    return pl.pallas_call(
        kern,
        grid=(n_tiles + 1,),
        in_specs=[
            proj_rows(D_MODEL),
            pl.BlockSpec((1, HIST_ROWS, POOL_WIDTH), lambda s: (proj(s) // n_t, 0, 0)),
            const(D_MODEL, 3 * SB_WIDTH + POOL_WIDTH),
            const(len(POOL_WINDOWS), POOL_GC, POOL_GC),
            const(1, POOL_WIDTH),
            pl.BlockSpec(memory_space=pl.ANY),
        ],
        out_specs=[
            proj_cols,
            proj_cols,
            proj_rows(POOL_WIDTH),
            pl.BlockSpec((1, POOL_HIST, POOL_WIDTH), lambda s: (proj(s) // n_t, 0, 0)),
            pl.BlockSpec((1, tm, SB_WIDTH), lambda s: (attn(s) // n_t, attn(s) % n_t, 0)),
        ],
        out_shape=[
            jax.ShapeDtypeStruct((B, SB_WIDTH, T), F32),
            jax.ShapeDtypeStruct((B, SB_WIDTH, T), F32),
            jax.ShapeDtypeStruct((B, T, POOL_WIDTH), BF16),
            jax.ShapeDtypeStruct((B, POOL_HIST, POOL_WIDTH), F32),
            jax.ShapeDtypeStruct((B, T, SB_WIDTH), BF16),
        ],
        scratch_shapes=[
            pltpu.VMEM(w_in.shape, BF16),
            pltpu.VMEM((2 * SB_WIDTH, D_MODEL), BF16),
            pltpu.VMEM((HIST_ROWS + tm, POOL_WIDTH), F32),
            pltpu.VMEM((tm, D_MODEL), BF16),
            pltpu.VMEM((tm, SB_WIDTH), BF16),
            pltpu.VMEM((2 * SB_WIDTH, tm), BF16),
            pltpu.VMEM((2 * SB_WIDTH, T), BF16),
        ] + _attn_scratch(tm),
        compiler_params=pltpu.CompilerParams(
            dimension_semantics=("arbitrary",), vmem_limit_bytes=VMEM_LIMIT),
        name="proj_attn_prompt",
    )(x, hist, w_in, pool_w, pool_scale2, after)


def _attn_sample_kernel(q_ref, kn_ref, vn_ref, ck_ref, cv_ref, o_ref, qm_ref, decay_ref, acc_ref,
                        *, t_new, past, blk):
    _store_masked_queries(q_ref[0], qm_ref)
    new_kv = lambda p: (kn_ref[0, :, p * LANES:(p + 1) * LANES], vn_ref[0, :, p * LANES:(p + 1) * LANES])
    least = _sb_all_heads(qm_ref, new_kv, decay_ref, acc_ref, _strict_upper(t_new), _causal_mask(t_new), True)
    upper = _strict_upper(blk)

    def cache_kv(j):
        start = pl.multiple_of(j * blk, blk)
        return lambda p: (ck_ref[0, p * LANES:(p + 1) * LANES, pl.ds(start, blk)].astype(BF16),
                          cv_ref[0, p * LANES:(p + 1) * LANES, pl.ds(start, blk)].astype(BF16))

    def body(state):
        j, _ = state
        return j - 1, _sb_all_heads(qm_ref, cache_kv(j), decay_ref, acc_ref, upper, None, False, key_minor=True)

    lax.while_loop(_more_blocks, body, (past // blk - 1, least))
    _store_heads(acc_ref, o_ref)


def _attn_sample(qkv, cache_kt, cache_vt):
    B, t_new, _ = qkv.shape
    past = cache_kt.shape[2]
    blk = ATTN_BLOCK
    assert past % blk == 0
    kern = functools.partial(_attn_sample_kernel, t_new=t_new, past=past, blk=blk)
    return pl.pallas_call(
        kern,
        grid=(B,),
        in_specs=[
            pl.BlockSpec((1, t_new, SB_WIDTH), lambda b: (b, 0, 0)),
            pl.BlockSpec((1, t_new, SB_WIDTH), lambda b: (b, 0, 1)),
            pl.BlockSpec((1, t_new, SB_WIDTH), lambda b: (b, 0, 2)),
            pl.BlockSpec((1, SB_WIDTH, past), lambda b: (b, 0, 0)),
            pl.BlockSpec((1, SB_WIDTH, past), lambda b: (b, 0, 0)),
        ],
        out_specs=pl.BlockSpec((1, t_new, SB_WIDTH), lambda b: (b, 0, 0)),
        out_shape=jax.ShapeDtypeStruct((B, t_new, SB_WIDTH), BF16),
        scratch_shapes=_attn_scratch(t_new),
        compiler_params=pltpu.CompilerParams(
            dimension_semantics=("arbitrary",), vmem_limit_bytes=VMEM_LIMIT),
        name="sb_attn_sample",
    )(qkv, qkv, qkv, cache_kt, cache_vt)


def _layer_norm(x, g, b):
    mu = jnp.mean(x, axis=-1, keepdims=True)
    xc = x - mu
    var = jnp.mean(xc * xc, axis=-1, keepdims=True)
    return xc * lax.rsqrt(var + LN_EPS) * g + b


def _ffn_rows(x_ref, a_ref, p_ref, y_ref, wo_ref, g1, b1, wg_ref, wu_ref, wd_ref, g2, b2, acc_ref):
    m = x_ref.shape[0]
    halves = tuple(slice(i * m // FFN_PARTS, (i + 1) * m // FFN_PARTS) for i in range(FFN_PARTS))
    chunks = list(range(0, D_FF, FF_CHUNK))

    def swiglu_chunk(x1b, c):
        gate = _dot(x1b, wg_ref[:, c:c + FF_CHUNK])
        up = _dot(x1b, wu_ref[:, c:c + FF_CHUNK])
        hmid = (gate * jax.nn.sigmoid(gate) * up).astype(BF16)
        return _dot(hmid, wd_ref[c:c + FF_CHUNK, :])

    mix = [_dot(a_ref[r, :], wo_ref[0:SB_WIDTH, :]) + _dot(p_ref[r, :], wo_ref[SB_WIDTH:, :]) for r in halves]
    x1, x1b = [], []
    for r, mx in zip(halves, mix):
        x1.append(_layer_norm(ALPHA * x_ref[r, :] + mx, g1, b1))
        x1b.append(x1[-1].astype(BF16))
        acc_ref[r, :] = swiglu_chunk(x1b[-1], chunks[0])
    x1b_all = jnp.concatenate(x1b, axis=0)
    for c in chunks[1:-1]:
        acc_ref[...] += swiglu_chunk(x1b_all, c)
    last = [swiglu_chunk(xb, chunks[-1]) for xb in x1b]
    for r, x1_r, part in zip(halves, x1, last):
        y_ref[r, :] = _layer_norm(ALPHA * x1_r + acc_ref[r, :] + part, g2, b2)


def _out_ffn_kernel(xp_ref, ap_ref, pp_ref, xs_ref, as_ref, ps_ref, wo_c, wg_c, wu_c, wd_c,
                    g1_ref, b1_ref, g2_ref, b2_ref, yp_ref, ys_ref,
                    wo_s, wg_s, wu_s, wd_s, accp_ref, accs_ref, *, n_conv, n_tiles):
    i = pl.program_id(0)

    @pl.when(i < n_conv)
    def _():
        for c_ref, s_ref in ((wo_c, wo_s), (wg_c, wg_s), (wu_c, wu_s), (wd_c, wd_s)):
            rows = c_ref.shape[0]
            s_ref[pl.ds(pl.multiple_of(i * rows, rows), rows), :] = c_ref[...].astype(BF16)

    norms = (g1_ref, b1_ref, g2_ref, b2_ref)

    def rows(x_ref, a_ref, p_ref, y_ref, acc_ref):
        g1, b1, g2, b2 = (r[...] for r in norms)
        _ffn_rows(x_ref, a_ref, p_ref, y_ref, wo_s, g1, b1, wg_s, wu_s, wd_s, g2, b2, acc_ref)

    @pl.when(jnp.logical_and(i >= n_conv, i < n_conv + n_tiles))
    def _():
        rows(xp_ref, ap_ref, pp_ref, yp_ref, accp_ref)

    @pl.when(i == n_conv + n_tiles)
    def _():
        rows(xs_ref, as_ref, ps_ref, ys_ref, accs_ref)


def _out_ffn(xp, ap, pp, xs, a_s, ps, w_out, w_gate, w_up, w_down, g1, b1, g2, b2, *, tm, n_conv):
    Np, Ns = xp.shape[0], xs.shape[0]
    assert Np % tm == 0 and D_FF % FF_CHUNK == 0
    n_tiles = Np // tm
    tile = lambda i: jnp.clip(i - n_conv, 0, n_tiles - 1)
    row = lambda width: pl.BlockSpec((tm, width), lambda i: (tile(i), 0))
    once = lambda *shape: pl.BlockSpec(shape, lambda i: (0,) * len(shape), pipeline_mode=pl.Buffered(1))

    def chunked(w):
        assert w.shape[0] % (n_conv * 16) == 0
        return pl.BlockSpec((w.shape[0] // n_conv, w.shape[1]), lambda i: (jnp.minimum(i, n_conv - 1), 0))

    kern = functools.partial(_out_ffn_kernel, n_conv=n_conv, n_tiles=n_tiles)
    return pl.pallas_call(
        kern,
        grid=(n_conv + n_tiles + 1,),
        in_specs=[
            row(D_MODEL), row(SB_WIDTH), row(POOL_WIDTH),
            once(Ns, D_MODEL), once(Ns, SB_WIDTH), once(Ns, POOL_WIDTH),
            chunked(w_out), chunked(w_gate), chunked(w_up), chunked(w_down),
            once(1, D_MODEL), once(1, D_MODEL), once(1, D_MODEL), once(1, D_MODEL),
        ],
        out_specs=[row(D_MODEL), pl.BlockSpec((Ns, D_MODEL), lambda i: (0, 0))],
        out_shape=[jax.ShapeDtypeStruct((Np, D_MODEL), F32), jax.ShapeDtypeStruct((Ns, D_MODEL), F32)],
        scratch_shapes=[pltpu.VMEM(w.shape, BF16) for w in (w_out, w_gate, w_up, w_down)]
        + [pltpu.VMEM((tm, D_MODEL), F32), pltpu.VMEM((Ns, D_MODEL), F32)],
        compiler_params=pltpu.CompilerParams(
            dimension_semantics=("arbitrary",), vmem_limit_bytes=VMEM_LIMIT),
        name="out_proj_ffn",
    )(xp, ap, pp, xs, a_s, ps, w_out, w_gate, w_up, w_down, g1, b1, g2, b2)


def kernel(x_prompt, x_sample, cache_k, cache_v, state_pool, w_in, pool_w, pool_scale, w_out,
           ln1_g, ln1_b, w_gate, w_up, w_down, ln2_g, ln2_b):
    Bp, Tp, _ = x_prompt.shape
    Bs, Ts, _ = x_sample.shape
    past = cache_k.shape[1]

    row2 = lambda a: a.reshape(1, -1).astype(F32)
    pool_scale2 = row2(pool_scale)

    hist_s = jnp.concatenate([jnp.zeros((Bs, HIST_ROWS - POOL_HIST, POOL_WIDTH), F32), state_pool.astype(F32)], axis=1)
    k_s, v_s, qkv_s, mix_s, pool_s = _in_proj(x_sample, hist_s, w_in, pool_w, pool_scale2, tm=Ts, start_pos=past)
    feature_major = lambda c: jnp.transpose(c, (0, 2, 3, 1)).reshape(Bs, SB_WIDTH, past)
    a_s = _attn_sample(qkv_s, feature_major(cache_k), feature_major(cache_v))

    hist_p = jnp.zeros((Bp, HIST_ROWS, POOL_WIDTH), F32)
    kt_p, vt_p, mix_p, pool_p, a_p = _proj_attn_prompt(x_prompt, hist_p, w_in, pool_w, pool_scale2, after=a_s)
    position_major = lambda c: jnp.transpose(c.reshape(Bp, SB_HEADS, SB_HEAD_DIM, Tp), (0, 3, 1, 2))

    y_p, y_s = _out_ffn(
        x_prompt.reshape(Bp * Tp, D_MODEL), a_p.reshape(Bp * Tp, SB_WIDTH), mix_p.reshape(Bp * Tp, POOL_WIDTH),
        x_sample.reshape(Bs * Ts, D_MODEL), a_s.reshape(Bs * Ts, SB_WIDTH), mix_s.reshape(Bs * Ts, POOL_WIDTH),
        w_out, w_gate, w_up, w_down, row2(ln1_g), row2(ln1_b), row2(ln2_g), row2(ln2_b), tm=512, n_conv=16)

    heads = lambda a: a.reshape(a.shape[0], a.shape[1], SB_HEADS, SB_HEAD_DIM)
    return (y_p.reshape(Bp, Tp, D_MODEL), y_s.reshape(Bs, Ts, D_MODEL),
            position_major(kt_p), position_major(vt_p), pool_p, heads(k_s), heads(v_s), pool_s)
```

```python
import functools

import jax
import jax.numpy as jnp
from jax import lax
from jax.experimental import pallas as pl
from jax.experimental.pallas import tpu as pltpu

D_MODEL = 1024
SB_WIDTH = 512
SB_HEADS = 8
SB_HEAD_DIM = 64
POOL_WIDTH = 512
POOL_WINDOWS = (2, 4, 8, 16)
POOL_GC = 128
POOL_HIST = 15
HIST_ROWS = 16
D_FF = 2816
DEPTH = 1
ALPHA = (2 * DEPTH) ** 0.25
LN_EPS = 1e-5
LANES = 128
HEAD_PAIRS = SB_WIDTH // LANES
LOG2E = 1.4426950408889634
Q_SCALE = SB_HEAD_DIM ** -0.5 * LOG2E
SKIP_LOG2 = 160.0
ATTN_BLOCK = 256
FF_CHUNKS = (256, 768, 768, 768, 256)
FFN_PARTS = 2
VMEM_LIMIT = 56 * 1024 * 1024

F32 = jnp.float32
BF16 = jnp.bfloat16


def _dot(a, b):
    return jnp.dot(a, b, preferred_element_type=F32)


def _dot_nt(a, b):
    return lax.dot_general(a, b, (((1,), (1,)), ((), ())), preferred_element_type=F32)


Q_COLS, K_COLS, V_COLS, U_COLS = (SB_WIDTH * i for i in range(4))


def _proj_cols(x, w_in_ref, start, width):
    return _dot(x, w_in_ref[:, start:start + width])


def _pool_mix(u, hist, pool_w_ref, pool_scale_ref, u_ext, t, *, tm, start_pos):
    u_ext[0:HIST_ROWS, :] = jnp.where(t == 0, hist, u_ext[tm:tm + HIST_ROWS, :])
    u_ext[HIST_ROWS:HIST_ROWS + tm, :] = u

    pos = start_pos + t * tm + lax.broadcasted_iota(jnp.int32, (tm, 1), 0)
    pooled = []
    for g, w in enumerate(POOL_WINDOWS):
        cols = slice(g * POOL_GC, (g + 1) * POOL_GC)
        s = u_ext[:, cols]
        span = 1
        while span < w:
            s = s + pltpu.roll(s, span, 0)
            span *= 2
        cnt = jnp.minimum(w, pos + 1).astype(F32)
        d = s[HIST_ROWS:, :] / cnt - u[:, cols]
        y = _dot(d.astype(BF16), pool_w_ref[g].astype(BF16)) * pool_scale_ref[:, cols]
        pooled.append(y.astype(BF16))
    return pooled


def _pool_state(u_ext, tm):
    return u_ext[tm + HIST_ROWS - POOL_HIST:tm + HIST_ROWS, :]


def _strict_upper(n):
    r = lax.broadcasted_iota(jnp.int32, (n, n), 0)
    c = lax.broadcasted_iota(jnp.int32, (n, n), 1)
    return (r > c).astype(BF16)


def _causal_mask(n):
    r = lax.broadcasted_iota(jnp.int32, (n, n), 0)
    c = lax.broadcasted_iota(jnp.int32, (n, n), 1)
    return c < r


def _store_masked_queries(q, qm_ref):
    m = q.shape[0]
    lower_half = lax.broadcasted_iota(jnp.int32, (m, LANES), 1) < SB_HEAD_DIM
    for p in range(HEAD_PAIRS):
        q2 = q[:, p * LANES:(p + 1) * LANES]
        zero = jnp.zeros_like(q2)
        qm_ref[p, 0:m, :] = jnp.where(lower_half, q2, zero)
        qm_ref[p, m:2 * m, :] = jnp.where(lower_half, zero, q2)


def _sb_all_heads(qm_ref, kv_block, decay_ref, acc_ref, upper, mask, first, key_minor=False, extra_decay=None,
                  fillers=()):
    fillers = list(fillers)
    m = qm_ref.shape[1] // 2
    lower_half = lax.broadcasted_iota(jnp.int32, (m, LANES), 1) < SB_HEAD_DIM
    mask2 = None if mask is None else jnp.concatenate([mask, mask], axis=0)
    kv = [kv_block(p) for p in range(HEAD_PAIRS)]

    logit, sps, rowsums = [], [], []
    for p in range(HEAD_PAIRS):
        z = (_dot if key_minor else _dot_nt)(qm_ref[p], kv[p][0])
        sp = jnp.maximum(z, 0.0) + jnp.log(1.0 + jnp.exp2(-jnp.abs(z))) * LOG2E
        spm = sp if mask2 is None else jnp.where(mask2, sp, 0.0)
        sps.append(spm.astype(BF16))
        logit.append(z - sp)
        rowsums.append(jnp.sum(spm, axis=1, keepdims=True))
        if fillers:
            fillers.pop(0)()

    later = _dot(jnp.concatenate(sps, axis=0), upper)

    least = None
    for p in range(HEAD_PAIRS):
        a = jnp.exp2(logit[p] - later[2 * m * p:2 * m * (p + 1)])
        if mask2 is not None:
            a = jnp.where(mask2, a, 0.0)
        pv = (_dot_nt if key_minor else _dot)(a.astype(BF16), kv[p][1])
        merged = jnp.where(lower_half, pv[0:m], pv[m:2 * m])
        rowsum = jnp.where(lower_half, rowsums[p][0:m], rowsums[p][m:2 * m])
        if first:
            decay = rowsum
            acc_ref[p] = merged
        else:
            prev = decay_ref[p]
            if extra_decay is not None:
                prev = prev + extra_decay
            decay = prev + rowsum
            acc_ref[p] += merged * jnp.exp2(-prev)
        decay_ref[p] = decay
        least = decay if least is None else jnp.minimum(least, decay)
    return jnp.min(least)


def _more_blocks(state):
    j, least = state
    return jnp.logical_and(j >= 0, least < SKIP_LOG2)


def _store_heads(acc_ref, o_ref):
    for p in range(HEAD_PAIRS):
        o_ref[0, :, p * LANES:(p + 1) * LANES] = acc_ref[p].astype(BF16)


def _attn_scratch(m):
    return [pltpu.VMEM((HEAD_PAIRS, 2 * m, LANES), BF16), pltpu.VMEM((HEAD_PAIRS, m, LANES), F32),
            pltpu.VMEM((HEAD_PAIRS, m, LANES), F32)]


def _proj_attn_kernel(x_ref, hist_ref, w_in_f32_ref, pool_w_ref, pool_scale_ref, after_ref,
                      k_ref, v_ref, p_ref, pool_ref, a_ref,
                      w_in_ref, w_kvt_ref, u_ext, xb_ref, stage_q, stage_kvt, kvt_all, qm_ref, decay_ref, acc_ref,
                      *, tm, n_t, n_tiles):
    s = pl.program_id(0)

    @pl.when(s == 0)
    def _():
        w_in_ref[...] = w_in_f32_ref[...].astype(BF16)
        w_kvt_ref[...] = w_in_f32_ref[:, K_COLS:U_COLS].T.astype(BF16)
        stage_q[...] = jnp.zeros_like(stage_q)
        stage_kvt[...] = jnp.zeros_like(stage_kvt)
        u_ext[...] = jnp.zeros_like(u_ext)

    upper = _strict_upper(tm)
    qi = lax.rem(jnp.maximum(s - 1, 0), n_t)

    def kv_block(j):
        start = pl.multiple_of(j * tm, tm)
        return lambda p: (kvt_all[p * LANES:(p + 1) * LANES, pl.ds(start, tm)],
                          kvt_all[SB_WIDTH + p * LANES:SB_WIDTH + (p + 1) * LANES, pl.ds(start, tm)])

    def take_stage():
        kvt_all[:, pl.ds(pl.multiple_of(qi * tm, tm), tm)] = stage_kvt[...]
        _store_masked_queries(stage_q[...], qm_ref)

    def attend_near(fillers):
        _sb_all_heads(qm_ref, kv_block(qi), decay_ref, acc_ref, upper, _causal_mask(tm), True, key_minor=True,
                      fillers=fillers[0:HEAD_PAIRS])
        void = jnp.where(qi == 0, 2.0 * SKIP_LOG2, 0.0)
        return _sb_all_heads(qm_ref, kv_block(jnp.maximum(qi - 1, 0)), decay_ref, acc_ref, upper, None, False,
                             key_minor=True, extra_decay=void, fillers=fillers[HEAD_PAIRS:2 * HEAD_PAIRS])

    def attend_far(least):
        def body(state):
            j, _ = state
            return j - 1, _sb_all_heads(qm_ref, kv_block(j), decay_ref, acc_ref, upper, None, False, key_minor=True)

        lax.while_loop(_more_blocks, body, (qi - 2, least))
        _store_heads(acc_ref, a_ref)

    half = SB_WIDTH // 2

    def project_q(c):
        def piece():
            q = _proj_cols(xb_ref[...], w_in_ref, Q_COLS + c, half) * Q_SCALE
            stage_q[:, c:c + half] = q.astype(BF16)
        return piece

    def project_kv(out_ref, first_row, c):
        def piece():
            yt = _dot_nt(w_kvt_ref[first_row + c:first_row + c + half, :], xb_ref[...])
            out_ref[0, c:c + half, :] = yt
            stage_kvt[first_row + c:first_row + c + half, :] = yt.astype(BF16)
        return piece

    u_halves = []

    def project_u0():
        u_halves.append(_proj_cols(xb_ref[...], w_in_ref, U_COLS, half))

    def project_u1_pool():
        u_halves.append(_proj_cols(xb_ref[...], w_in_ref, U_COLS + half, half))
        pooled = _pool_mix(jnp.concatenate(u_halves, axis=1), hist_ref[0], pool_w_ref, pool_scale_ref, u_ext,
                           lax.rem(s, n_t), tm=tm, start_pos=0)
        for g, y in enumerate(pooled):
            p_ref[0, :, g * POOL_GC:(g + 1) * POOL_GC] = y
        pool_ref[0] = _pool_state(u_ext, tm)

    @pl.when(s < n_tiles)
    def _():
        take_stage()
        xb_ref[...] = x_ref[0].astype(BF16)
        attend_far(attend_near([
            project_q(0), project_q(half),
            project_kv(k_ref, 0, 0), project_kv(k_ref, 0, half),
            project_kv(v_ref, SB_WIDTH, 0), project_kv(v_ref, SB_WIDTH, half),
            project_u0, project_u1_pool]))

    @pl.when(s == n_tiles)
    def _():
        take_stage()
        attend_far(attend_near([]))


def _proj_attn_prompt(x, hist, w_in, pool_w, pool_scale2, after):
    B, T, _ = x.shape
    tm = ATTN_BLOCK
    assert T % tm == 0 and T >= POOL_HIST
    n_t = T // tm
    n_tiles = B * n_t
    kern = functools.partial(_proj_attn_kernel, tm=tm, n_t=n_t, n_tiles=n_tiles)
    proj = lambda s: jnp.minimum(s, n_tiles - 1)
    attn = lambda s: jnp.maximum(s - 1, 0)
    const = lambda *shape: pl.BlockSpec(shape, lambda s: (0,) * len(shape), pipeline_mode=pl.Buffered(1))
    proj_rows = lambda width: pl.BlockSpec((1, tm, width), lambda s: (proj(s) // n_t, proj(s) % n_t, 0))
    proj_cols = pl.BlockSpec((1, SB_WIDTH, tm), lambda s: (proj(s) // n_t, 0, proj(s) % n_t))
    return pl.pallas_call(
        kern,
        grid=(n_tiles + 1,),
        in_specs=[
            proj_rows(D_MODEL),
            pl.BlockSpec((1, HIST_ROWS, POOL_WIDTH), lambda s: (proj(s) // n_t, 0, 0)),
            const(D_MODEL, 3 * SB_WIDTH + POOL_WIDTH),
            const(len(POOL_WINDOWS), POOL_GC, POOL_GC),
            const(1, POOL_WIDTH),
            pl.BlockSpec(memory_space=pl.ANY),
        ],
        out_specs=[
            proj_cols,
            proj_cols,
            proj_rows(POOL_WIDTH),
            pl.BlockSpec((1, POOL_HIST, POOL_WIDTH), lambda s: (proj(s) // n_t, 0, 0)),
            pl.BlockSpec((1, tm, SB_WIDTH), lambda s: (attn(s) // n_t, attn(s) % n_t, 0)),
        ],
        out_shape=[
            jax.ShapeDtypeStruct((B, SB_WIDTH, T), F32),
            jax.ShapeDtypeStruct((B, SB_WIDTH, T), F32),
            jax.ShapeDtypeStruct((B, T, POOL_WIDTH), BF16),
            jax.ShapeDtypeStruct((B, POOL_HIST, POOL_WIDTH), F32),
            jax.ShapeDtypeStruct((B, T, SB_WIDTH), BF16),
        ],
        scratch_shapes=[
            pltpu.VMEM(w_in.shape, BF16),
            pltpu.VMEM((2 * SB_WIDTH, D_MODEL), BF16),
            pltpu.VMEM((HIST_ROWS + tm, POOL_WIDTH), F32),
            pltpu.VMEM((tm, D_MODEL), BF16),
            pltpu.VMEM((tm, SB_WIDTH), BF16),
            pltpu.VMEM((2 * SB_WIDTH, tm), BF16),
            pltpu.VMEM((2 * SB_WIDTH, T), BF16),
        ] + _attn_scratch(tm),
        compiler_params=pltpu.CompilerParams(
            dimension_semantics=("arbitrary",), vmem_limit_bytes=VMEM_LIMIT),
        name="proj_attn_prompt",
    )(x, hist, w_in, pool_w, pool_scale2, after)


def _proj_attn_sample_kernel(x_ref, hist_ref, w_in_f32_ref, pool_w_ref, pool_scale_ref,
                             ck_ref, cv_ref, ck_hbm, cv_hbm,
                             k_ref, v_ref, p_ref, pool_ref, a_ref,
                             w_in_ref, q_all, kv_new, u_all, u_ext, ck_buf, cv_buf, qm_ref, decay_ref, acc_ref,
                             *, t_new, past, blk):
    b = pl.program_id(0)
    rows = pl.ds(pl.multiple_of(b * t_new, t_new), t_new)

    @pl.when(b == 0)
    def _():
        w_in_ref[...] = w_in_f32_ref[...].astype(BF16)
        u_ext[...] = jnp.zeros_like(u_ext)
        x = x_ref[...].astype(BF16)
        q_all[...] = (_proj_cols(x, w_in_ref, Q_COLS, SB_WIDTH) * Q_SCALE).astype(BF16)
        k = _proj_cols(x, w_in_ref, K_COLS, SB_WIDTH)
        k_ref[...] = k
        kv_new[:, 0:SB_WIDTH] = k.astype(BF16)
        v = _proj_cols(x, w_in_ref, V_COLS, SB_WIDTH)
        v_ref[...] = v
        kv_new[:, SB_WIDTH:2 * SB_WIDTH] = v.astype(BF16)
        u_all[...] = _proj_cols(x, w_in_ref, U_COLS, POOL_WIDTH)

    pooled = _pool_mix(u_all[rows, :], hist_ref[0], pool_w_ref, pool_scale_ref, u_ext, 0, tm=t_new, start_pos=past)
    for g, y in enumerate(pooled):
        p_ref[rows, g * POOL_GC:(g + 1) * POOL_GC] = y
    pool_ref[b] = _pool_state(u_ext, t_new)

    _store_masked_queries(q_all[rows, :], qm_ref)
    new_kv = lambda p: (kv_new[rows, p * LANES:(p + 1) * LANES],
                        kv_new[rows, SB_WIDTH + p * LANES:SB_WIDTH + (p + 1) * LANES])
    _sb_all_heads(qm_ref, new_kv, decay_ref, acc_ref, _strict_upper(t_new), _causal_mask(t_new), True)
    upper = _strict_upper(blk)
    recent = lambda p: (ck_ref[0, p * LANES:(p + 1) * LANES, :].astype(BF16),
                        cv_ref[0, p * LANES:(p + 1) * LANES, :].astype(BF16))
    least = _sb_all_heads(qm_ref, recent, decay_ref, acc_ref, upper, None, False, key_minor=True)

    def body(state):
        j, _ = state
        start = pl.multiple_of(j * blk, blk)
        pltpu.sync_copy(ck_hbm.at[b, :, pl.ds(start, blk)], ck_buf)
        pltpu.sync_copy(cv_hbm.at[b, :, pl.ds(start, blk)], cv_buf)
        older = lambda p: (ck_buf[p * LANES:(p + 1) * LANES, :].astype(BF16),
                           cv_buf[p * LANES:(p + 1) * LANES, :].astype(BF16))
        return j - 1, _sb_all_heads(qm_ref, older, decay_ref, acc_ref, upper, None, False, key_minor=True)

    lax.while_loop(_more_blocks, body, (past // blk - 2, least))
    for p in range(HEAD_PAIRS):
        a_ref[rows, p * LANES:(p + 1) * LANES] = acc_ref[p].astype(BF16)


def _proj_attn_sample(x, hist, w_in, pool_w, pool_scale2, cache_kt, cache_vt):
    B, t_new, _ = x.shape
    past = cache_kt.shape[2]
    blk = ATTN_BLOCK
    assert past % blk == 0 and t_new >= HIST_ROWS
    n = B * t_new
    kern = functools.partial(_proj_attn_sample_kernel, t_new=t_new, past=past, blk=blk)
    const = lambda *shape: pl.BlockSpec(shape, lambda b: (0,) * len(shape), pipeline_mode=pl.Buffered(1))
    whole = lambda *shape: pl.BlockSpec(shape, lambda b: (0,) * len(shape))
    recent = pl.BlockSpec((1, SB_WIDTH, blk), lambda b: (b, 0, past // blk - 1))
    k, v, p, pool, a = pl.pallas_call(
        kern,
        grid=(B,),
        in_specs=[
            const(n, D_MODEL),
            pl.BlockSpec((1, HIST_ROWS, POOL_WIDTH), lambda b: (b, 0, 0)),
            const(D_MODEL, 3 * SB_WIDTH + POOL_WIDTH),
            const(len(POOL_WINDOWS), POOL_GC, POOL_GC),
            const(1, POOL_WIDTH),
            recent, recent,
            pl.BlockSpec(memory_space=pl.ANY), pl.BlockSpec(memory_space=pl.ANY),
        ],
        out_specs=[whole(n, SB_WIDTH), whole(n, SB_WIDTH), whole(n, POOL_WIDTH),
                   whole(B, POOL_HIST, POOL_WIDTH), whole(n, SB_WIDTH)],
        out_shape=[
            jax.ShapeDtypeStruct((n, SB_WIDTH), F32),
            jax.ShapeDtypeStruct((n, SB_WIDTH), F32),
            jax.ShapeDtypeStruct((n, POOL_WIDTH), BF16),
            jax.ShapeDtypeStruct((B, POOL_HIST, POOL_WIDTH), F32),
            jax.ShapeDtypeStruct((n, SB_WIDTH), BF16),
        ],
        scratch_shapes=[
            pltpu.VMEM(w_in.shape, BF16),
            pltpu.VMEM((n, SB_WIDTH), BF16),
            pltpu.VMEM((n, 2 * SB_WIDTH), BF16),
            pltpu.VMEM((n, POOL_WIDTH), F32),
            pltpu.VMEM((HIST_ROWS + t_new, POOL_WIDTH), F32),
            pltpu.VMEM((SB_WIDTH, blk), F32),
            pltpu.VMEM((SB_WIDTH, blk), F32),
        ] + _attn_scratch(t_new),
        compiler_params=pltpu.CompilerParams(
            dimension_semantics=("arbitrary",), vmem_limit_bytes=VMEM_LIMIT),
        name="proj_attn_sample",
    )(x.reshape(n, D_MODEL), hist, w_in, pool_w, pool_scale2, cache_kt, cache_vt, cache_kt, cache_vt)
    return k, v, p, pool, a


def _layer_norm(x, g, b):
    mu = jnp.mean(x, axis=-1, keepdims=True)
    xc = x - mu
    var = jnp.mean(xc * xc, axis=-1, keepdims=True)
    return xc * lax.rsqrt(var + LN_EPS) * g + b


def _ffn_rows(x_ref, a_ref, p_ref, y_ref, wo_ref, g1, b1, wg_ref, wu_ref, wd_ref, g2, b2, acc_ref):
    m = x_ref.shape[0]
    halves = tuple(slice(i * m // FFN_PARTS, (i + 1) * m // FFN_PARTS) for i in range(FFN_PARTS))
    chunks = [(sum(FF_CHUNKS[:i]), w) for i, w in enumerate(FF_CHUNKS)]

    def swiglu_chunk(x1b, chunk):
        c, w = chunk
        gate = _dot(x1b, wg_ref[:, c:c + w])
        up = _dot(x1b, wu_ref[:, c:c + w])
        hmid = (gate * jax.nn.sigmoid(gate) * up).astype(BF16)
        return _dot(hmid, wd_ref[c:c + w, :])

    mix = [_dot(a_ref[r, :], wo_ref[0:SB_WIDTH, :]) + _dot(p_ref[r, :], wo_ref[SB_WIDTH:, :]) for r in halves]
    x1, x1b = [], []
    for r, mx in zip(halves, mix):
        x1.append(_layer_norm(ALPHA * x_ref[r, :] + mx, g1, b1))
        x1b.append(x1[-1].astype(BF16))
        acc_ref[r, :] = swiglu_chunk(x1b[-1], chunks[0])
    x1b_all = jnp.concatenate(x1b, axis=0)
    for c in chunks[1:-1]:
        acc_ref[...] += swiglu_chunk(x1b_all, c)
    last = [swiglu_chunk(xb, chunks[-1]) for xb in x1b]
    for r, x1_r, part in zip(halves, x1, last):
        y_ref[r, :] = _layer_norm(ALPHA * x1_r + acc_ref[r, :] + part, g2, b2)


def _out_ffn_kernel(xp_ref, ap_ref, pp_ref, xs_ref, as_ref, ps_ref, wo_c, wg_c, wu_c, wd_c,
                    g1_ref, b1_ref, g2_ref, b2_ref, yp_ref, ys_ref,
                    wo_s, wg_s, wu_s, wd_s, accp_ref, accs_ref, *, n_conv, n_tiles):
    i = pl.program_id(0)

    @pl.when(i < n_conv)
    def _():
        for c_ref, s_ref in ((wo_c, wo_s), (wg_c, wg_s), (wu_c, wu_s), (wd_c, wd_s)):
            rows = c_ref.shape[0]
            s_ref[pl.ds(pl.multiple_of(i * rows, rows), rows), :] = c_ref[...].astype(BF16)

    norms = (g1_ref, b1_ref, g2_ref, b2_ref)

    def rows(x_ref, a_ref, p_ref, y_ref, acc_ref):
        g1, b1, g2, b2 = (r[...] for r in norms)
        _ffn_rows(x_ref, a_ref, p_ref, y_ref, wo_s, g1, b1, wg_s, wu_s, wd_s, g2, b2, acc_ref)

    @pl.when(jnp.logical_and(i >= n_conv, i < n_conv + n_tiles))
    def _():
        rows(xp_ref, ap_ref, pp_ref, yp_ref, accp_ref)

    @pl.when(i == n_conv + n_tiles)
    def _():
        rows(xs_ref, as_ref, ps_ref, ys_ref, accs_ref)


def _out_ffn(xp, ap, pp, xs, a_s, ps, w_out, w_gate, w_up, w_down, g1, b1, g2, b2, *, tm, n_conv):
    Np, Ns = xp.shape[0], xs.shape[0]
    assert Np % tm == 0 and sum(FF_CHUNKS) == D_FF
    n_tiles = Np // tm
    tile = lambda i: jnp.clip(i - n_conv, 0, n_tiles - 1)
    row = lambda width: pl.BlockSpec((tm, width), lambda i: (tile(i), 0))
    once = lambda *shape: pl.BlockSpec(shape, lambda i: (0,) * len(shape), pipeline_mode=pl.Buffered(1))

    def chunked(w):
        assert w.shape[0] % (n_conv * 16) == 0
        return pl.BlockSpec((w.shape[0] // n_conv, w.shape[1]), lambda i: (jnp.minimum(i, n_conv - 1), 0))

    kern = functools.partial(_out_ffn_kernel, n_conv=n_conv, n_tiles=n_tiles)
    return pl.pallas_call(
        kern,
        grid=(n_conv + n_tiles + 1,),
        in_specs=[
            row(D_MODEL), row(SB_WIDTH), row(POOL_WIDTH),
            once(Ns, D_MODEL), once(Ns, SB_WIDTH), once(Ns, POOL_WIDTH),
            chunked(w_out), chunked(w_gate), chunked(w_up), chunked(w_down),
            once(1, D_MODEL), once(1, D_MODEL), once(1, D_MODEL), once(1, D_MODEL),
        ],
        out_specs=[row(D_MODEL), pl.BlockSpec((Ns, D_MODEL), lambda i: (0, 0))],
        out_shape=[jax.ShapeDtypeStruct((Np, D_MODEL), F32), jax.ShapeDtypeStruct((Ns, D_MODEL), F32)],
        scratch_shapes=[pltpu.VMEM(w.shape, BF16) for w in (w_out, w_gate, w_up, w_down)]
        + [pltpu.VMEM((tm, D_MODEL), F32), pltpu.VMEM((Ns, D_MODEL), F32)],
        compiler_params=pltpu.CompilerParams(
            dimension_semantics=("arbitrary",), vmem_limit_bytes=VMEM_LIMIT),
        name="out_proj_ffn",
    )(xp, ap, pp, xs, a_s, ps, w_out, w_gate, w_up, w_down, g1, b1, g2, b2)


def kernel(x_prompt, x_sample, cache_k, cache_v, state_pool, w_in, pool_w, pool_scale, w_out,
           ln1_g, ln1_b, w_gate, w_up, w_down, ln2_g, ln2_b):
    Bp, Tp, _ = x_prompt.shape
    Bs, Ts, _ = x_sample.shape
    past = cache_k.shape[1]

    row2 = lambda a: a.reshape(1, -1).astype(F32)
    pool_scale2 = row2(pool_scale)

    hist_s = jnp.concatenate([jnp.zeros((Bs, HIST_ROWS - POOL_HIST, POOL_WIDTH), F32), state_pool.astype(F32)], axis=1)
    feature_major = lambda c: jnp.transpose(c, (0, 2, 3, 1)).reshape(Bs, SB_WIDTH, past)
    k_s, v_s, mix_s, pool_s, a_s = _proj_attn_sample(x_sample, hist_s, w_in, pool_w, pool_scale2,
                                                     feature_major(cache_k), feature_major(cache_v))

    hist_p = jnp.zeros((Bp, HIST_ROWS, POOL_WIDTH), F32)
    kt_p, vt_p, mix_p, pool_p, a_p = _proj_attn_prompt(x_prompt, hist_p, w_in, pool_w, pool_scale2, after=a_s)
    position_major = lambda c: jnp.transpose(c.reshape(Bp, SB_HEADS, SB_HEAD_DIM, Tp), (0, 3, 1, 2))

    y_p, y_s = _out_ffn(
        x_prompt.reshape(Bp * Tp, D_MODEL), a_p.reshape(Bp * Tp, SB_WIDTH), mix_p.reshape(Bp * Tp, POOL_WIDTH),
        x_sample.reshape(Bs * Ts, D_MODEL), a_s, mix_s,
        w_out, w_gate, w_up, w_down, row2(ln1_g), row2(ln1_b), row2(ln2_g), row2(ln2_b), tm=512, n_conv=16)

    heads = lambda a: a.reshape(Bs, Ts, SB_HEADS, SB_HEAD_DIM)
    return (y_p.reshape(Bp, Tp, D_MODEL), y_s.reshape(Bs, Ts, D_MODEL),
            position_major(kt_p), position_major(vt_p), pool_p, heads(k_s), heads(v_s), pool_s)
```

```python
import functools

import jax
import jax.numpy as jnp
from jax import lax
from jax.experimental import pallas as pl
from jax.experimental.pallas import tpu as pltpu

D_MODEL = 1024
SB_WIDTH = 512
SB_HEADS = 8
SB_HEAD_DIM = 64
POOL_WIDTH = 512
POOL_WINDOWS = (2, 4, 8, 16)
POOL_GC = 128
POOL_HIST = 15
HIST_ROWS = 16
D_FF = 2816
DEPTH = 1
ALPHA = (2 * DEPTH) ** 0.25
LN_EPS = 1e-5
LANES = 128
HEAD_PAIRS = SB_WIDTH // LANES
LOG2E = 1.4426950408889634
Q_SCALE = SB_HEAD_DIM ** -0.5 * LOG2E
SKIP_LOG2 = 160.0
ATTN_BLOCK = 256
FF_CHUNKS = (256, 768, 768, 768, 256)
FFN_PARTS = 2
VMEM_LIMIT = 60 * 1024 * 1024

F32 = jnp.float32
BF16 = jnp.bfloat16


def _dot(a, b):
    return jnp.dot(a, b, preferred_element_type=F32)


def _dot_nt(a, b):
    return lax.dot_general(a, b, (((1,), (1,)), ((), ())), preferred_element_type=F32)


Q_COLS, K_COLS, V_COLS, U_COLS = (SB_WIDTH * i for i in range(4))


def _proj_cols(x, w_in_ref, start, width):
    return _dot(x, w_in_ref[:, start:start + width])


def _pool_mix(u, hist, pool_w_ref, pool_scale_ref, u_ext, t, *, tm, start_pos):
    u_ext[0:HIST_ROWS, :] = jnp.where(t == 0, hist, u_ext[tm:tm + HIST_ROWS, :])
    u_ext[HIST_ROWS:HIST_ROWS + tm, :] = u

    pos = start_pos + t * tm + lax.broadcasted_iota(jnp.int32, (tm, 1), 0)
    pooled = []
    for g, w in enumerate(POOL_WINDOWS):
        cols = slice(g * POOL_GC, (g + 1) * POOL_GC)
        s = u_ext[:, cols]
        span = 1
        while span < w:
            s = s + pltpu.roll(s, span, 0)
            span *= 2
        cnt = jnp.minimum(w, pos + 1).astype(F32)
        d = s[HIST_ROWS:, :] / cnt - u[:, cols]
        y = _dot(d.astype(BF16), pool_w_ref[g].astype(BF16)) * pool_scale_ref[:, cols]
        pooled.append(y.astype(BF16))
    return pooled


def _pool_state(u_ext, tm):
    return u_ext[tm + HIST_ROWS - POOL_HIST:tm + HIST_ROWS, :]


def _strict_upper(n):
    r = lax.broadcasted_iota(jnp.int32, (n, n), 0)
    c = lax.broadcasted_iota(jnp.int32, (n, n), 1)
    return (r > c).astype(BF16)


def _causal_mask(n):
    r = lax.broadcasted_iota(jnp.int32, (n, n), 0)
    c = lax.broadcasted_iota(jnp.int32, (n, n), 1)
    return c < r


def _store_masked_queries(q, qm_ref):
    m = q.shape[0]
    lower_half = lax.broadcasted_iota(jnp.int32, (m, LANES), 1) < SB_HEAD_DIM
    for p in range(HEAD_PAIRS):
        q2 = q[:, p * LANES:(p + 1) * LANES]
        zero = jnp.zeros_like(q2)
        qm_ref[p, 0:m, :] = jnp.where(lower_half, q2, zero)
        qm_ref[p, m:2 * m, :] = jnp.where(lower_half, zero, q2)


def _sb_all_heads(qm_ref, kv_block, decay_ref, acc_ref, upper, mask, first, key_minor=False, extra_decay=None,
                  fillers=()):
    fillers = list(fillers)
    m = qm_ref.shape[1] // 2
    lower_half = lax.broadcasted_iota(jnp.int32, (m, LANES), 1) < SB_HEAD_DIM
    mask2 = None if mask is None else jnp.concatenate([mask, mask], axis=0)
    kv = [kv_block(p) for p in range(HEAD_PAIRS)]

    logit, sps, rowsums = [], [], []
    for p in range(HEAD_PAIRS):
        z = (_dot if key_minor else _dot_nt)(qm_ref[p], kv[p][0])
        sp = jnp.maximum(z, 0.0) + jnp.log(1.0 + jnp.exp2(-jnp.abs(z))) * LOG2E
        spm = sp if mask2 is None else jnp.where(mask2, sp, 0.0)
        sps.append(spm.astype(BF16))
        logit.append(z - sp)
        rowsums.append(jnp.sum(spm, axis=1, keepdims=True))
        if fillers:
            fillers.pop(0)()

    later = _dot(jnp.concatenate(sps, axis=0), upper)

    least = None
    for p in range(HEAD_PAIRS):
        a = jnp.exp2(logit[p] - later[2 * m * p:2 * m * (p + 1)])
        if mask2 is not None:
            a = jnp.where(mask2, a, 0.0)
        pv = (_dot_nt if key_minor else _dot)(a.astype(BF16), kv[p][1])
        merged = jnp.where(lower_half, pv[0:m], pv[m:2 * m])
        rowsum = jnp.where(lower_half, rowsums[p][0:m], rowsums[p][m:2 * m])
        if first:
            decay = rowsum
            acc_ref[p] = merged
        else:
            prev = decay_ref[p]
            if extra_decay is not None:
                prev = prev + extra_decay
            decay = prev + rowsum
            acc_ref[p] += merged * jnp.exp2(-prev)
        decay_ref[p] = decay
        least = decay if least is None else jnp.minimum(least, decay)
    return jnp.min(least)


def _more_blocks(state):
    j, least = state
    return jnp.logical_and(j >= 0, least < SKIP_LOG2)


def _store_heads(acc_ref, o_ref):
    for p in range(HEAD_PAIRS):
        o_ref[0, :, p * LANES:(p + 1) * LANES] = acc_ref[p].astype(BF16)


def _attn_scratch(m):
    return [pltpu.VMEM((HEAD_PAIRS, 2 * m, LANES), BF16), pltpu.VMEM((HEAD_PAIRS, m, LANES), F32),
            pltpu.VMEM((HEAD_PAIRS, m, LANES), F32)]


def _proj_attn_kernel(x_ref, hist_ref, w_in_f32_ref, pool_w_ref, pool_scale_ref, after_ref,
                      k_ref, v_ref, p_ref, pool_ref, a_ref,
                      w_in_ref, w_kvt_ref, u_ext, xb_ref, stage_q, stage_kvt, kvt_all, qm_ref, decay_ref, acc_ref,
                      *, tm, n_t, n_tiles):
    s = pl.program_id(0)

    @pl.when(s == 0)
    def _():
        w_in_ref[...] = w_in_f32_ref[...].astype(BF16)
        w_kvt_ref[...] = w_in_f32_ref[:, K_COLS:U_COLS].T.astype(BF16)
        stage_q[...] = jnp.zeros_like(stage_q)
        stage_kvt[...] = jnp.zeros_like(stage_kvt)
        u_ext[...] = jnp.zeros_like(u_ext)

    upper = _strict_upper(tm)
    qi = lax.rem(jnp.maximum(s - 1, 0), n_t)

    def kv_block(j):
        start = pl.multiple_of(j * tm, tm)
        return lambda p: (kvt_all[p * LANES:(p + 1) * LANES, pl.ds(start, tm)],
                          kvt_all[SB_WIDTH + p * LANES:SB_WIDTH + (p + 1) * LANES, pl.ds(start, tm)])

    def take_stage():
        kvt_all[:, pl.ds(pl.multiple_of(qi * tm, tm), tm)] = stage_kvt[...]
        _store_masked_queries(stage_q[...], qm_ref)

    def attend_near(fillers):
        _sb_all_heads(qm_ref, kv_block(qi), decay_ref, acc_ref, upper, _causal_mask(tm), True, key_minor=True,
                      fillers=fillers[0:HEAD_PAIRS])
        void = jnp.where(qi == 0, 2.0 * SKIP_LOG2, 0.0)
        return _sb_all_heads(qm_ref, kv_block(jnp.maximum(qi - 1, 0)), decay_ref, acc_ref, upper, None, False,
                             key_minor=True, extra_decay=void, fillers=fillers[HEAD_PAIRS:2 * HEAD_PAIRS])

    def attend_far(least):
        def body(state):
            j, _ = state
            return j - 1, _sb_all_heads(qm_ref, kv_block(j), decay_ref, acc_ref, upper, None, False, key_minor=True)

        lax.while_loop(_more_blocks, body, (qi - 2, least))
        _store_heads(acc_ref, a_ref)

    half = SB_WIDTH // 2

    def project_q(c):
        def piece():
            q = _proj_cols(xb_ref[...], w_in_ref, Q_COLS + c, half) * Q_SCALE
            stage_q[:, c:c + half] = q.astype(BF16)
        return piece

    def project_kv(out_ref, first_row, c):
        def piece():
            yt = _dot_nt(w_kvt_ref[first_row + c:first_row + c + half, :], xb_ref[...])
            out_ref[0, c:c + half, :] = yt
            stage_kvt[first_row + c:first_row + c + half, :] = yt.astype(BF16)
        return piece

    u_halves = []

    def project_u0():
        u_halves.append(_proj_cols(xb_ref[...], w_in_ref, U_COLS, half))

    def project_u1_pool():
        u_halves.append(_proj_cols(xb_ref[...], w_in_ref, U_COLS + half, half))
        pooled = _pool_mix(jnp.concatenate(u_halves, axis=1), hist_ref[0], pool_w_ref, pool_scale_ref, u_ext,
                           lax.rem(s, n_t), tm=tm, start_pos=0)
        for g, y in enumerate(pooled):
            p_ref[0, :, g * POOL_GC:(g + 1) * POOL_GC] = y
        pool_ref[0] = _pool_state(u_ext, tm)

    @pl.when(s < n_tiles)
    def _():
        take_stage()
        xb_ref[...] = x_ref[0].astype(BF16)
        attend_far(attend_near([
            project_q(0), project_q(half),
            project_kv(k_ref, 0, 0), project_kv(k_ref, 0, half),
            project_kv(v_ref, SB_WIDTH, 0), project_kv(v_ref, SB_WIDTH, half),
            project_u0, project_u1_pool]))

    @pl.when(s == n_tiles)
    def _():
        take_stage()
        attend_far(attend_near([]))


def _proj_attn_prompt(x, hist, w_in, pool_w, pool_scale2, after):
    B, T, _ = x.shape
    tm = ATTN_BLOCK
    assert T % tm == 0 and T >= POOL_HIST
    n_t = T // tm
    n_tiles = B * n_t
    kern = functools.partial(_proj_attn_kernel, tm=tm, n_t=n_t, n_tiles=n_tiles)
    proj = lambda s: jnp.minimum(s, n_tiles - 1)
    attn = lambda s: jnp.maximum(s - 1, 0)
    const = lambda *shape: pl.BlockSpec(shape, lambda s: (0,) * len(shape), pipeline_mode=pl.Buffered(1))
    proj_rows = lambda width: pl.BlockSpec((1, tm, width), lambda s: (proj(s) // n_t, proj(s) % n_t, 0))
    proj_cols = pl.BlockSpec((1, SB_WIDTH, tm), lambda s: (proj(s) // n_t, 0, proj(s) % n_t))
    return pl.pallas_call(
        kern,
        grid=(n_tiles + 1,),
        in_specs=[
            proj_rows(D_MODEL),
            pl.BlockSpec((1, HIST_ROWS, POOL_WIDTH), lambda s: (proj(s) // n_t, 0, 0)),
            const(D_MODEL, 3 * SB_WIDTH + POOL_WIDTH),
            const(len(POOL_WINDOWS), POOL_GC, POOL_GC),
            const(1, POOL_WIDTH),
            pl.BlockSpec(memory_space=pl.ANY),
        ],
        out_specs=[
            proj_cols,
            proj_cols,
            proj_rows(POOL_WIDTH),
            pl.BlockSpec((1, POOL_HIST, POOL_WIDTH), lambda s: (proj(s) // n_t, 0, 0)),
            pl.BlockSpec((1, tm, SB_WIDTH), lambda s: (attn(s) // n_t, attn(s) % n_t, 0)),
        ],
        out_shape=[
            jax.ShapeDtypeStruct((B, SB_WIDTH, T), F32),
            jax.ShapeDtypeStruct((B, SB_WIDTH, T), F32),
            jax.ShapeDtypeStruct((B, T, POOL_WIDTH), BF16),
            jax.ShapeDtypeStruct((B, POOL_HIST, POOL_WIDTH), F32),
            jax.ShapeDtypeStruct((B, T, SB_WIDTH), BF16),
        ],
        scratch_shapes=[
            pltpu.VMEM(w_in.shape, BF16),
            pltpu.VMEM((2 * SB_WIDTH, D_MODEL), BF16),
            pltpu.VMEM((HIST_ROWS + tm, POOL_WIDTH), F32),
            pltpu.VMEM((tm, D_MODEL), BF16),
            pltpu.VMEM((tm, SB_WIDTH), BF16),
            pltpu.VMEM((2 * SB_WIDTH, tm), BF16),
            pltpu.VMEM((2 * SB_WIDTH, T), BF16),
        ] + _attn_scratch(tm),
        compiler_params=pltpu.CompilerParams(
            dimension_semantics=("arbitrary",), vmem_limit_bytes=VMEM_LIMIT),
        name="proj_attn_prompt",
    )(x, hist, w_in, pool_w, pool_scale2, after)


def _proj_attn_sample_kernel(x_ref, hist_ref, w_in_f32_ref, pool_w_ref, pool_scale_ref,
                             ck_ref, cv_ref, ck_hbm, cv_hbm,
                             k_ref, v_ref, p_ref, pool_ref, a_ref,
                             w_in_ref, q_all, kv_new, u_all, u_ext, ck_buf, cv_buf, qm_ref, decay_ref, acc_ref,
                             *, t_new, past, blk):
    b = pl.program_id(0)
    rows = pl.ds(pl.multiple_of(b * t_new, t_new), t_new)

    @pl.when(b == 0)
    def _():
        w_in_ref[...] = w_in_f32_ref[...].astype(BF16)
        u_ext[...] = jnp.zeros_like(u_ext)
        x = x_ref[...].astype(BF16)
        q_all[...] = (_proj_cols(x, w_in_ref, Q_COLS, SB_WIDTH) * Q_SCALE).astype(BF16)
        k = _proj_cols(x, w_in_ref, K_COLS, SB_WIDTH)
        k_ref[...] = k
        kv_new[:, 0:SB_WIDTH] = k.astype(BF16)
        v = _proj_cols(x, w_in_ref, V_COLS, SB_WIDTH)
        v_ref[...] = v
        kv_new[:, SB_WIDTH:2 * SB_WIDTH] = v.astype(BF16)
        u_all[...] = _proj_cols(x, w_in_ref, U_COLS, POOL_WIDTH)

    pooled = _pool_mix(u_all[rows, :], hist_ref[0], pool_w_ref, pool_scale_ref, u_ext, 0, tm=t_new, start_pos=past)
    for g, y in enumerate(pooled):
        p_ref[rows, g * POOL_GC:(g + 1) * POOL_GC] = y
    pool_ref[b] = _pool_state(u_ext, t_new)

    _store_masked_queries(q_all[rows, :], qm_ref)
    new_kv = lambda p: (kv_new[rows, p * LANES:(p + 1) * LANES],
                        kv_new[rows, SB_WIDTH + p * LANES:SB_WIDTH + (p + 1) * LANES])
    _sb_all_heads(qm_ref, new_kv, decay_ref, acc_ref, _strict_upper(t_new), _causal_mask(t_new), True)
    upper = _strict_upper(blk)
    recent = lambda p: (ck_ref[0, p * LANES:(p + 1) * LANES, :].astype(BF16),
                        cv_ref[0, p * LANES:(p + 1) * LANES, :].astype(BF16))
    least = _sb_all_heads(qm_ref, recent, decay_ref, acc_ref, upper, None, False, key_minor=True)

    def body(state):
        j, _ = state
        start = pl.multiple_of(j * blk, blk)
        pltpu.sync_copy(ck_hbm.at[b, :, pl.ds(start, blk)], ck_buf)
        pltpu.sync_copy(cv_hbm.at[b, :, pl.ds(start, blk)], cv_buf)
        older = lambda p: (ck_buf[p * LANES:(p + 1) * LANES, :].astype(BF16),
                           cv_buf[p * LANES:(p + 1) * LANES, :].astype(BF16))
        return j - 1, _sb_all_heads(qm_ref, older, decay_ref, acc_ref, upper, None, False, key_minor=True)

    lax.while_loop(_more_blocks, body, (past // blk - 2, least))
    for p in range(HEAD_PAIRS):
        a_ref[rows, p * LANES:(p + 1) * LANES] = acc_ref[p].astype(BF16)


def _proj_attn_sample(x, hist, w_in, pool_w, pool_scale2, cache_kt, cache_vt):
    B, t_new, _ = x.shape
    past = cache_kt.shape[2]
    blk = ATTN_BLOCK
    assert past % blk == 0 and t_new >= HIST_ROWS
    n = B * t_new
    kern = functools.partial(_proj_attn_sample_kernel, t_new=t_new, past=past, blk=blk)
    const = lambda *shape: pl.BlockSpec(shape, lambda b: (0,) * len(shape), pipeline_mode=pl.Buffered(1))
    whole = lambda *shape: pl.BlockSpec(shape, lambda b: (0,) * len(shape))
    recent = pl.BlockSpec((1, SB_WIDTH, blk), lambda b: (b, 0, past // blk - 1))
    k, v, p, pool, a = pl.pallas_call(
        kern,
        grid=(B,),
        in_specs=[
            const(n, D_MODEL),
            pl.BlockSpec((1, HIST_ROWS, POOL_WIDTH), lambda b: (b, 0, 0)),
            const(D_MODEL, 3 * SB_WIDTH + POOL_WIDTH),
            const(len(POOL_WINDOWS), POOL_GC, POOL_GC),
            const(1, POOL_WIDTH),
            recent, recent,
            pl.BlockSpec(memory_space=pl.ANY), pl.BlockSpec(memory_space=pl.ANY),
        ],
        out_specs=[whole(n, SB_WIDTH), whole(n, SB_WIDTH), whole(n, POOL_WIDTH),
                   whole(B, POOL_HIST, POOL_WIDTH), whole(n, SB_WIDTH)],
        out_shape=[
            jax.ShapeDtypeStruct((n, SB_WIDTH), F32),
            jax.ShapeDtypeStruct((n, SB_WIDTH), F32),
            jax.ShapeDtypeStruct((n, POOL_WIDTH), BF16),
            jax.ShapeDtypeStruct((B, POOL_HIST, POOL_WIDTH), F32),
            jax.ShapeDtypeStruct((n, SB_WIDTH), BF16),
        ],
        scratch_shapes=[
            pltpu.VMEM(w_in.shape, BF16),
            pltpu.VMEM((n, SB_WIDTH), BF16),
            pltpu.VMEM((n, 2 * SB_WIDTH), BF16),
            pltpu.VMEM((n, POOL_WIDTH), F32),
            pltpu.VMEM((HIST_ROWS + t_new, POOL_WIDTH), F32),
            pltpu.VMEM((SB_WIDTH, blk), F32),
            pltpu.VMEM((SB_WIDTH, blk), F32),
        ] + _attn_scratch(t_new),
        compiler_params=pltpu.CompilerParams(
            dimension_semantics=("arbitrary",), vmem_limit_bytes=VMEM_LIMIT),
        name="proj_attn_sample",
    )(x.reshape(n, D_MODEL), hist, w_in, pool_w, pool_scale2, cache_kt, cache_vt, cache_kt, cache_vt)
    return k, v, p, pool, a


def _layer_norm(x, g, b):
    mu = jnp.mean(x, axis=-1, keepdims=True)
    xc = x - mu
    var = jnp.mean(xc * xc, axis=-1, keepdims=True)
    return xc * lax.rsqrt(var + LN_EPS) * g + b


def _ffn_rows(x_ref, a_ref, p_ref, y_ref, wo_ref, g1, b1, wg_ref, wu_ref, wd_ref, g2, b2, acc_ref):
    m = x_ref.shape[0]
    halves = tuple(slice(i * m // FFN_PARTS, (i + 1) * m // FFN_PARTS) for i in range(FFN_PARTS))
    chunks = [(sum(FF_CHUNKS[:i]), w) for i, w in enumerate(FF_CHUNKS)]

    def swiglu_chunk(x1b, chunk):
        c, w = chunk
        gate = _dot(x1b, wg_ref[:, c:c + w])
        up = _dot(x1b, wu_ref[:, c:c + w])
        hmid = (gate * jax.nn.sigmoid(gate) * up).astype(BF16)
        return _dot(hmid, wd_ref[c:c + w, :])

    mix = [_dot(a_ref[r, :], wo_ref[0:SB_WIDTH, :]) + _dot(p_ref[r, :], wo_ref[SB_WIDTH:, :]) for r in halves]
    x1, x1b = [], []
    for r, mx in zip(halves, mix):
        x1.append(_layer_norm(ALPHA * x_ref[r, :] + mx, g1, b1))
        x1b.append(x1[-1].astype(BF16))
        acc_ref[r, :] = swiglu_chunk(x1b[-1], chunks[0])
    x1b_all = jnp.concatenate(x1b, axis=0)
    for c in chunks[1:-1]:
        acc_ref[...] += swiglu_chunk(x1b_all, c)
    last = [swiglu_chunk(xb, chunks[-1]) for xb in x1b]
    for r, x1_r, part in zip(halves, x1, last):
        y_ref[r, :] = _layer_norm(ALPHA * x1_r + acc_ref[r, :] + part, g2, b2)


def _out_ffn_kernel(xp_ref, ap_ref, pp_ref, xs_ref, as_ref, ps_ref, wo_c, wg_c, wu_c, wd_c,
                    g1_ref, b1_ref, g2_ref, b2_ref, yp_ref, ys_ref,
                    wo_s, wg_s, wu_s, wd_s, accp_ref, accs_ref, *, n_conv, n_tiles):
    i = pl.program_id(0)

    @pl.when(i < n_conv)
    def _():
        for c_ref, s_ref in ((wo_c, wo_s), (wg_c, wg_s), (wu_c, wu_s), (wd_c, wd_s)):
            rows = c_ref.shape[0]
            s_ref[pl.ds(pl.multiple_of(i * rows, rows), rows), :] = c_ref[...].astype(BF16)

    norms = (g1_ref, b1_ref, g2_ref, b2_ref)

    def rows(x_ref, a_ref, p_ref, y_ref, acc_ref):
        g1, b1, g2, b2 = (r[...] for r in norms)
        _ffn_rows(x_ref, a_ref, p_ref, y_ref, wo_s, g1, b1, wg_s, wu_s, wd_s, g2, b2, acc_ref)

    @pl.when(jnp.logical_and(i >= n_conv, i < n_conv + n_tiles))
    def _():
        rows(xp_ref, ap_ref, pp_ref, yp_ref, accp_ref)

    @pl.when(i == n_conv + n_tiles)
    def _():
        rows(xs_ref, as_ref, ps_ref, ys_ref, accs_ref)


def _out_ffn(xp, ap, pp, xs, a_s, ps, w_out, w_gate, w_up, w_down, g1, b1, g2, b2, *, tm, n_conv):
    Np, Ns = xp.shape[0], xs.shape[0]
    assert Np % tm == 0 and sum(FF_CHUNKS) == D_FF
    n_tiles = Np // tm
    tile = lambda i: jnp.clip(i - n_conv, 0, n_tiles - 1)
    row = lambda width: pl.BlockSpec((tm, width), lambda i: (tile(i), 0))
    once = lambda *shape: pl.BlockSpec(shape, lambda i: (0,) * len(shape), pipeline_mode=pl.Buffered(1))

    def chunked(w):
        assert w.shape[0] % (n_conv * 16) == 0
        return pl.BlockSpec((w.shape[0] // n_conv, w.shape[1]), lambda i: (jnp.minimum(i, n_conv - 1), 0))

    kern = functools.partial(_out_ffn_kernel, n_conv=n_conv, n_tiles=n_tiles)
    return pl.pallas_call(
        kern,
        grid=(n_conv + n_tiles + 1,),
        in_specs=[
            row(D_MODEL), row(SB_WIDTH), row(POOL_WIDTH),
            once(Ns, D_MODEL), once(Ns, SB_WIDTH), once(Ns, POOL_WIDTH),
            chunked(w_out), chunked(w_gate), chunked(w_up), chunked(w_down),
            once(1, D_MODEL), once(1, D_MODEL), once(1, D_MODEL), once(1, D_MODEL),
        ],
        out_specs=[row(D_MODEL), pl.BlockSpec((Ns, D_MODEL), lambda i: (0, 0))],
        out_shape=[jax.ShapeDtypeStruct((Np, D_MODEL), F32), jax.ShapeDtypeStruct((Ns, D_MODEL), F32)],
        scratch_shapes=[pltpu.VMEM(w.shape, BF16) for w in (w_out, w_gate, w_up, w_down)]
        + [pltpu.VMEM((tm, D_MODEL), F32), pltpu.VMEM((Ns, D_MODEL), F32)],
        compiler_params=pltpu.CompilerParams(
            dimension_semantics=("arbitrary",), vmem_limit_bytes=VMEM_LIMIT),
        name="out_proj_ffn",
    )(xp, ap, pp, xs, a_s, ps, w_out, w_gate, w_up, w_down, g1, b1, g2, b2)


def kernel(x_prompt, x_sample, cache_k, cache_v, state_pool, w_in, pool_w, pool_scale, w_out,
           ln1_g, ln1_b, w_gate, w_up, w_down, ln2_g, ln2_b):
    Bp, Tp, _ = x_prompt.shape
    Bs, Ts, _ = x_sample.shape
    past = cache_k.shape[1]

    row2 = lambda a: a.reshape(1, -1).astype(F32)
    pool_scale2 = row2(pool_scale)

    hist_s = jnp.concatenate([jnp.zeros((Bs, HIST_ROWS - POOL_HIST, POOL_WIDTH), F32), state_pool.astype(F32)], axis=1)
    feature_major = lambda c: jnp.transpose(c, (0, 2, 3, 1)).reshape(Bs, SB_WIDTH, past)
    k_s, v_s, mix_s, pool_s, a_s = _proj_attn_sample(x_sample, hist_s, w_in, pool_w, pool_scale2,
                                                     feature_major(cache_k), feature_major(cache_v))

    hist_p = jnp.zeros((Bp, HIST_ROWS, POOL_WIDTH), F32)
    kt_p, vt_p, mix_p, pool_p, a_p = _proj_attn_prompt(x_prompt, hist_p, w_in, pool_w, pool_scale2, after=a_s)
    position_major = lambda c: jnp.transpose(c.reshape(Bp, SB_HEADS, SB_HEAD_DIM, Tp), (0, 3, 1, 2))

    y_p, y_s = _out_ffn(
        x_prompt.reshape(Bp * Tp, D_MODEL), a_p.reshape(Bp * Tp, SB_WIDTH), mix_p.reshape(Bp * Tp, POOL_WIDTH),
        x_sample.reshape(Bs * Ts, D_MODEL), a_s, mix_s,
        w_out, w_gate, w_up, w_down, row2(ln1_g), row2(ln1_b), row2(ln2_g), row2(ln2_b), tm=1024, n_conv=16)

    heads = lambda a: a.reshape(Bs, Ts, SB_HEADS, SB_HEAD_DIM)
    return (y_p.reshape(Bp, Tp, D_MODEL), y_s.reshape(Bs, Ts, D_MODEL),
            position_major(kt_p), position_major(vt_p), pool_p, heads(k_s), heads(v_s), pool_s)
```

```python
import functools

import jax
import jax.numpy as jnp
from jax import lax
from jax.experimental import pallas as pl
from jax.experimental.pallas import tpu as pltpu

D_MODEL = 1024
SB_WIDTH = 512
SB_HEADS = 8
SB_HEAD_DIM = 64
POOL_WIDTH = 512
POOL_WINDOWS = (2, 4, 8, 16)
POOL_GC = 128
POOL_HIST = 15
HIST_ROWS = 16
D_FF = 2816
DEPTH = 1
ALPHA = (2 * DEPTH) ** 0.25
LN_EPS = 1e-5
LANES = 128
HEAD_PAIRS = SB_WIDTH // LANES
LOG2E = 1.4426950408889634
Q_SCALE = SB_HEAD_DIM ** -0.5 * LOG2E
SKIP_LOG2 = 160.0
ATTN_BLOCK = 256
FF_CHUNKS = (256, 768, 768, 768, 256)
FFN_PARTS = 2
VMEM_LIMIT = 56 * 1024 * 1024

F32 = jnp.float32
BF16 = jnp.bfloat16
F32_MAX = 3.4028234663852886e38


def _dot(a, b):
    return jnp.dot(a, b, preferred_element_type=F32)


def _dot_nt(a, b):
    return lax.dot_general(a, b, (((1,), (1,)), ((), ())), preferred_element_type=F32)


Q_COLS, K_COLS, V_COLS, U_COLS = (SB_WIDTH * i for i in range(4))


def _proj_cols(x, w_in_ref, start, width):
    return _dot(x, w_in_ref[:, start:start + width])


def _pool_mix(u, hist, pool_w_ref, pool_scale_ref, u_ext, t, *, tm, start_pos):
    u_ext[0:HIST_ROWS, :] = jnp.where(t == 0, hist, u_ext[tm:tm + HIST_ROWS, :])
    u_ext[HIST_ROWS:HIST_ROWS + tm, :] = u

    pos = start_pos + t * tm + lax.broadcasted_iota(jnp.int32, (tm, 1), 0)
    pooled = []
    for g, w in enumerate(POOL_WINDOWS):
        cols = slice(g * POOL_GC, (g + 1) * POOL_GC)
        s = u_ext[:, cols]
        span = 1
        while span < w:
            s = s + pltpu.roll(s, span, 0)
            span *= 2
        cnt = jnp.minimum(w, pos + 1).astype(F32)
        d = s[HIST_ROWS:, :] / cnt - u[:, cols]
        y = _dot(d.astype(BF16), pool_w_ref[g].astype(BF16)) * pool_scale_ref[:, cols]
        pooled.append(y.astype(BF16))
    return pooled


def _pool_state(u_ext, tm):
    return u_ext[tm + HIST_ROWS - POOL_HIST:tm + HIST_ROWS, :]


def _strict_upper(n):
    r = lax.broadcasted_iota(jnp.int32, (n, n), 0)
    c = lax.broadcasted_iota(jnp.int32, (n, n), 1)
    return (r > c).astype(BF16)


def _causal_mask(n):
    r = lax.broadcasted_iota(jnp.int32, (n, n), 0)
    c = lax.broadcasted_iota(jnp.int32, (n, n), 1)
    return c < r


def _store_masked_queries(q, qm_ref):
    m = q.shape[0]
    lower_half = lax.broadcasted_iota(jnp.int32, (m, LANES), 1) < SB_HEAD_DIM
    for p in range(HEAD_PAIRS):
        q2 = q[:, p * LANES:(p + 1) * LANES]
        zero = jnp.zeros_like(q2)
        qm_ref[p, 0:m, :] = jnp.where(lower_half, q2, zero)
        qm_ref[p, m:2 * m, :] = jnp.where(lower_half, zero, q2)


def _sb_all_heads(qm_ref, kv_block, decay_ref, acc_ref, upper, mask, first, key_minor=False, extra_decay=None,
                  fillers=()):
    fillers = list(fillers)
    m = qm_ref.shape[1] // 2
    lower_half = lax.broadcasted_iota(jnp.int32, (m, LANES), 1) < SB_HEAD_DIM
    mask2 = None if mask is None else jnp.concatenate([mask, mask], axis=0)
    kv = [kv_block(p) for p in range(HEAD_PAIRS)]

    logit, sps, rowsums = [], [], []
    for p in range(HEAD_PAIRS):
        z = (_dot if key_minor else _dot_nt)(qm_ref[p], kv[p][0])
        sp = jnp.maximum(z, 0.0) + jnp.log(1.0 + jnp.exp2(-jnp.abs(z))) * LOG2E
        spm = sp if mask2 is None else jnp.where(mask2, sp, 0.0)
        sps.append(spm.astype(BF16))
        logit.append(z - sp)
        rowsums.append(jnp.sum(spm, axis=1, keepdims=True))
        if fillers:
            fillers.pop(0)()

    later = _dot(jnp.concatenate(sps, axis=0), upper)

    least = None
    for p in range(HEAD_PAIRS):
        a = jnp.exp2(logit[p] - later[2 * m * p:2 * m * (p + 1)])
        if mask2 is not None:
            a = jnp.where(mask2, a, 0.0)
        pv = (_dot_nt if key_minor else _dot)(a.astype(BF16), kv[p][1])
        merged = jnp.where(lower_half, pv[0:m], pv[m:2 * m])
        rowsum = jnp.where(lower_half, rowsums[p][0:m], rowsums[p][m:2 * m])
        if first:
            decay = rowsum
            acc_ref[p] = merged
        else:
            prev = decay_ref[p]
            if extra_decay is not None:
                prev = prev + extra_decay
            decay = prev + rowsum
            acc_ref[p] += merged * jnp.exp2(-prev)
        decay_ref[p] = decay
        least = decay if least is None else jnp.minimum(least, decay)
    return jnp.min(least)


def _more_blocks(state):
    j, least = state
    return jnp.logical_and(j >= 0, least < SKIP_LOG2)


def _store_heads(acc_ref, o_ref):
    for p in range(HEAD_PAIRS):
        o_ref[0, :, p * LANES:(p + 1) * LANES] = acc_ref[p].astype(BF16)


def _attn_scratch(m):
    return [pltpu.VMEM((HEAD_PAIRS, 2 * m, LANES), BF16), pltpu.VMEM((HEAD_PAIRS, m, LANES), F32),
            pltpu.VMEM((HEAD_PAIRS, m, LANES), F32)]


def _proj_attn_kernel(x_ref, hist_ref, w_in_f32_ref, pool_w_ref, pool_scale_ref, after_ref,
                      k_ref, v_ref, p_ref, pool_ref, a_ref,
                      w_in_ref, w_kvt_ref, u_ext, xb_ref, stage_q, stage_kvt, kvt_all, qm_ref, decay_ref, acc_ref,
                      *, tm, n_t, n_tiles):
    s = pl.program_id(0)

    @pl.when(s == 0)
    def _():
        w_in_ref[...] = w_in_f32_ref[...].astype(BF16)
        w_kvt_ref[...] = w_in_f32_ref[:, K_COLS:U_COLS].T.astype(BF16)
        stage_q[...] = jnp.zeros_like(stage_q)
        stage_kvt[...] = jnp.zeros_like(stage_kvt)
        u_ext[...] = jnp.zeros_like(u_ext)

    upper = _strict_upper(tm)
    qi = lax.rem(jnp.maximum(s - 1, 0), n_t)

    def kv_block(j):
        start = pl.multiple_of(j * tm, tm)
        return lambda p: (kvt_all[p * LANES:(p + 1) * LANES, pl.ds(start, tm)],
                          kvt_all[SB_WIDTH + p * LANES:SB_WIDTH + (p + 1) * LANES, pl.ds(start, tm)])

    def take_stage():
        kvt_all[:, pl.ds(pl.multiple_of(qi * tm, tm), tm)] = stage_kvt[...]
        _store_masked_queries(stage_q[...], qm_ref)

    def attend_near(fillers):
        _sb_all_heads(qm_ref, kv_block(qi), decay_ref, acc_ref, upper, _causal_mask(tm), True, key_minor=True,
                      fillers=fillers[0:HEAD_PAIRS])
        void = jnp.where(qi == 0, 2.0 * SKIP_LOG2, 0.0)
        return _sb_all_heads(qm_ref, kv_block(jnp.maximum(qi - 1, 0)), decay_ref, acc_ref, upper, None, False,
                             key_minor=True, extra_decay=void, fillers=fillers[HEAD_PAIRS:2 * HEAD_PAIRS])

    def attend_far(least):
        def body(state):
            j, _ = state
            return j - 1, _sb_all_heads(qm_ref, kv_block(j), decay_ref, acc_ref, upper, None, False, key_minor=True)

        lax.while_loop(_more_blocks, body, (qi - 2, least))
        _store_heads(acc_ref, a_ref)

    half = SB_WIDTH // 2

    def project_q(c):
        def piece():
            q = _proj_cols(xb_ref[...], w_in_ref, Q_COLS + c, half) * Q_SCALE
            stage_q[:, c:c + half] = q.astype(BF16)
        return piece

    def project_kv(out_ref, first_row, c):
        def piece():
            yt = _dot_nt(w_kvt_ref[first_row + c:first_row + c + half, :], xb_ref[...])
            out_ref[0, c:c + half, :] = yt
            stage_kvt[first_row + c:first_row + c + half, :] = yt.astype(BF16)
        return piece

    u_halves = []

    def project_u0():
        u_halves.append(_proj_cols(xb_ref[...], w_in_ref, U_COLS, half))

    def project_u1_pool():
        u_halves.append(_proj_cols(xb_ref[...], w_in_ref, U_COLS + half, half))
        pooled = _pool_mix(jnp.concatenate(u_halves, axis=1), hist_ref[0], pool_w_ref, pool_scale_ref, u_ext,
                           lax.rem(s, n_t), tm=tm, start_pos=0)
        for g, y in enumerate(pooled):
            p_ref[0, :, g * POOL_GC:(g + 1) * POOL_GC] = y
        pool_ref[0] = _pool_state(u_ext, tm)

    @pl.when(s < n_tiles)
    def _():
        take_stage()
        xb_ref[...] = x_ref[0].astype(BF16)
        attend_far(attend_near([
            project_q(0), project_q(half),
            project_kv(k_ref, 0, 0), project_kv(k_ref, 0, half),
            project_kv(v_ref, SB_WIDTH, 0), project_kv(v_ref, SB_WIDTH, half),
            project_u0, project_u1_pool]))

    @pl.when(s == n_tiles)
    def _():
        take_stage()
        attend_far(attend_near([]))


def _proj_attn_prompt(x, hist, w_in, pool_w, pool_scale2, after):
    B, T, _ = x.shape
    tm = ATTN_BLOCK
    assert T % tm == 0 and T >= POOL_HIST
    n_t = T // tm
    n_tiles = B * n_t
    kern = functools.partial(_proj_attn_kernel, tm=tm, n_t=n_t, n_tiles=n_tiles)
    proj = lambda s: jnp.minimum(s, n_tiles - 1)
    attn = lambda s: jnp.maximum(s - 1, 0)
    const = lambda *shape: pl.BlockSpec(shape, lambda s: (0,) * len(shape), pipeline_mode=pl.Buffered(1))
    proj_rows = lambda width: pl.BlockSpec((1, tm, width), lambda s: (proj(s) // n_t, proj(s) % n_t, 0))
    proj_cols = pl.BlockSpec((1, SB_WIDTH, tm), lambda s: (proj(s) // n_t, 0, proj(s) % n_t))
    return pl.pallas_call(
        kern,
        grid=(n_tiles + 1,),
        in_specs=[
            proj_rows(D_MODEL),
            pl.BlockSpec((1, HIST_ROWS, POOL_WIDTH), lambda s: (proj(s) // n_t, 0, 0)),
            const(D_MODEL, 3 * SB_WIDTH + POOL_WIDTH),
            const(len(POOL_WINDOWS), POOL_GC, POOL_GC),
            const(1, POOL_WIDTH),
            pl.BlockSpec(memory_space=pl.ANY),
        ],
        out_specs=[
            proj_cols,
            proj_cols,
            proj_rows(POOL_WIDTH),
            pl.BlockSpec((1, POOL_HIST, POOL_WIDTH), lambda s: (proj(s) // n_t, 0, 0)),
            pl.BlockSpec((1, tm, SB_WIDTH), lambda s: (attn(s) // n_t, attn(s) % n_t, 0)),
        ],
        out_shape=[
            jax.ShapeDtypeStruct((B, SB_WIDTH, T), F32),
            jax.ShapeDtypeStruct((B, SB_WIDTH, T), F32),
            jax.ShapeDtypeStruct((B, T, POOL_WIDTH), BF16),
            jax.ShapeDtypeStruct((B, POOL_HIST, POOL_WIDTH), F32),
            jax.ShapeDtypeStruct((B, T, SB_WIDTH), BF16),
        ],
        scratch_shapes=[
            pltpu.VMEM(w_in.shape, BF16),
            pltpu.VMEM((2 * SB_WIDTH, D_MODEL), BF16),
            pltpu.VMEM((HIST_ROWS + tm, POOL_WIDTH), F32),
            pltpu.VMEM((tm, D_MODEL), BF16),
            pltpu.VMEM((tm, SB_WIDTH), BF16),
            pltpu.VMEM((2 * SB_WIDTH, tm), BF16),
            pltpu.VMEM((2 * SB_WIDTH, T), BF16),
        ] + _attn_scratch(tm),
        compiler_params=pltpu.CompilerParams(
            dimension_semantics=("arbitrary",), vmem_limit_bytes=VMEM_LIMIT),
        name="proj_attn_prompt",
    )(x, hist, w_in, pool_w, pool_scale2, after)


def _proj_attn_sample_kernel(x_ref, hist_ref, w_in_f32_ref, pool_w_ref, pool_scale_ref,
                             ck_ref, cv_ref, ck_hbm, cv_hbm,
                             k_ref, v_ref, p_ref, pool_ref, a_ref,
                             w_in_ref, q_all, kv_new, u_all, u_ext, ck_buf, cv_buf, qm_ref, decay_ref, acc_ref,
                             *, t_new, past, blk):
    b = pl.program_id(0)
    rows = pl.ds(pl.multiple_of(b * t_new, t_new), t_new)

    @pl.when(b == 0)
    def _():
        w_in_ref[...] = w_in_f32_ref[...].astype(BF16)
        u_ext[...] = jnp.zeros_like(u_ext)
        x = x_ref[...].astype(BF16)
        q_all[...] = (_proj_cols(x, w_in_ref, Q_COLS, SB_WIDTH) * Q_SCALE).astype(BF16)
        k = _proj_cols(x, w_in_ref, K_COLS, SB_WIDTH)
        k_ref[...] = k
        kv_new[:, 0:SB_WIDTH] = k.astype(BF16)
        v = _proj_cols(x, w_in_ref, V_COLS, SB_WIDTH)
        v_ref[...] = v
        kv_new[:, SB_WIDTH:2 * SB_WIDTH] = v.astype(BF16)
        u_all[...] = _proj_cols(x, w_in_ref, U_COLS, POOL_WIDTH)

    pooled = _pool_mix(u_all[rows, :], hist_ref[0], pool_w_ref, pool_scale_ref, u_ext, 0, tm=t_new, start_pos=past)
    for g, y in enumerate(pooled):
        p_ref[rows, g * POOL_GC:(g + 1) * POOL_GC] = y
    pool_ref[b] = _pool_state(u_ext, t_new)

    _store_masked_queries(q_all[rows, :], qm_ref)
    new_kv = lambda p: (kv_new[rows, p * LANES:(p + 1) * LANES],
                        kv_new[rows, SB_WIDTH + p * LANES:SB_WIDTH + (p + 1) * LANES])
    _sb_all_heads(qm_ref, new_kv, decay_ref, acc_ref, _strict_upper(t_new), _causal_mask(t_new), True)
    upper = _strict_upper(blk)
    recent = lambda p: (ck_ref[0, p * LANES:(p + 1) * LANES, :].astype(BF16),
                        cv_ref[0, p * LANES:(p + 1) * LANES, :].astype(BF16))
    least = _sb_all_heads(qm_ref, recent, decay_ref, acc_ref, upper, None, False, key_minor=True)

    def body(state):
        j, _ = state
        start = pl.multiple_of(j * blk, blk)
        pltpu.sync_copy(ck_hbm.at[b, :, pl.ds(start, blk)], ck_buf)
        pltpu.sync_copy(cv_hbm.at[b, :, pl.ds(start, blk)], cv_buf)
        older = lambda p: (ck_buf[p * LANES:(p + 1) * LANES, :].astype(BF16),
                           cv_buf[p * LANES:(p + 1) * LANES, :].astype(BF16))
        return j - 1, _sb_all_heads(qm_ref, older, decay_ref, acc_ref, upper, None, False, key_minor=True)

    lax.while_loop(_more_blocks, body, (past // blk - 2, least))
    for p in range(HEAD_PAIRS):
        a_ref[rows, p * LANES:(p + 1) * LANES] = acc_ref[p].astype(BF16)


def _proj_attn_sample(x, hist, w_in, pool_w, pool_scale2, cache_kt, cache_vt):
    B, t_new, _ = x.shape
    past = cache_kt.shape[2]
    blk = ATTN_BLOCK
    assert past % blk == 0 and t_new >= HIST_ROWS
    n = B * t_new
    kern = functools.partial(_proj_attn_sample_kernel, t_new=t_new, past=past, blk=blk)
    const = lambda *shape: pl.BlockSpec(shape, lambda b: (0,) * len(shape), pipeline_mode=pl.Buffered(1))
    whole = lambda *shape: pl.BlockSpec(shape, lambda b: (0,) * len(shape))
    recent = pl.BlockSpec((1, SB_WIDTH, blk), lambda b: (b, 0, past // blk - 1))
    k, v, p, pool, a = pl.pallas_call(
        kern,
        grid=(B,),
        in_specs=[
            const(n, D_MODEL),
            pl.BlockSpec((1, HIST_ROWS, POOL_WIDTH), lambda b: (b, 0, 0)),
            const(D_MODEL, 3 * SB_WIDTH + POOL_WIDTH),
            const(len(POOL_WINDOWS), POOL_GC, POOL_GC),
            const(1, POOL_WIDTH),
            recent, recent,
            pl.BlockSpec(memory_space=pl.ANY), pl.BlockSpec(memory_space=pl.ANY),
        ],
        out_specs=[whole(n, SB_WIDTH), whole(n, SB_WIDTH), whole(n, POOL_WIDTH),
                   whole(B, POOL_HIST, POOL_WIDTH), whole(n, SB_WIDTH)],
        out_shape=[
            jax.ShapeDtypeStruct((n, SB_WIDTH), F32),
            jax.ShapeDtypeStruct((n, SB_WIDTH), F32),
            jax.ShapeDtypeStruct((n, POOL_WIDTH), BF16),
            jax.ShapeDtypeStruct((B, POOL_HIST, POOL_WIDTH), F32),
            jax.ShapeDtypeStruct((n, SB_WIDTH), BF16),
        ],
        scratch_shapes=[
            pltpu.VMEM(w_in.shape, BF16),
            pltpu.VMEM((n, SB_WIDTH), BF16),
            pltpu.VMEM((n, 2 * SB_WIDTH), BF16),
            pltpu.VMEM((n, POOL_WIDTH), F32),
            pltpu.VMEM((HIST_ROWS + t_new, POOL_WIDTH), F32),
            pltpu.VMEM((SB_WIDTH, blk), F32),
            pltpu.VMEM((SB_WIDTH, blk), F32),
        ] + _attn_scratch(t_new),
        compiler_params=pltpu.CompilerParams(
            dimension_semantics=("arbitrary",), vmem_limit_bytes=VMEM_LIMIT),
        name="proj_attn_sample",
    )(x.reshape(n, D_MODEL), hist, w_in, pool_w, pool_scale2, cache_kt, cache_vt, cache_kt, cache_vt)
    return k, v, p, pool, a


def _layer_norm(x, g, b):
    mu = jnp.mean(x, axis=-1, keepdims=True)
    xc = x - mu
    var = jnp.mean(xc * xc, axis=-1, keepdims=True)
    return xc * lax.rsqrt(var + LN_EPS) * g + b


def _ffn_rows(x_ref, a_ref, p_ref, wo_ref, g1, b1, wg_ref, wu_ref, wd_ref, acc_ref, x1b_ref, filler):
    m = x_ref.shape[0]
    halves = tuple(slice(i * m // FFN_PARTS, (i + 1) * m // FFN_PARTS) for i in range(FFN_PARTS))
    chunks = [(sum(FF_CHUNKS[:i]), w) for i, w in enumerate(FF_CHUNKS)]

    def swiglu_chunk(x1b, chunk):
        c, w = chunk
        gate = _dot(x1b, wg_ref[:, c:c + w])
        up = _dot(x1b, wu_ref[:, c:c + w])
        hmid = (gate * jax.nn.sigmoid(gate) * up).astype(BF16)
        return _dot(hmid, wd_ref[c:c + w, :])

    mix = [_dot(a_ref[r, :], wo_ref[0:SB_WIDTH, :]) + _dot(p_ref[r, :], wo_ref[SB_WIDTH:, :]) for r in halves]
    for r, mx in zip(halves, mix):
        x1 = _layer_norm(ALPHA * x_ref[r, :] + mx, g1, b1)
        x1b_ref[r, :] = x1.astype(BF16)
        acc_ref[r, :] = ALPHA * x1 + swiglu_chunk(x1b_ref[r, :], chunks[0])
    total = None
    for j, c in enumerate(chunks[1:]):
        anchor = filler(j, len(chunks) - 1)
        x1b_ref[0:anchor.shape[0], 0:anchor.shape[1]] += anchor.astype(BF16)
        total = acc_ref[...] + swiglu_chunk(x1b_ref[...], c)
        if c is not chunks[-1]:
            acc_ref[...] = total
    return total


def _out_ffn_kernel(xp_ref, ap_ref, pp_ref, xs_ref, as_ref, ps_ref, wo_c, wg_c, wu_c, wd_c,
                    g1_ref, b1_ref, g2_ref, b2_ref, yp_ref, ys_ref,
                    wo_s, wg_s, wu_s, wd_s, accp_ref, accs_ref, pre_ref, x1b_ref, *, n_conv, n_tiles):
    i = pl.program_id(0)

    @pl.when(i == 0)
    def _():
        pre_ref[...] = jnp.zeros_like(pre_ref)

    @pl.when(i < n_conv)
    def _():
        for c_ref, s_ref in ((wo_c, wo_s), (wg_c, wg_s), (wu_c, wu_s), (wd_c, wd_s)):
            rows = c_ref.shape[0]
            s_ref[pl.ds(pl.multiple_of(i * rows, rows), rows), :] = c_ref[...].astype(BF16)

    def finish_previous_tile(j, n):
        rows = yp_ref.shape[0] // n
        r = slice(j * rows, (j + 1) * rows)
        y = _layer_norm(pre_ref[r, :], g2_ref[...], b2_ref[...])
        yp_ref[r, :] = y
        top = jnp.max(y.reshape(rows // 16, 16, D_MODEL), axis=0)
        top = functools.reduce(jnp.maximum, [top[:, c:c + LANES] for c in range(0, D_MODEL, LANES)])
        return jnp.where(top > F32_MAX, top, 0.0)

    def rows(x_ref, a_ref, p_ref, acc_ref):
        m = x_ref.shape[0]
        return _ffn_rows(x_ref, a_ref, p_ref, wo_s, g1_ref[...], b1_ref[...], wg_s, wu_s, wd_s, acc_ref,
                         x1b_ref.at[0:m, :], finish_previous_tile)

    @pl.when(jnp.logical_and(i >= n_conv, i < n_conv + n_tiles))
    def _():
        pre_ref[...] = rows(xp_ref, ap_ref, pp_ref, accp_ref)

    @pl.when(i == n_conv + n_tiles)
    def _():
        ys_ref[...] = _layer_norm(rows(xs_ref, as_ref, ps_ref, accs_ref), g2_ref[...], b2_ref[...])


def _out_ffn(xp, ap, pp, xs, a_s, ps, w_out, w_gate, w_up, w_down, g1, b1, g2, b2, *, tm, n_conv):
    Np, Ns = xp.shape[0], xs.shape[0]
    assert Np % tm == 0 and sum(FF_CHUNKS) == D_FF
    n_tiles = Np // tm
    tile = lambda i: jnp.clip(i - n_conv, 0, n_tiles - 1)
    row = lambda width: pl.BlockSpec((tm, width), lambda i: (tile(i), 0))
    once = lambda *shape: pl.BlockSpec(shape, lambda i: (0,) * len(shape), pipeline_mode=pl.Buffered(1))

    def chunked(w):
        assert w.shape[0] % (n_conv * 16) == 0
        return pl.BlockSpec((w.shape[0] // n_conv, w.shape[1]), lambda i: (jnp.minimum(i, n_conv - 1), 0))

    kern = functools.partial(_out_ffn_kernel, n_conv=n_conv, n_tiles=n_tiles)
    return pl.pallas_call(
        kern,
        grid=(n_conv + n_tiles + 1,),
        in_specs=[
            row(D_MODEL), row(SB_WIDTH), row(POOL_WIDTH),
            once(Ns, D_MODEL), once(Ns, SB_WIDTH), once(Ns, POOL_WIDTH),
            chunked(w_out), chunked(w_gate), chunked(w_up), chunked(w_down),
            once(1, D_MODEL), once(1, D_MODEL), once(1, D_MODEL), once(1, D_MODEL),
        ],
        out_specs=[pl.BlockSpec((tm, D_MODEL), lambda i: (tile(i - 1), 0)),
                   pl.BlockSpec((Ns, D_MODEL), lambda i: (0, 0))],
        out_shape=[jax.ShapeDtypeStruct((Np, D_MODEL), F32), jax.ShapeDtypeStruct((Ns, D_MODEL), F32)],
        scratch_shapes=[pltpu.VMEM(w.shape, BF16) for w in (w_out, w_gate, w_up, w_down)]
        + [pltpu.VMEM((tm, D_MODEL), F32), pltpu.VMEM((Ns, D_MODEL), F32), pltpu.VMEM((tm, D_MODEL), F32),
           pltpu.VMEM((tm, D_MODEL), BF16)],
        compiler_params=pltpu.CompilerParams(
            dimension_semantics=("arbitrary",), vmem_limit_bytes=VMEM_LIMIT),
        name="out_proj_ffn",
    )(xp, ap, pp, xs, a_s, ps, w_out, w_gate, w_up, w_down, g1, b1, g2, b2)


def kernel(x_prompt, x_sample, cache_k, cache_v, state_pool, w_in, pool_w, pool_scale, w_out,
           ln1_g, ln1_b, w_gate, w_up, w_down, ln2_g, ln2_b):
    Bp, Tp, _ = x_prompt.shape
    Bs, Ts, _ = x_sample.shape
    past = cache_k.shape[1]

    row2 = lambda a: a.reshape(1, -1).astype(F32)
    pool_scale2 = row2(pool_scale)

    hist_s = jnp.concatenate([jnp.zeros((Bs, HIST_ROWS - POOL_HIST, POOL_WIDTH), F32), state_pool.astype(F32)], axis=1)
    feature_major = lambda c: jnp.transpose(c, (0, 2, 3, 1)).reshape(Bs, SB_WIDTH, past)
    k_s, v_s, mix_s, pool_s, a_s = _proj_attn_sample(x_sample, hist_s, w_in, pool_w, pool_scale2,
                                                     feature_major(cache_k), feature_major(cache_v))

    hist_p = jnp.zeros((Bp, HIST_ROWS, POOL_WIDTH), F32)
    kt_p, vt_p, mix_p, pool_p, a_p = _proj_attn_prompt(x_prompt, hist_p, w_in, pool_w, pool_scale2, after=a_s)
    position_major = lambda c: jnp.transpose(c.reshape(Bp, SB_HEADS, SB_HEAD_DIM, Tp), (0, 3, 1, 2))

    y_p, y_s = _out_ffn(
        x_prompt.reshape(Bp * Tp, D_MODEL), a_p.reshape(Bp * Tp, SB_WIDTH), mix_p.reshape(Bp * Tp, POOL_WIDTH),
        x_sample.reshape(Bs * Ts, D_MODEL), a_s, mix_s,
        w_out, w_gate, w_up, w_down, row2(ln1_g), row2(ln1_b), row2(ln2_g), row2(ln2_b), tm=512, n_conv=16)

    heads = lambda a: a.reshape(Bs, Ts, SB_HEADS, SB_HEAD_DIM)
    return (y_p.reshape(Bp, Tp, D_MODEL), y_s.reshape(Bs, Ts, D_MODEL),
            position_major(kt_p), position_major(vt_p), pool_p, heads(k_s), heads(v_s), pool_s)
```

```python
import functools

import jax
import jax.numpy as jnp
from jax import lax
from jax.experimental import pallas as pl
from jax.experimental.pallas import tpu as pltpu

D_MODEL = 1024
SB_WIDTH = 512
SB_HEADS = 8
SB_HEAD_DIM = 64
POOL_WIDTH = 512
POOL_WINDOWS = (2, 4, 8, 16)
POOL_GC = 128
POOL_HIST = 15
HIST_ROWS = 16
D_FF = 2816
DEPTH = 1
ALPHA = (2 * DEPTH) ** 0.25
LN_EPS = 1e-5
LANES = 128
HEAD_PAIRS = SB_WIDTH // LANES
LOG2E = 1.4426950408889634
Q_SCALE = SB_HEAD_DIM ** -0.5 * LOG2E
SKIP_LOG2 = 160.0
ATTN_BLOCK = 256
FF_CHUNKS = (256, 768, 768, 768, 256)
FFN_PARTS = 2
VMEM_LIMIT = 56 * 1024 * 1024

F32 = jnp.float32
BF16 = jnp.bfloat16
F32_MAX = 3.4028234663852886e38


def _dot(a, b):
    return jnp.dot(a, b, preferred_element_type=F32)


def _dot_nt(a, b):
    return lax.dot_general(a, b, (((1,), (1,)), ((), ())), preferred_element_type=F32)


Q_COLS, K_COLS, V_COLS, U_COLS = (SB_WIDTH * i for i in range(4))


def _proj_cols(x, w_in_ref, start, width):
    return _dot(x, w_in_ref[:, start:start + width])


def _pool_mix(u, hist, pool_w_ref, pool_scale_ref, u_ext, t, *, tm, start_pos):
    u_ext[0:HIST_ROWS, :] = jnp.where(t == 0, hist, u_ext[tm:tm + HIST_ROWS, :])
    u_ext[HIST_ROWS:HIST_ROWS + tm, :] = u

    pos = start_pos + t * tm + lax.broadcasted_iota(jnp.int32, (tm, 1), 0)
    pooled = []
    for g, w in enumerate(POOL_WINDOWS):
        cols = slice(g * POOL_GC, (g + 1) * POOL_GC)
        s = u_ext[:, cols]
        span = 1
        while span < w:
            s = s + pltpu.roll(s, span, 0)
            span *= 2
        cnt = jnp.minimum(w, pos + 1).astype(F32)
        d = s[HIST_ROWS:, :] / cnt - u[:, cols]
        y = _dot(d.astype(BF16), pool_w_ref[g].astype(BF16)) * pool_scale_ref[:, cols]
        pooled.append(y.astype(BF16))
    return pooled


def _pool_state(u_ext, tm):
    return u_ext[tm + HIST_ROWS - POOL_HIST:tm + HIST_ROWS, :]


def _strict_upper(n):
    r = lax.broadcasted_iota(jnp.int32, (n, n), 0)
    c = lax.broadcasted_iota(jnp.int32, (n, n), 1)
    return (r > c).astype(BF16)


def _causal_mask(n):
    r = lax.broadcasted_iota(jnp.int32, (n, n), 0)
    c = lax.broadcasted_iota(jnp.int32, (n, n), 1)
    return c < r


def _store_masked_queries(q, qm_ref):
    m = q.shape[0]
    lower_half = lax.broadcasted_iota(jnp.int32, (m, LANES), 1) < SB_HEAD_DIM
    for p in range(HEAD_PAIRS):
        q2 = q[:, p * LANES:(p + 1) * LANES]
        zero = jnp.zeros_like(q2)
        qm_ref[p, 0:m, :] = jnp.where(lower_half, q2, zero)
        qm_ref[p, m:2 * m, :] = jnp.where(lower_half, zero, q2)


def _sb_all_heads(qm_ref, kv_block, decay_ref, acc_ref, upper, mask, first, key_minor=False, extra_decay=None,
                  fillers=()):
    fillers = list(fillers)
    m = qm_ref.shape[1] // 2
    lower_half = lax.broadcasted_iota(jnp.int32, (m, LANES), 1) < SB_HEAD_DIM
    mask2 = None if mask is None else jnp.concatenate([mask, mask], axis=0)
    kv = [kv_block(p) for p in range(HEAD_PAIRS)]

    logit, sps, rowsums = [], [], []
    for p in range(HEAD_PAIRS):
        z = (_dot if key_minor else _dot_nt)(qm_ref[p], kv[p][0])
        sp = jnp.maximum(z, 0.0) + jnp.log(1.0 + jnp.exp2(-jnp.abs(z))) * LOG2E
        spm = sp if mask2 is None else jnp.where(mask2, sp, 0.0)
        sps.append(spm.astype(BF16))
        logit.append(z - sp)
        rowsums.append(jnp.sum(spm, axis=1, keepdims=True))
        if fillers:
            fillers.pop(0)()

    later = _dot(jnp.concatenate(sps, axis=0), upper)

    least = None
    for p in range(HEAD_PAIRS):
        a = jnp.exp2(logit[p] - later[2 * m * p:2 * m * (p + 1)])
        if mask2 is not None:
            a = jnp.where(mask2, a, 0.0)
        pv = (_dot_nt if key_minor else _dot)(a.astype(BF16), kv[p][1])
        merged = jnp.where(lower_half, pv[0:m], pv[m:2 * m])
        rowsum = jnp.where(lower_half, rowsums[p][0:m], rowsums[p][m:2 * m])
        if first:
            decay = rowsum
            acc_ref[p] = merged
        else:
            prev = decay_ref[p]
            if extra_decay is not None:
                prev = prev + extra_decay
            decay = prev + rowsum
            acc_ref[p] += merged * jnp.exp2(-prev)
        decay_ref[p] = decay
        least = decay if least is None else jnp.minimum(least, decay)
    return jnp.min(least)


def _more_blocks(state):
    j, least = state
    return jnp.logical_and(j >= 0, least < SKIP_LOG2)


def _store_heads(acc_ref, o_ref):
    for p in range(HEAD_PAIRS):
        o_ref[0, :, p * LANES:(p + 1) * LANES] = acc_ref[p].astype(BF16)


def _attn_scratch(m):
    return [pltpu.VMEM((HEAD_PAIRS, 2 * m, LANES), BF16), pltpu.VMEM((HEAD_PAIRS, m, LANES), F32),
            pltpu.VMEM((HEAD_PAIRS, m, LANES), F32)]


def _proj_attn_kernel(x_ref, hist_ref, w_in_f32_ref, pool_w_ref, pool_scale_ref, after_ref,
                      wo_c, wg_c, wu_c, wd_c,
                      k_ref, v_ref, p_ref, pool_ref, a_ref, wo_b, wg_b, wu_b, wd_b,
                      w_in_ref, w_kvt_ref, u_ext, xb_ref, stage_q, stage_kvt, kvt_all, qm_ref, decay_ref, acc_ref,
                      *, tm, n_t, n_tiles, n_conv):
    s = pl.program_id(0)

    @pl.when(s < n_conv)
    def _():
        for c_ref, b_ref in ((wo_c, wo_b), (wg_c, wg_b), (wu_c, wu_b), (wd_c, wd_b)):
            b_ref[...] = c_ref[...].astype(BF16)

    @pl.when(s == 0)
    def _():
        w_in_ref[...] = w_in_f32_ref[...].astype(BF16)
        w_kvt_ref[...] = w_in_f32_ref[:, K_COLS:U_COLS].T.astype(BF16)
        stage_q[...] = jnp.zeros_like(stage_q)
        stage_kvt[...] = jnp.zeros_like(stage_kvt)
        u_ext[...] = jnp.zeros_like(u_ext)

    upper = _strict_upper(tm)
    qi = lax.rem(jnp.maximum(s - 1, 0), n_t)

    def kv_block(j):
        start = pl.multiple_of(j * tm, tm)
        return lambda p: (kvt_all[p * LANES:(p + 1) * LANES, pl.ds(start, tm)],
                          kvt_all[SB_WIDTH + p * LANES:SB_WIDTH + (p + 1) * LANES, pl.ds(start, tm)])

    def take_stage():
        kvt_all[:, pl.ds(pl.multiple_of(qi * tm, tm), tm)] = stage_kvt[...]
        _store_masked_queries(stage_q[...], qm_ref)

    def attend_near(fillers):
        _sb_all_heads(qm_ref, kv_block(qi), decay_ref, acc_ref, upper, _causal_mask(tm), True, key_minor=True,
                      fillers=fillers[0:HEAD_PAIRS])
        void = jnp.where(qi == 0, 2.0 * SKIP_LOG2, 0.0)
        return _sb_all_heads(qm_ref, kv_block(jnp.maximum(qi - 1, 0)), decay_ref, acc_ref, upper, None, False,
                             key_minor=True, extra_decay=void, fillers=fillers[HEAD_PAIRS:2 * HEAD_PAIRS])

    def attend_far(least):
        def body(state):
            j, _ = state
            return j - 1, _sb_all_heads(qm_ref, kv_block(j), decay_ref, acc_ref, upper, None, False, key_minor=True)

        lax.while_loop(_more_blocks, body, (qi - 2, least))
        _store_heads(acc_ref, a_ref)

    half = SB_WIDTH // 2

    def project_q(c):
        def piece():
            q = _proj_cols(xb_ref[...], w_in_ref, Q_COLS + c, half) * Q_SCALE
            stage_q[:, c:c + half] = q.astype(BF16)
        return piece

    def project_kv(out_ref, first_row, c):
        def piece():
            yt = _dot_nt(w_kvt_ref[first_row + c:first_row + c + half, :], xb_ref[...])
            out_ref[0, c:c + half, :] = yt
            stage_kvt[first_row + c:first_row + c + half, :] = yt.astype(BF16)
        return piece

    u_halves = []

    def project_u0():
        u_halves.append(_proj_cols(xb_ref[...], w_in_ref, U_COLS, half))

    def project_u1_pool():
        u_halves.append(_proj_cols(xb_ref[...], w_in_ref, U_COLS + half, half))
        pooled = _pool_mix(jnp.concatenate(u_halves, axis=1), hist_ref[0], pool_w_ref, pool_scale_ref, u_ext,
                           lax.rem(s, n_t), tm=tm, start_pos=0)
        for g, y in enumerate(pooled):
            p_ref[0, :, g * POOL_GC:(g + 1) * POOL_GC] = y
        pool_ref[0] = _pool_state(u_ext, tm)

    @pl.when(s < n_tiles)
    def _():
        take_stage()
        xb_ref[...] = x_ref[0].astype(BF16)
        attend_far(attend_near([
            project_q(0), project_q(half),
            project_kv(k_ref, 0, 0), project_kv(k_ref, 0, half),
            project_kv(v_ref, SB_WIDTH, 0), project_kv(v_ref, SB_WIDTH, half),
            project_u0, project_u1_pool]))

    @pl.when(s == n_tiles)
    def _():
        take_stage()
        attend_far(attend_near([]))


def _proj_attn_prompt(x, hist, w_in, pool_w, pool_scale2, after, ffn_weights, n_conv):
    B, T, _ = x.shape
    tm = ATTN_BLOCK
    assert T % tm == 0 and T >= POOL_HIST
    n_t = T // tm
    n_tiles = B * n_t
    assert n_conv <= n_tiles and all(w.shape[0] % (n_conv * 16) == 0 for w in ffn_weights)
    chunked = [pl.BlockSpec((w.shape[0] // n_conv, w.shape[1]), lambda s: (jnp.minimum(s, n_conv - 1), 0))
               for w in ffn_weights]
    kern = functools.partial(_proj_attn_kernel, tm=tm, n_t=n_t, n_tiles=n_tiles, n_conv=n_conv)
    proj = lambda s: jnp.minimum(s, n_tiles - 1)
    attn = lambda s: jnp.maximum(s - 1, 0)
    const = lambda *shape: pl.BlockSpec(shape, lambda s: (0,) * len(shape), pipeline_mode=pl.Buffered(1))
    proj_rows = lambda width: pl.BlockSpec((1, tm, width), lambda s: (proj(s) // n_t, proj(s) % n_t, 0))
    proj_cols = pl.BlockSpec((1, SB_WIDTH, tm), lambda s: (proj(s) // n_t, 0, proj(s) % n_t))
    return pl.pallas_call(
        kern,
        grid=(n_tiles + 1,),
        in_specs=[
            proj_rows(D_MODEL),
            pl.BlockSpec((1, HIST_ROWS, POOL_WIDTH), lambda s: (proj(s) // n_t, 0, 0)),
            const(D_MODEL, 3 * SB_WIDTH + POOL_WIDTH),
            const(len(POOL_WINDOWS), POOL_GC, POOL_GC),
            const(1, POOL_WIDTH),
            pl.BlockSpec(memory_space=pl.ANY),
        ] + chunked,
        out_specs=[
            proj_cols,
            proj_cols,
            proj_rows(POOL_WIDTH),
            pl.BlockSpec((1, POOL_HIST, POOL_WIDTH), lambda s: (proj(s) // n_t, 0, 0)),
            pl.BlockSpec((1, tm, SB_WIDTH), lambda s: (attn(s) // n_t, attn(s) % n_t, 0)),
        ] + chunked,
        out_shape=[
            jax.ShapeDtypeStruct((B, SB_WIDTH, T), F32),
            jax.ShapeDtypeStruct((B, SB_WIDTH, T), F32),
            jax.ShapeDtypeStruct((B, T, POOL_WIDTH), BF16),
            jax.ShapeDtypeStruct((B, POOL_HIST, POOL_WIDTH), F32),
            jax.ShapeDtypeStruct((B, T, SB_WIDTH), BF16),
        ] + [jax.ShapeDtypeStruct(w.shape, BF16) for w in ffn_weights],
        scratch_shapes=[
            pltpu.VMEM(w_in.shape, BF16),
            pltpu.VMEM((2 * SB_WIDTH, D_MODEL), BF16),
            pltpu.VMEM((HIST_ROWS + tm, POOL_WIDTH), F32),
            pltpu.VMEM((tm, D_MODEL), BF16),
            pltpu.VMEM((tm, SB_WIDTH), BF16),
            pltpu.VMEM((2 * SB_WIDTH, tm), BF16),
            pltpu.VMEM((2 * SB_WIDTH, T), BF16),
        ] + _attn_scratch(tm),
        compiler_params=pltpu.CompilerParams(
            dimension_semantics=("arbitrary",), vmem_limit_bytes=VMEM_LIMIT),
        name="proj_attn_prompt",
    )(x, hist, w_in, pool_w, pool_scale2, after, *ffn_weights)


def _proj_attn_sample_kernel(x_ref, hist_ref, w_in_f32_ref, pool_w_ref, pool_scale_ref,
                             ck_ref, cv_ref, ck_hbm, cv_hbm,
                             k_ref, v_ref, p_ref, pool_ref, a_ref,
                             w_in_ref, q_all, kv_new, u_all, u_ext, ck_buf, cv_buf, qm_ref, decay_ref, acc_ref,
                             *, t_new, past, blk):
    b = pl.program_id(0)
    rows = pl.ds(pl.multiple_of(b * t_new, t_new), t_new)

    @pl.when(b == 0)
    def _():
        w_in_ref[...] = w_in_f32_ref[...].astype(BF16)
        u_ext[...] = jnp.zeros_like(u_ext)
        x = x_ref[...].astype(BF16)
        q_all[...] = (_proj_cols(x, w_in_ref, Q_COLS, SB_WIDTH) * Q_SCALE).astype(BF16)
        k = _proj_cols(x, w_in_ref, K_COLS, SB_WIDTH)
        k_ref[...] = k
        kv_new[:, 0:SB_WIDTH] = k.astype(BF16)
        v = _proj_cols(x, w_in_ref, V_COLS, SB_WIDTH)
        v_ref[...] = v
        kv_new[:, SB_WIDTH:2 * SB_WIDTH] = v.astype(BF16)
        u_all[...] = _proj_cols(x, w_in_ref, U_COLS, POOL_WIDTH)

    pooled = _pool_mix(u_all[rows, :], hist_ref[0], pool_w_ref, pool_scale_ref, u_ext, 0, tm=t_new, start_pos=past)
    for g, y in enumerate(pooled):
        p_ref[rows, g * POOL_GC:(g + 1) * POOL_GC] = y
    pool_ref[b] = _pool_state(u_ext, t_new)

    _store_masked_queries(q_all[rows, :], qm_ref)
    new_kv = lambda p: (kv_new[rows, p * LANES:(p + 1) * LANES],
                        kv_new[rows, SB_WIDTH + p * LANES:SB_WIDTH + (p + 1) * LANES])
    _sb_all_heads(qm_ref, new_kv, decay_ref, acc_ref, _strict_upper(t_new), _causal_mask(t_new), True)
    upper = _strict_upper(blk)
    recent = lambda p: (ck_ref[0, p * LANES:(p + 1) * LANES, :].astype(BF16),
                        cv_ref[0, p * LANES:(p + 1) * LANES, :].astype(BF16))
    least = _sb_all_heads(qm_ref, recent, decay_ref, acc_ref, upper, None, False, key_minor=True)

    def body(state):
        j, _ = state
        start = pl.multiple_of(j * blk, blk)
        pltpu.sync_copy(ck_hbm.at[b, :, pl.ds(start, blk)], ck_buf)
        pltpu.sync_copy(cv_hbm.at[b, :, pl.ds(start, blk)], cv_buf)
        older = lambda p: (ck_buf[p * LANES:(p + 1) * LANES, :].astype(BF16),
                           cv_buf[p * LANES:(p + 1) * LANES, :].astype(BF16))
        return j - 1, _sb_all_heads(qm_ref, older, decay_ref, acc_ref, upper, None, False, key_minor=True)

    lax.while_loop(_more_blocks, body, (past // blk - 2, least))
    for p in range(HEAD_PAIRS):
        a_ref[rows, p * LANES:(p + 1) * LANES] = acc_ref[p].astype(BF16)


def _proj_attn_sample(x, hist, w_in, pool_w, pool_scale2, cache_kt, cache_vt):
    B, t_new, _ = x.shape
    past = cache_kt.shape[2]
    blk = ATTN_BLOCK
    assert past % blk == 0 and t_new >= HIST_ROWS
    n = B * t_new
    kern = functools.partial(_proj_attn_sample_kernel, t_new=t_new, past=past, blk=blk)
    const = lambda *shape: pl.BlockSpec(shape, lambda b: (0,) * len(shape), pipeline_mode=pl.Buffered(1))
    whole = lambda *shape: pl.BlockSpec(shape, lambda b: (0,) * len(shape))
    recent = pl.BlockSpec((1, SB_WIDTH, blk), lambda b: (b, 0, past // blk - 1))
    k, v, p, pool, a = pl.pallas_call(
        kern,
        grid=(B,),
        in_specs=[
            const(n, D_MODEL),
            pl.BlockSpec((1, HIST_ROWS, POOL_WIDTH), lambda b: (b, 0, 0)),
            const(D_MODEL, 3 * SB_WIDTH + POOL_WIDTH),
            const(len(POOL_WINDOWS), POOL_GC, POOL_GC),
            const(1, POOL_WIDTH),
            recent, recent,
            pl.BlockSpec(memory_space=pl.ANY), pl.BlockSpec(memory_space=pl.ANY),
        ],
        out_specs=[whole(n, SB_WIDTH), whole(n, SB_WIDTH), whole(n, POOL_WIDTH),
                   whole(B, POOL_HIST, POOL_WIDTH), whole(n, SB_WIDTH)],
        out_shape=[
            jax.ShapeDtypeStruct((n, SB_WIDTH), F32),
            jax.ShapeDtypeStruct((n, SB_WIDTH), F32),
            jax.ShapeDtypeStruct((n, POOL_WIDTH), BF16),
            jax.ShapeDtypeStruct((B, POOL_HIST, POOL_WIDTH), F32),
            jax.ShapeDtypeStruct((n, SB_WIDTH), BF16),
        ],
        scratch_shapes=[
            pltpu.VMEM(w_in.shape, BF16),
            pltpu.VMEM((n, SB_WIDTH), BF16),
            pltpu.VMEM((n, 2 * SB_WIDTH), BF16),
            pltpu.VMEM((n, POOL_WIDTH), F32),
            pltpu.VMEM((HIST_ROWS + t_new, POOL_WIDTH), F32),
            pltpu.VMEM((SB_WIDTH, blk), F32),
            pltpu.VMEM((SB_WIDTH, blk), F32),
        ] + _attn_scratch(t_new),
        compiler_params=pltpu.CompilerParams(
            dimension_semantics=("arbitrary",), vmem_limit_bytes=VMEM_LIMIT),
        name="proj_attn_sample",
    )(x.reshape(n, D_MODEL), hist, w_in, pool_w, pool_scale2, cache_kt, cache_vt, cache_kt, cache_vt)
    return k, v, p, pool, a


def _layer_norm(x, g, b):
    mu = jnp.mean(x, axis=-1, keepdims=True)
    xc = x - mu
    var = jnp.mean(xc * xc, axis=-1, keepdims=True)
    return xc * lax.rsqrt(var + LN_EPS) * g + b


def _ffn_rows(x_ref, a_ref, p_ref, wo_ref, g1, b1, wg_ref, wu_ref, wd_ref, acc_ref, x1b_ref, filler):
    m = x_ref.shape[0]
    halves = tuple(slice(i * m // FFN_PARTS, (i + 1) * m // FFN_PARTS) for i in range(FFN_PARTS))
    chunks = [(sum(FF_CHUNKS[:i]), w) for i, w in enumerate(FF_CHUNKS)]

    def swiglu_chunk(x1b, chunk):
        c, w = chunk
        gate = _dot(x1b, wg_ref[:, c:c + w])
        up = _dot(x1b, wu_ref[:, c:c + w])
        hmid = (gate * jax.nn.sigmoid(gate) * up).astype(BF16)
        return _dot(hmid, wd_ref[c:c + w, :])

    mix = [_dot(a_ref[r, :], wo_ref[0:SB_WIDTH, :]) + _dot(p_ref[r, :], wo_ref[SB_WIDTH:, :]) for r in halves]
    for r, mx in zip(halves, mix):
        x1 = _layer_norm(ALPHA * x_ref[r, :] + mx, g1, b1)
        x1b_ref[r, :] = x1.astype(BF16)
        acc_ref[r, :] = ALPHA * x1 + swiglu_chunk(x1b_ref[r, :], chunks[0])
    total = None
    for j, c in enumerate(chunks[1:]):
        anchor = filler(j, len(chunks) - 1)
        x1b_ref[0:anchor.shape[0], 0:anchor.shape[1]] += anchor.astype(BF16)
        total = acc_ref[...] + swiglu_chunk(x1b_ref[...], c)
        if c is not chunks[-1]:
            acc_ref[...] = total
    return total


def _out_ffn_kernel(xp_ref, ap_ref, pp_ref, xs_ref, as_ref, ps_ref, wo_s, wg_s, wu_s, wd_s,
                    g1_ref, b1_ref, g2_ref, b2_ref, yp_ref, ys_ref,
                    accp_ref, accs_ref, pre_ref, x1b_ref, *, n_tiles):
    i = pl.program_id(0)

    @pl.when(i == 0)
    def _():
        pre_ref[...] = jnp.zeros_like(pre_ref)

    def finish_previous_tile(j, n):
        rows = yp_ref.shape[0] // n
        r = slice(j * rows, (j + 1) * rows)
        y = _layer_norm(pre_ref[r, :], g2_ref[...], b2_ref[...])
        yp_ref[r, :] = y
        top = jnp.max(y.reshape(rows // 16, 16, D_MODEL), axis=0)
        top = functools.reduce(jnp.maximum, [top[:, c:c + LANES] for c in range(0, D_MODEL, LANES)])
        return jnp.where(top > F32_MAX, top, 0.0)

    def rows(x_ref, a_ref, p_ref, acc_ref):
        m = x_ref.shape[0]
        return _ffn_rows(x_ref, a_ref, p_ref, wo_s, g1_ref[...], b1_ref[...], wg_s, wu_s, wd_s, acc_ref,
                         x1b_ref.at[0:m, :], finish_previous_tile)

    @pl.when(i < n_tiles)
    def _():
        pre_ref[...] = rows(xp_ref, ap_ref, pp_ref, accp_ref)

    @pl.when(i == n_tiles)
    def _():
        ys_ref[...] = _layer_norm(rows(xs_ref, as_ref, ps_ref, accs_ref), g2_ref[...], b2_ref[...])


def _out_ffn(xp, ap, pp, xs, a_s, ps, wo_b, wg_b, wu_b, wd_b, g1, b1, g2, b2, *, tm):
    Np, Ns = xp.shape[0], xs.shape[0]
    assert Np % tm == 0 and sum(FF_CHUNKS) == D_FF
    n_tiles = Np // tm
    tile = lambda i: jnp.clip(i, 0, n_tiles - 1)
    row = lambda width: pl.BlockSpec((tm, width), lambda i: (tile(i), 0))
    once = lambda *shape: pl.BlockSpec(shape, lambda i: (0,) * len(shape), pipeline_mode=pl.Buffered(1))
    kern = functools.partial(_out_ffn_kernel, n_tiles=n_tiles)
    return pl.pallas_call(
        kern,
        grid=(n_tiles + 1,),
        in_specs=[
            row(D_MODEL), row(SB_WIDTH), row(POOL_WIDTH),
            once(Ns, D_MODEL), once(Ns, SB_WIDTH), once(Ns, POOL_WIDTH),
            once(*wo_b.shape), once(*wg_b.shape), once(*wu_b.shape), once(*wd_b.shape),
            once(1, D_MODEL), once(1, D_MODEL), once(1, D_MODEL), once(1, D_MODEL),
        ],
        out_specs=[pl.BlockSpec((tm, D_MODEL), lambda i: (tile(i - 1), 0)),
                   pl.BlockSpec((Ns, D_MODEL), lambda i: (0, 0))],
        out_shape=[jax.ShapeDtypeStruct((Np, D_MODEL), F32), jax.ShapeDtypeStruct((Ns, D_MODEL), F32)],
        scratch_shapes=[pltpu.VMEM((tm, D_MODEL), F32), pltpu.VMEM((Ns, D_MODEL), F32),
                        pltpu.VMEM((tm, D_MODEL), F32), pltpu.VMEM((tm, D_MODEL), BF16)],
        compiler_params=pltpu.CompilerParams(
            dimension_semantics=("arbitrary",), vmem_limit_bytes=VMEM_LIMIT),
        name="out_proj_ffn",
    )(xp, ap, pp, xs, a_s, ps, wo_b, wg_b, wu_b, wd_b, g1, b1, g2, b2)


def kernel(x_prompt, x_sample, cache_k, cache_v, state_pool, w_in, pool_w, pool_scale, w_out,
           ln1_g, ln1_b, w_gate, w_up, w_down, ln2_g, ln2_b):
    Bp, Tp, _ = x_prompt.shape
    Bs, Ts, _ = x_sample.shape
    past = cache_k.shape[1]

    row2 = lambda a: a.reshape(1, -1).astype(F32)
    pool_scale2 = row2(pool_scale)

    hist_s = jnp.concatenate([jnp.zeros((Bs, HIST_ROWS - POOL_HIST, POOL_WIDTH), F32), state_pool.astype(F32)], axis=1)
    feature_major = lambda c: jnp.transpose(c, (0, 2, 3, 1)).reshape(Bs, SB_WIDTH, past)
    k_s, v_s, mix_s, pool_s, a_s = _proj_attn_sample(x_sample, hist_s, w_in, pool_w, pool_scale2,
                                                     feature_major(cache_k), feature_major(cache_v))

    hist_p = jnp.zeros((Bp, HIST_ROWS, POOL_WIDTH), F32)
    kt_p, vt_p, mix_p, pool_p, a_p, wo_b, wg_b, wu_b, wd_b = _proj_attn_prompt(
        x_prompt, hist_p, w_in, pool_w, pool_scale2, after=a_s, ffn_weights=(w_out, w_gate, w_up, w_down), n_conv=16)
    position_major = lambda c: jnp.transpose(c.reshape(Bp, SB_HEADS, SB_HEAD_DIM, Tp), (0, 3, 1, 2))

    y_p, y_s = _out_ffn(
        x_prompt.reshape(Bp * Tp, D_MODEL), a_p.reshape(Bp * Tp, SB_WIDTH), mix_p.reshape(Bp * Tp, POOL_WIDTH),
        x_sample.reshape(Bs * Ts, D_MODEL), a_s, mix_s,
        wo_b, wg_b, wu_b, wd_b, row2(ln1_g), row2(ln1_b), row2(ln2_g), row2(ln2_b), tm=512)

    heads = lambda a: a.reshape(Bs, Ts, SB_HEADS, SB_HEAD_DIM)
    return (y_p.reshape(Bp, Tp, D_MODEL), y_s.reshape(Bs, Ts, D_MODEL),
            position_major(kt_p), position_major(vt_p), pool_p, heads(k_s), heads(v_s), pool_s)
```

```python
import functools

import jax
import jax.numpy as jnp
from jax import lax
from jax.experimental import pallas as pl
from jax.experimental.pallas import tpu as pltpu

D_MODEL = 1024
SB_WIDTH = 512
SB_HEADS = 8
SB_HEAD_DIM = 64
POOL_WIDTH = 512
POOL_WINDOWS = (2, 4, 8, 16)
POOL_GC = 128
POOL_HIST = 15
HIST_ROWS = 16
D_FF = 2816
DEPTH = 1
ALPHA = (2 * DEPTH) ** 0.25
LN_EPS = 1e-5
LANES = 128
HEAD_PAIRS = SB_WIDTH // LANES
LOG2E = 1.4426950408889634
Q_SCALE = SB_HEAD_DIM ** -0.5 * LOG2E
SKIP_LOG2 = 160.0
SP_CLAMP = 32.0
ATTN_BLOCK = 256
FF_CHUNKS = (256, 768, 768, 768, 256)
FFN_PARTS = 2
VMEM_LIMIT = 56 * 1024 * 1024

F32 = jnp.float32
BF16 = jnp.bfloat16
F32_MAX = 3.4028234663852886e38


def _dot(a, b):
    return jnp.dot(a, b, preferred_element_type=F32)


def _dot_nt(a, b):
    return lax.dot_general(a, b, (((1,), (1,)), ((), ())), preferred_element_type=F32)


Q_COLS, K_COLS, V_COLS, U_COLS = (SB_WIDTH * i for i in range(4))


def _proj_cols(x, w_in_ref, start, width):
    return _dot(x, w_in_ref[:, start:start + width])


def _pool_mix(u, hist, pool_w_ref, pool_scale_ref, u_ext, t, *, tm, start_pos):
    u_ext[0:HIST_ROWS, :] = jnp.where(t == 0, hist, u_ext[tm:tm + HIST_ROWS, :])
    u_ext[HIST_ROWS:HIST_ROWS + tm, :] = u

    pos = start_pos + t * tm + lax.broadcasted_iota(jnp.int32, (tm, 1), 0)
    pooled = []
    for g, w in enumerate(POOL_WINDOWS):
        cols = slice(g * POOL_GC, (g + 1) * POOL_GC)
        s = u_ext[:, cols]
        span = 1
        while span < w:
            s = s + pltpu.roll(s, span, 0)
            span *= 2
        cnt = jnp.minimum(w, pos + 1).astype(F32)
        d = s[HIST_ROWS:, :] / cnt - u[:, cols]
        y = _dot(d.astype(BF16), pool_w_ref[g].astype(BF16)) * pool_scale_ref[:, cols]
        pooled.append(y.astype(BF16))
    return pooled


def _pool_state(u_ext, tm):
    return u_ext[tm + HIST_ROWS - POOL_HIST:tm + HIST_ROWS, :]


def _strict_upper(n):
    r = lax.broadcasted_iota(jnp.int32, (n, n), 0)
    c = lax.broadcasted_iota(jnp.int32, (n, n), 1)
    return (r > c).astype(BF16)


def _causal_mask(n):
    r = lax.broadcasted_iota(jnp.int32, (n, n), 0)
    c = lax.broadcasted_iota(jnp.int32, (n, n), 1)
    return c < r


def _store_masked_queries(q, qm_ref):
    m = q.shape[0]
    lower_half = lax.broadcasted_iota(jnp.int32, (m, LANES), 1) < SB_HEAD_DIM
    for p in range(HEAD_PAIRS):
        q2 = q[:, p * LANES:(p + 1) * LANES]
        zero = jnp.zeros_like(q2)
        qm_ref[p, 0:m, :] = jnp.where(lower_half, q2, zero)
        qm_ref[p, m:2 * m, :] = jnp.where(lower_half, zero, q2)


def _sb_all_heads(qm_ref, kv_block, decay_ref, acc_ref, upper, mask, first, key_minor=False, extra_decay=None,
                  fillers=()):
    fillers = list(fillers)
    m = qm_ref.shape[1] // 2
    lower_half = lax.broadcasted_iota(jnp.int32, (m, LANES), 1) < SB_HEAD_DIM
    mask2 = None if mask is None else jnp.concatenate([mask, mask], axis=0)
    kv = [kv_block(p) for p in range(HEAD_PAIRS)]

    logit, sps, rowsums = [], [], []
    for p in range(HEAD_PAIRS):
        z = (_dot if key_minor else _dot_nt)(qm_ref[p], kv[p][0])
        sp = jnp.maximum(z, jnp.log(1.0 + jnp.exp2(jnp.minimum(z, SP_CLAMP))) * LOG2E)
        spm = sp if mask2 is None else jnp.where(mask2, sp, 0.0)
        sps.append(spm.astype(BF16))
        logit.append(z - sp)
        rowsums.append(jnp.sum(spm, axis=1, keepdims=True))
        if fillers:
            fillers.pop(0)()

    later = _dot(jnp.concatenate(sps, axis=0), upper)

    least = None
    for p in range(HEAD_PAIRS):
        a = jnp.exp2(logit[p] - later[2 * m * p:2 * m * (p + 1)])
        if mask2 is not None:
            a = jnp.where(mask2, a, 0.0)
        pv = (_dot_nt if key_minor else _dot)(a.astype(BF16), kv[p][1])
        merged = jnp.where(lower_half, pv[0:m], pv[m:2 * m])
        rowsum = jnp.where(lower_half, rowsums[p][0:m], rowsums[p][m:2 * m])
        if first:
            decay = rowsum
            acc_ref[p] = merged
        else:
            prev = decay_ref[p]
            if extra_decay is not None:
                prev = prev + extra_decay
            decay = prev + rowsum
            acc_ref[p] += merged * jnp.exp2(-prev)
        decay_ref[p] = decay
        least = decay if least is None else jnp.minimum(least, decay)
    return jnp.min(least)


def _more_blocks(state):
    j, least = state
    return jnp.logical_and(j >= 0, least < SKIP_LOG2)


def _store_heads(acc_ref, o_ref):
    for p in range(HEAD_PAIRS):
        o_ref[0, :, p * LANES:(p + 1) * LANES] = acc_ref[p].astype(BF16)


def _attn_scratch(m):
    return [pltpu.VMEM((HEAD_PAIRS, 2 * m, LANES), BF16), pltpu.VMEM((HEAD_PAIRS, m, LANES), F32),
            pltpu.VMEM((HEAD_PAIRS, m, LANES), F32)]


def _proj_attn_kernel(x_ref, hist_ref, w_in_f32_ref, pool_w_ref, pool_scale_ref, after_ref,
                      wo_c, wg_c, wu_c, wd_c,
                      k_ref, v_ref, p_ref, pool_ref, a_ref, wo_b, wg_b, wu_b, wd_b,
                      w_in_ref, w_kvt_ref, u_ext, xb_ref, stage_q, stage_kvt, kvt_all, qm_ref, decay_ref, acc_ref,
                      *, tm, n_t, n_tiles, n_conv):
    s = pl.program_id(0)

    @pl.when(s < n_conv)
    def _():
        for c_ref, b_ref in ((wo_c, wo_b), (wg_c, wg_b), (wu_c, wu_b), (wd_c, wd_b)):
            b_ref[...] = c_ref[...].astype(BF16)

    @pl.when(s == 0)
    def _():
        w_in_ref[...] = w_in_f32_ref[...].astype(BF16)
        w_kvt_ref[...] = w_in_f32_ref[:, K_COLS:U_COLS].T.astype(BF16)
        stage_q[...] = jnp.zeros_like(stage_q)
        stage_kvt[...] = jnp.zeros_like(stage_kvt)
        u_ext[...] = jnp.zeros_like(u_ext)

    upper = _strict_upper(tm)
    qi = lax.rem(jnp.maximum(s - 1, 0), n_t)

    def kv_block(j):
        start = pl.multiple_of(j * tm, tm)
        return lambda p: (kvt_all[p * LANES:(p + 1) * LANES, pl.ds(start, tm)],
                          kvt_all[SB_WIDTH + p * LANES:SB_WIDTH + (p + 1) * LANES, pl.ds(start, tm)])

    def take_stage():
        kvt_all[:, pl.ds(pl.multiple_of(qi * tm, tm), tm)] = stage_kvt[...]
        _store_masked_queries(stage_q[...], qm_ref)

    def attend_near(fillers):
        _sb_all_heads(qm_ref, kv_block(qi), decay_ref, acc_ref, upper, _causal_mask(tm), True, key_minor=True,
                      fillers=fillers[0:HEAD_PAIRS])
        void = jnp.where(qi == 0, 2.0 * SKIP_LOG2, 0.0)
        return _sb_all_heads(qm_ref, kv_block(jnp.maximum(qi - 1, 0)), decay_ref, acc_ref, upper, None, False,
                             key_minor=True, extra_decay=void, fillers=fillers[HEAD_PAIRS:2 * HEAD_PAIRS])

    def attend_far(least):
        def body(state):
            j, _ = state
            return j - 1, _sb_all_heads(qm_ref, kv_block(j), decay_ref, acc_ref, upper, None, False, key_minor=True)

        lax.while_loop(_more_blocks, body, (qi - 2, least))
        _store_heads(acc_ref, a_ref)

    half = SB_WIDTH // 2

    def project_q(c):
        def piece():
            q = _proj_cols(xb_ref[...], w_in_ref, Q_COLS + c, half) * Q_SCALE
            stage_q[:, c:c + half] = q.astype(BF16)
        return piece

    def project_kv(out_ref, first_row, c):
        def piece():
            yt = _dot_nt(w_kvt_ref[first_row + c:first_row + c + half, :], xb_ref[...])
            out_ref[0, c:c + half, :] = yt
            stage_kvt[first_row + c:first_row + c + half, :] = yt.astype(BF16)
        return piece

    u_halves = []

    def project_u0():
        u_halves.append(_proj_cols(xb_ref[...], w_in_ref, U_COLS, half))

    def project_u1_pool():
        u_halves.append(_proj_cols(xb_ref[...], w_in_ref, U_COLS + half, half))
        pooled = _pool_mix(jnp.concatenate(u_halves, axis=1), hist_ref[0], pool_w_ref, pool_scale_ref, u_ext,
                           lax.rem(s, n_t), tm=tm, start_pos=0)
        for g, y in enumerate(pooled):
            p_ref[0, :, g * POOL_GC:(g + 1) * POOL_GC] = y
        pool_ref[0] = _pool_state(u_ext, tm)

    @pl.when(s < n_tiles)
    def _():
        take_stage()
        xb_ref[...] = x_ref[0].astype(BF16)
        attend_far(attend_near([
            project_q(0), project_q(half),
            project_kv(k_ref, 0, 0), project_kv(k_ref, 0, half),
            project_kv(v_ref, SB_WIDTH, 0), project_kv(v_ref, SB_WIDTH, half),
            project_u0, project_u1_pool]))

    @pl.when(s == n_tiles)
    def _():
        take_stage()
        attend_far(attend_near([]))


def _proj_attn_prompt(x, hist, w_in, pool_w, pool_scale2, after, ffn_weights, n_conv):
    B, T, _ = x.shape
    tm = ATTN_BLOCK
    assert T % tm == 0 and T >= POOL_HIST
    n_t = T // tm
    n_tiles = B * n_t
    assert n_conv <= n_tiles and all(w.shape[0] % (n_conv * 16) == 0 for w in ffn_weights)
    chunked = [pl.BlockSpec((w.shape[0] // n_conv, w.shape[1]), lambda s: (jnp.minimum(s, n_conv - 1), 0))
               for w in ffn_weights]
    kern = functools.partial(_proj_attn_kernel, tm=tm, n_t=n_t, n_tiles=n_tiles, n_conv=n_conv)
    proj = lambda s: jnp.minimum(s, n_tiles - 1)
    attn = lambda s: jnp.maximum(s - 1, 0)
    const = lambda *shape: pl.BlockSpec(shape, lambda s: (0,) * len(shape), pipeline_mode=pl.Buffered(1))
    proj_rows = lambda width: pl.BlockSpec((1, tm, width), lambda s: (proj(s) // n_t, proj(s) % n_t, 0))
    proj_cols = pl.BlockSpec((1, SB_WIDTH, tm), lambda s: (proj(s) // n_t, 0, proj(s) % n_t))
    return pl.pallas_call(
        kern,
        grid=(n_tiles + 1,),
        in_specs=[
            proj_rows(D_MODEL),
            pl.BlockSpec((1, HIST_ROWS, POOL_WIDTH), lambda s: (proj(s) // n_t, 0, 0)),
            const(D_MODEL, 3 * SB_WIDTH + POOL_WIDTH),
            const(len(POOL_WINDOWS), POOL_GC, POOL_GC),
            const(1, POOL_WIDTH),
            pl.BlockSpec(memory_space=pl.ANY),
        ] + chunked,
        out_specs=[
            proj_cols,
            proj_cols,
            proj_rows(POOL_WIDTH),
            pl.BlockSpec((1, POOL_HIST, POOL_WIDTH), lambda s: (proj(s) // n_t, 0, 0)),
            pl.BlockSpec((1, tm, SB_WIDTH), lambda s: (attn(s) // n_t, attn(s) % n_t, 0)),
        ] + chunked,
        out_shape=[
            jax.ShapeDtypeStruct((B, SB_WIDTH, T), F32),
            jax.ShapeDtypeStruct((B, SB_WIDTH, T), F32),
            jax.ShapeDtypeStruct((B, T, POOL_WIDTH), BF16),
            jax.ShapeDtypeStruct((B, POOL_HIST, POOL_WIDTH), F32),
            jax.ShapeDtypeStruct((B, T, SB_WIDTH), BF16),
        ] + [jax.ShapeDtypeStruct(w.shape, BF16) for w in ffn_weights],
        scratch_shapes=[
            pltpu.VMEM(w_in.shape, BF16),
            pltpu.VMEM((2 * SB_WIDTH, D_MODEL), BF16),
            pltpu.VMEM((HIST_ROWS + tm, POOL_WIDTH), F32),
            pltpu.VMEM((tm, D_MODEL), BF16),
            pltpu.VMEM((tm, SB_WIDTH), BF16),
            pltpu.VMEM((2 * SB_WIDTH, tm), BF16),
            pltpu.VMEM((2 * SB_WIDTH, T), BF16),
        ] + _attn_scratch(tm),
        compiler_params=pltpu.CompilerParams(
            dimension_semantics=("arbitrary",), vmem_limit_bytes=VMEM_LIMIT),
        name="proj_attn_prompt",
    )(x, hist, w_in, pool_w, pool_scale2, after, *ffn_weights)


def _proj_attn_sample_kernel(x_ref, hist_ref, w_in_f32_ref, pool_w_ref, pool_scale_ref,
                             ck_ref, cv_ref, ck_hbm, cv_hbm,
                             k_ref, v_ref, p_ref, pool_ref, a_ref,
                             w_in_ref, q_all, kv_new, u_all, u_ext, ck_buf, cv_buf, qm_ref, decay_ref, acc_ref,
                             *, t_new, past, blk):
    b = pl.program_id(0)
    rows = pl.ds(pl.multiple_of(b * t_new, t_new), t_new)

    @pl.when(b == 0)
    def _():
        w_in_ref[...] = w_in_f32_ref[...].astype(BF16)
        u_ext[...] = jnp.zeros_like(u_ext)
        x = x_ref[...].astype(BF16)
        q_all[...] = (_proj_cols(x, w_in_ref, Q_COLS, SB_WIDTH) * Q_SCALE).astype(BF16)
        k = _proj_cols(x, w_in_ref, K_COLS, SB_WIDTH)
        k_ref[...] = k
        kv_new[:, 0:SB_WIDTH] = k.astype(BF16)
        v = _proj_cols(x, w_in_ref, V_COLS, SB_WIDTH)
        v_ref[...] = v
        kv_new[:, SB_WIDTH:2 * SB_WIDTH] = v.astype(BF16)
        u_all[...] = _proj_cols(x, w_in_ref, U_COLS, POOL_WIDTH)

    pooled = _pool_mix(u_all[rows, :], hist_ref[0], pool_w_ref, pool_scale_ref, u_ext, 0, tm=t_new, start_pos=past)
    for g, y in enumerate(pooled):
        p_ref[rows, g * POOL_GC:(g + 1) * POOL_GC] = y
    pool_ref[b] = _pool_state(u_ext, t_new)

    _store_masked_queries(q_all[rows, :], qm_ref)
    new_kv = lambda p: (kv_new[rows, p * LANES:(p + 1) * LANES],
                        kv_new[rows, SB_WIDTH + p * LANES:SB_WIDTH + (p + 1) * LANES])
    _sb_all_heads(qm_ref, new_kv, decay_ref, acc_ref, _strict_upper(t_new), _causal_mask(t_new), True)
    upper = _strict_upper(blk)
    recent = lambda p: (ck_ref[0, p * LANES:(p + 1) * LANES, :].astype(BF16),
                        cv_ref[0, p * LANES:(p + 1) * LANES, :].astype(BF16))
    least = _sb_all_heads(qm_ref, recent, decay_ref, acc_ref, upper, None, False, key_minor=True)

    def body(state):
        j, _ = state
        start = pl.multiple_of(j * blk, blk)
        pltpu.sync_copy(ck_hbm.at[b, :, pl.ds(start, blk)], ck_buf)
        pltpu.sync_copy(cv_hbm.at[b, :, pl.ds(start, blk)], cv_buf)
        older = lambda p: (ck_buf[p * LANES:(p + 1) * LANES, :].astype(BF16),
                           cv_buf[p * LANES:(p + 1) * LANES, :].astype(BF16))
        return j - 1, _sb_all_heads(qm_ref, older, decay_ref, acc_ref, upper, None, False, key_minor=True)

    lax.while_loop(_more_blocks, body, (past // blk - 2, least))
    for p in range(HEAD_PAIRS):
        a_ref[rows, p * LANES:(p + 1) * LANES] = acc_ref[p].astype(BF16)


def _proj_attn_sample(x, hist, w_in, pool_w, pool_scale2, cache_kt, cache_vt):
    B, t_new, _ = x.shape
    past = cache_kt.shape[2]
    blk = ATTN_BLOCK
    assert past % blk == 0 and t_new >= HIST_ROWS
    n = B * t_new
    kern = functools.partial(_proj_attn_sample_kernel, t_new=t_new, past=past, blk=blk)
    const = lambda *shape: pl.BlockSpec(shape, lambda b: (0,) * len(shape), pipeline_mode=pl.Buffered(1))
    whole = lambda *shape: pl.BlockSpec(shape, lambda b: (0,) * len(shape))
    recent = pl.BlockSpec((1, SB_WIDTH, blk), lambda b: (b, 0, past // blk - 1))
    k, v, p, pool, a = pl.pallas_call(
        kern,
        grid=(B,),
        in_specs=[
            const(n, D_MODEL),
            pl.BlockSpec((1, HIST_ROWS, POOL_WIDTH), lambda b: (b, 0, 0)),
            const(D_MODEL, 3 * SB_WIDTH + POOL_WIDTH),
            const(len(POOL_WINDOWS), POOL_GC, POOL_GC),
            const(1, POOL_WIDTH),
            recent, recent,
            pl.BlockSpec(memory_space=pl.ANY), pl.BlockSpec(memory_space=pl.ANY),
        ],
        out_specs=[whole(n, SB_WIDTH), whole(n, SB_WIDTH), whole(n, POOL_WIDTH),
                   whole(B, POOL_HIST, POOL_WIDTH), whole(n, SB_WIDTH)],
        out_shape=[
            jax.ShapeDtypeStruct((n, SB_WIDTH), F32),
            jax.ShapeDtypeStruct((n, SB_WIDTH), F32),
            jax.ShapeDtypeStruct((n, POOL_WIDTH), BF16),
            jax.ShapeDtypeStruct((B, POOL_HIST, POOL_WIDTH), F32),
            jax.ShapeDtypeStruct((n, SB_WIDTH), BF16),
        ],
        scratch_shapes=[
            pltpu.VMEM(w_in.shape, BF16),
            pltpu.VMEM((n, SB_WIDTH), BF16),
            pltpu.VMEM((n, 2 * SB_WIDTH), BF16),
            pltpu.VMEM((n, POOL_WIDTH), F32),
            pltpu.VMEM((HIST_ROWS + t_new, POOL_WIDTH), F32),
            pltpu.VMEM((SB_WIDTH, blk), F32),
            pltpu.VMEM((SB_WIDTH, blk), F32),
        ] + _attn_scratch(t_new),
        compiler_params=pltpu.CompilerParams(
            dimension_semantics=("arbitrary",), vmem_limit_bytes=VMEM_LIMIT),
        name="proj_attn_sample",
    )(x.reshape(n, D_MODEL), hist, w_in, pool_w, pool_scale2, cache_kt, cache_vt, cache_kt, cache_vt)
    return k, v, p, pool, a


def _layer_norm(x, g, b):
    mu = jnp.mean(x, axis=-1, keepdims=True)
    xc = x - mu
    var = jnp.mean(xc * xc, axis=-1, keepdims=True)
    return xc * lax.rsqrt(var + LN_EPS) * g + b


def _ffn_rows(x_ref, a_ref, p_ref, wo_ref, g1, b1, wg_ref, wu_ref, wd_ref, acc_ref, x1b_ref, filler):
    m = x_ref.shape[0]
    halves = tuple(slice(i * m // FFN_PARTS, (i + 1) * m // FFN_PARTS) for i in range(FFN_PARTS))
    chunks = [(sum(FF_CHUNKS[:i]), w) for i, w in enumerate(FF_CHUNKS)]

    def swiglu_chunk(x1b, chunk):
        c, w = chunk
        gate = _dot(x1b, wg_ref[:, c:c + w])
        up = _dot(x1b, wu_ref[:, c:c + w])
        hmid = (gate * jax.nn.sigmoid(gate) * up).astype(BF16)
        return _dot(hmid, wd_ref[c:c + w, :])

    mix = [_dot(a_ref[r, :], wo_ref[0:SB_WIDTH, :]) + _dot(p_ref[r, :], wo_ref[SB_WIDTH:, :]) for r in halves]
    for r, mx in zip(halves, mix):
        x1 = _layer_norm(ALPHA * x_ref[r, :] + mx, g1, b1)
        x1b_ref[r, :] = x1.astype(BF16)
        acc_ref[r, :] = ALPHA * x1 + swiglu_chunk(x1b_ref[r, :], chunks[0])
    total = None
    for j, c in enumerate(chunks[1:]):
        anchor = filler(j, len(chunks) - 1)
        x1b_ref[0:anchor.shape[0], 0:anchor.shape[1]] += anchor.astype(BF16)
        total = acc_ref[...] + swiglu_chunk(x1b_ref[...], c)
        if c is not chunks[-1]:
            acc_ref[...] = total
    return total


def _out_ffn_kernel(xp_ref, ap_ref, pp_ref, xs_ref, as_ref, ps_ref, wo_s, wg_s, wu_s, wd_s,
                    g1_ref, b1_ref, g2_ref, b2_ref, yp_ref, ys_ref,
                    accp_ref, accs_ref, pre_ref, x1b_ref, *, n_tiles):
    i = pl.program_id(0)

    @pl.when(i == 0)
    def _():
        pre_ref[...] = jnp.zeros_like(pre_ref)

    def finish_previous_tile(j, n):
        rows = yp_ref.shape[0] // n
        r = slice(j * rows, (j + 1) * rows)
        y = _layer_norm(pre_ref[r, :], g2_ref[...], b2_ref[...])
        yp_ref[r, :] = y
        top = jnp.max(y.reshape(rows // 16, 16, D_MODEL), axis=0)
        top = functools.reduce(jnp.maximum, [top[:, c:c + LANES] for c in range(0, D_MODEL, LANES)])
        return jnp.where(top > F32_MAX, top, 0.0)

    def rows(x_ref, a_ref, p_ref, acc_ref):
        m = x_ref.shape[0]
        return _ffn_rows(x_ref, a_ref, p_ref, wo_s, g1_ref[...], b1_ref[...], wg_s, wu_s, wd_s, acc_ref,
                         x1b_ref.at[0:m, :], finish_previous_tile)

    @pl.when(i < n_tiles)
    def _():
        pre_ref[...] = rows(xp_ref, ap_ref, pp_ref, accp_ref)

    @pl.when(i == n_tiles)
    def _():
        ys_ref[...] = _layer_norm(rows(xs_ref, as_ref, ps_ref, accs_ref), g2_ref[...], b2_ref[...])


def _out_ffn(xp, ap, pp, xs, a_s, ps, wo_b, wg_b, wu_b, wd_b, g1, b1, g2, b2, *, tm):
    Np, Ns = xp.shape[0], xs.shape[0]
    assert Np % tm == 0 and sum(FF_CHUNKS) == D_FF
    n_tiles = Np // tm
    tile = lambda i: jnp.clip(i, 0, n_tiles - 1)
    row = lambda width: pl.BlockSpec((tm, width), lambda i: (tile(i), 0))
    once = lambda *shape: pl.BlockSpec(shape, lambda i: (0,) * len(shape), pipeline_mode=pl.Buffered(1))
    kern = functools.partial(_out_ffn_kernel, n_tiles=n_tiles)
    return pl.pallas_call(
        kern,
        grid=(n_tiles + 1,),
        in_specs=[
            row(D_MODEL), row(SB_WIDTH), row(POOL_WIDTH),
            once(Ns, D_MODEL), once(Ns, SB_WIDTH), once(Ns, POOL_WIDTH),
            once(*wo_b.shape), once(*wg_b.shape), once(*wu_b.shape), once(*wd_b.shape),
            once(1, D_MODEL), once(1, D_MODEL), once(1, D_MODEL), once(1, D_MODEL),
        ],
        out_specs=[pl.BlockSpec((tm, D_MODEL), lambda i: (tile(i - 1), 0)),
                   pl.BlockSpec((Ns, D_MODEL), lambda i: (0, 0))],
        out_shape=[jax.ShapeDtypeStruct((Np, D_MODEL), F32), jax.ShapeDtypeStruct((Ns, D_MODEL), F32)],
        scratch_shapes=[pltpu.VMEM((tm, D_MODEL), F32), pltpu.VMEM((Ns, D_MODEL), F32),
                        pltpu.VMEM((tm, D_MODEL), F32), pltpu.VMEM((tm, D_MODEL), BF16)],
        compiler_params=pltpu.CompilerParams(
            dimension_semantics=("arbitrary",), vmem_limit_bytes=VMEM_LIMIT),
        name="out_proj_ffn",
    )(xp, ap, pp, xs, a_s, ps, wo_b, wg_b, wu_b, wd_b, g1, b1, g2, b2)


def kernel(x_prompt, x_sample, cache_k, cache_v, state_pool, w_in, pool_w, pool_scale, w_out,
           ln1_g, ln1_b, w_gate, w_up, w_down, ln2_g, ln2_b):
    Bp, Tp, _ = x_prompt.shape
    Bs, Ts, _ = x_sample.shape
    past = cache_k.shape[1]

    row2 = lambda a: a.reshape(1, -1).astype(F32)
    pool_scale2 = row2(pool_scale)

    hist_s = jnp.concatenate([jnp.zeros((Bs, HIST_ROWS - POOL_HIST, POOL_WIDTH), F32), state_pool.astype(F32)], axis=1)
    feature_major = lambda c: jnp.transpose(c, (0, 2, 3, 1)).reshape(Bs, SB_WIDTH, past)
    k_s, v_s, mix_s, pool_s, a_s = _proj_attn_sample(x_sample, hist_s, w_in, pool_w, pool_scale2,
                                                     feature_major(cache_k), feature_major(cache_v))

    hist_p = jnp.zeros((Bp, HIST_ROWS, POOL_WIDTH), F32)
    kt_p, vt_p, mix_p, pool_p, a_p, wo_b, wg_b, wu_b, wd_b = _proj_attn_prompt(
        x_prompt, hist_p, w_in, pool_w, pool_scale2, after=a_s, ffn_weights=(w_out, w_gate, w_up, w_down), n_conv=16)
    position_major = lambda c: jnp.transpose(c.reshape(Bp, SB_HEADS, SB_HEAD_DIM, Tp), (0, 3, 1, 2))

    y_p, y_s = _out_ffn(
        x_prompt.reshape(Bp * Tp, D_MODEL), a_p.reshape(Bp * Tp, SB_WIDTH), mix_p.reshape(Bp * Tp, POOL_WIDTH),
        x_sample.reshape(Bs * Ts, D_MODEL), a_s, mix_s,
        wo_b, wg_b, wu_b, wd_b, row2(ln1_g), row2(ln1_b), row2(ln2_g), row2(ln2_b), tm=512)

    heads = lambda a: a.reshape(Bs, Ts, SB_HEADS, SB_HEAD_DIM)
    return (y_p.reshape(Bp, Tp, D_MODEL), y_s.reshape(Bs, Ts, D_MODEL),
            position_major(kt_p), position_major(vt_p), pool_p, heads(k_s), heads(v_s), pool_s)
```

```python
import functools

import jax
import jax.numpy as jnp
from jax import lax
from jax.experimental import pallas as pl
from jax.experimental.pallas import tpu as pltpu

D_MODEL = 1024
SB_WIDTH = 512
SB_HEADS = 8
SB_HEAD_DIM = 64
POOL_WIDTH = 512
POOL_WINDOWS = (2, 4, 8, 16)
POOL_GC = 128
POOL_HIST = 15
HIST_ROWS = 16
D_FF = 2816
DEPTH = 1
ALPHA = (2 * DEPTH) ** 0.25
LN_EPS = 1e-5
LANES = 128
HEAD_PAIRS = SB_WIDTH // LANES
LOG2E = 1.4426950408889634
Q_SCALE = SB_HEAD_DIM ** -0.5 * LOG2E
SKIP_LOG2 = 160.0
SP_CLAMP = 32.0
ATTN_BLOCK = 256
FF_CHUNKS = (256, 768, 768, 768, 256)
FFN_PARTS = 2
VMEM_LIMIT = 56 * 1024 * 1024

F32 = jnp.float32
BF16 = jnp.bfloat16
F32_MAX = 3.4028234663852886e38


def _dot(a, b):
    return jnp.dot(a, b, preferred_element_type=F32)


def _dot_nt(a, b):
    return lax.dot_general(a, b, (((1,), (1,)), ((), ())), preferred_element_type=F32)


Q_COLS, K_COLS, V_COLS, U_COLS = (SB_WIDTH * i for i in range(4))


def _proj_cols(x, w_in_ref, start, width):
    return _dot(x, w_in_ref[:, start:start + width])


def _pool_mix(u, hist, pool_w_ref, pool_scale_ref, u_ext, t, *, tm, start_pos):
    u_ext[0:HIST_ROWS, :] = jnp.where(t == 0, hist, u_ext[tm:tm + HIST_ROWS, :])
    u_ext[HIST_ROWS:HIST_ROWS + tm, :] = u

    pos = start_pos + t * tm + lax.broadcasted_iota(jnp.int32, (tm, 1), 0)
    pooled = []
    for g, w in enumerate(POOL_WINDOWS):
        cols = slice(g * POOL_GC, (g + 1) * POOL_GC)
        s = u_ext[:, cols]
        span = 1
        while span < w:
            s = s + pltpu.roll(s, span, 0)
            span *= 2
        cnt = jnp.minimum(w, pos + 1).astype(F32)
        d = s[HIST_ROWS:, :] / cnt - u[:, cols]
        y = _dot(d.astype(BF16), pool_w_ref[g].astype(BF16)) * pool_scale_ref[:, cols]
        pooled.append(y.astype(BF16))
    return pooled


def _pool_state(u_ext, tm):
    return u_ext[tm + HIST_ROWS - POOL_HIST:tm + HIST_ROWS, :]


def _strict_upper(n):
    r = lax.broadcasted_iota(jnp.int32, (n, n), 0)
    c = lax.broadcasted_iota(jnp.int32, (n, n), 1)
    return (r > c).astype(BF16)


def _causal_mask(n):
    r = lax.broadcasted_iota(jnp.int32, (n, n), 0)
    c = lax.broadcasted_iota(jnp.int32, (n, n), 1)
    return c < r


def _store_masked_queries(q, qm_ref):
    m = q.shape[0]
    lower_half = lax.broadcasted_iota(jnp.int32, (m, LANES), 1) < SB_HEAD_DIM
    for p in range(HEAD_PAIRS):
        q2 = q[:, p * LANES:(p + 1) * LANES]
        zero = jnp.zeros_like(q2)
        qm_ref[p, 0:m, :] = jnp.where(lower_half, q2, zero)
        qm_ref[p, m:2 * m, :] = jnp.where(lower_half, zero, q2)


def _sb_all_heads(qm_ref, kv_block, decay_ref, acc_ref, upper, mask, first, key_minor=False, extra_decay=None,
                  fillers=(), rows=None):
    fillers = list(fillers)
    tile_rows = qm_ref.shape[1] // 2
    r0, r1 = (0, tile_rows) if rows is None else rows
    m = r1 - r0
    lower_half = lax.broadcasted_iota(jnp.int32, (m, LANES), 1) < SB_HEAD_DIM
    mask2 = None if mask is None else jnp.concatenate([mask, mask], axis=0)
    kv = [kv_block(p) for p in range(HEAD_PAIRS)]

    def queries(p):
        if rows is None:
            return qm_ref[p]
        return jnp.concatenate([qm_ref[p, r0:r1, :], qm_ref[p, tile_rows + r0:tile_rows + r1, :]], axis=0)

    logit, sps, rowsums = [], [], []
    for p in range(HEAD_PAIRS):
        z = (_dot if key_minor else _dot_nt)(queries(p), kv[p][0])
        sp = jnp.maximum(z, jnp.log(1.0 + jnp.exp2(jnp.minimum(z, SP_CLAMP))) * LOG2E)
        spm = sp if mask2 is None else jnp.where(mask2, sp, 0.0)
        sps.append(spm.astype(BF16))
        logit.append(z - sp)
        rowsums.append(jnp.sum(spm, axis=1, keepdims=True))
        filler = fillers.pop(0) if fillers else None
        if filler is not None:
            filler()

    later = _dot(jnp.concatenate(sps, axis=0), upper)

    least = None
    for p in range(HEAD_PAIRS):
        a = jnp.exp2(logit[p] - later[2 * m * p:2 * m * (p + 1)])
        if mask2 is not None:
            a = jnp.where(mask2, a, 0.0)
        pv = (_dot_nt if key_minor else _dot)(a.astype(BF16), kv[p][1])
        merged = jnp.where(lower_half, pv[0:m], pv[m:2 * m])
        rowsum = jnp.where(lower_half, rowsums[p][0:m], rowsums[p][m:2 * m])
        if first:
            decay = rowsum
            acc_ref[p, r0:r1, :] = merged
        else:
            prev = decay_ref[p, r0:r1, :]
            if extra_decay is not None:
                prev = prev + extra_decay
            decay = prev + rowsum
            acc_ref[p, r0:r1, :] += merged * jnp.exp2(-prev)
        decay_ref[p, r0:r1, :] = decay
        least = decay if least is None else jnp.minimum(least, decay)
    return jnp.min(least)


def _more_blocks(state):
    j, least = state
    return jnp.logical_and(j >= 0, least < SKIP_LOG2)


def _store_heads(acc_ref, o_ref):
    for p in range(HEAD_PAIRS):
        o_ref[0, :, p * LANES:(p + 1) * LANES] = acc_ref[p].astype(BF16)


def _attn_scratch(m):
    return [pltpu.VMEM((HEAD_PAIRS, 2 * m, LANES), BF16), pltpu.VMEM((HEAD_PAIRS, m, LANES), F32),
            pltpu.VMEM((HEAD_PAIRS, m, LANES), F32)]


def _proj_attn_kernel(x_ref, hist_ref, w_in_f32_ref, pool_w_ref, pool_scale_ref, after_ref,
                      wo_c, wg_c, wu_c, wd_c,
                      k_ref, v_ref, p_ref, pool_ref, a_ref, wo_b, wg_b, wu_b, wd_b,
                      w_in_ref, w_kvt_ref, u_ext, xb_ref, stage_q, stage_kvt, kvt_all, qm_ref, decay_ref, acc_ref,
                      *, tm, n_t, n_tiles, n_conv):
    s = pl.program_id(0)

    @pl.when(s < n_conv)
    def _():
        for c_ref, b_ref in ((wo_c, wo_b), (wg_c, wg_b), (wu_c, wu_b), (wd_c, wd_b)):
            b_ref[...] = c_ref[...].astype(BF16)

    @pl.when(s == 0)
    def _():
        w_in_ref[...] = w_in_f32_ref[...].astype(BF16)
        w_kvt_ref[...] = w_in_f32_ref[:, K_COLS:U_COLS].T.astype(BF16)
        stage_q[...] = jnp.zeros_like(stage_q)
        stage_kvt[...] = jnp.zeros_like(stage_kvt)
        u_ext[...] = jnp.zeros_like(u_ext)

    blk = ATTN_BLOCK
    upper = _strict_upper(blk)
    qi = lax.rem(jnp.maximum(s - 1, 0), n_t)
    b0 = (tm // blk) * qi
    lo, hi = (0, blk), (blk, tm)

    def kv_block(j):
        start = pl.multiple_of(j * blk, blk)
        return lambda p: (kvt_all[p * LANES:(p + 1) * LANES, pl.ds(start, blk)],
                          kvt_all[SB_WIDTH + p * LANES:SB_WIDTH + (p + 1) * LANES, pl.ds(start, blk)])

    def take_stage():
        kvt_all[:, pl.ds(pl.multiple_of(qi * tm, tm), tm)] = stage_kvt[...]
        _store_masked_queries(stage_q[...], qm_ref)

    def attend(block, rows, mask, first, fillers=(), extra_decay=None):
        return _sb_all_heads(qm_ref, kv_block(block), decay_ref, acc_ref, upper, mask, first, key_minor=True,
                             extra_decay=extra_decay, fillers=fillers, rows=rows)

    def attend_near(fillers):
        n = HEAD_PAIRS
        causal = _causal_mask(blk)
        attend(b0 + 1, hi, causal, True, fillers[0:n])
        attend(b0, lo, causal, True, fillers[n:2 * n])
        least_hi = attend(b0, hi, None, False, fillers[2 * n:3 * n])
        void = jnp.where(qi == 0, 2.0 * SKIP_LOG2, 0.0)
        least_lo = attend(jnp.maximum(b0 - 1, 0), lo, None, False, fillers[3 * n:4 * n], extra_decay=void)
        return least_lo, least_hi

    def attend_far(least_lo, least_hi):
        for rows, first_block, least in ((lo, b0 - 2, least_lo), (hi, b0 - 1, least_hi)):
            def body(state, rows=rows):
                j, _ = state
                return j - 1, attend(j, rows, None, False)

            lax.while_loop(_more_blocks, body, (first_block, least))
        _store_heads(acc_ref, a_ref)

    half = SB_WIDTH // 2

    def project_q(c):
        def piece():
            q = _proj_cols(xb_ref[...], w_in_ref, Q_COLS + c, half) * Q_SCALE
            stage_q[:, c:c + half] = q.astype(BF16)
        return piece

    def project_kv(out_ref, first_row, c):
        def piece():
            yt = _dot_nt(w_kvt_ref[first_row + c:first_row + c + half, :], xb_ref[...])
            out_ref[0, c:c + half, :] = yt
            stage_kvt[first_row + c:first_row + c + half, :] = yt.astype(BF16)
        return piece

    u_halves = []

    def project_u0():
        u_halves.append(_proj_cols(xb_ref[...], w_in_ref, U_COLS, half))

    def project_u1_pool():
        u_halves.append(_proj_cols(xb_ref[...], w_in_ref, U_COLS + half, half))
        pooled = _pool_mix(jnp.concatenate(u_halves, axis=1), hist_ref[0], pool_w_ref, pool_scale_ref, u_ext,
                           lax.rem(s, n_t), tm=tm, start_pos=0)
        for g, y in enumerate(pooled):
            p_ref[0, :, g * POOL_GC:(g + 1) * POOL_GC] = y
        pool_ref[0] = _pool_state(u_ext, tm)

    @pl.when(s < n_tiles)
    def _():
        take_stage()
        xb_ref[...] = x_ref[0].astype(BF16)
        pieces = [project_q(0), project_q(half),
                  project_kv(k_ref, 0, 0), project_kv(k_ref, 0, half),
                  project_kv(v_ref, SB_WIDTH, 0), project_kv(v_ref, SB_WIDTH, half),
                  project_u0, project_u1_pool]
        attend_far(*attend_near([f for piece in pieces for f in (piece, None)]))

    @pl.when(s == n_tiles)
    def _():
        take_stage()
        attend_far(*attend_near([]))


def _proj_attn_prompt(x, hist, w_in, pool_w, pool_scale2, after, ffn_weights, n_conv):
    B, T, _ = x.shape
    tm = 2 * ATTN_BLOCK
    assert T % tm == 0 and T >= POOL_HIST
    n_t = T // tm
    n_tiles = B * n_t
    assert n_conv <= n_tiles and all(w.shape[0] % (n_conv * 16) == 0 for w in ffn_weights)
    chunked = [pl.BlockSpec((w.shape[0] // n_conv, w.shape[1]), lambda s: (jnp.minimum(s, n_conv - 1), 0))
               for w in ffn_weights]
    kern = functools.partial(_proj_attn_kernel, tm=tm, n_t=n_t, n_tiles=n_tiles, n_conv=n_conv)
    proj = lambda s: jnp.minimum(s, n_tiles - 1)
    attn = lambda s: jnp.maximum(s - 1, 0)
    const = lambda *shape: pl.BlockSpec(shape, lambda s: (0,) * len(shape), pipeline_mode=pl.Buffered(1))
    proj_rows = lambda width: pl.BlockSpec((1, tm, width), lambda s: (proj(s) // n_t, proj(s) % n_t, 0))
    proj_cols = pl.BlockSpec((1, SB_WIDTH, tm), lambda s: (proj(s) // n_t, 0, proj(s) % n_t))
    return pl.pallas_call(
        kern,
        grid=(n_tiles + 1,),
        in_specs=[
            proj_rows(D_MODEL),
            pl.BlockSpec((1, HIST_ROWS, POOL_WIDTH), lambda s: (proj(s) // n_t, 0, 0)),
            const(D_MODEL, 3 * SB_WIDTH + POOL_WIDTH),
            const(len(POOL_WINDOWS), POOL_GC, POOL_GC),
            const(1, POOL_WIDTH),
            pl.BlockSpec(memory_space=pl.ANY),
        ] + chunked,
        out_specs=[
            proj_cols,
            proj_cols,
            proj_rows(POOL_WIDTH),
            pl.BlockSpec((1, POOL_HIST, POOL_WIDTH), lambda s: (proj(s) // n_t, 0, 0)),
            pl.BlockSpec((1, tm, SB_WIDTH), lambda s: (attn(s) // n_t, attn(s) % n_t, 0)),
        ] + chunked,
        out_shape=[
            jax.ShapeDtypeStruct((B, SB_WIDTH, T), F32),
            jax.ShapeDtypeStruct((B, SB_WIDTH, T), F32),
            jax.ShapeDtypeStruct((B, T, POOL_WIDTH), BF16),
            jax.ShapeDtypeStruct((B, POOL_HIST, POOL_WIDTH), F32),
            jax.ShapeDtypeStruct((B, T, SB_WIDTH), BF16),
        ] + [jax.ShapeDtypeStruct(w.shape, BF16) for w in ffn_weights],
        scratch_shapes=[
            pltpu.VMEM(w_in.shape, BF16),
            pltpu.VMEM((2 * SB_WIDTH, D_MODEL), BF16),
            pltpu.VMEM((HIST_ROWS + tm, POOL_WIDTH), F32),
            pltpu.VMEM((tm, D_MODEL), BF16),
            pltpu.VMEM((tm, SB_WIDTH), BF16),
            pltpu.VMEM((2 * SB_WIDTH, tm), BF16),
            pltpu.VMEM((2 * SB_WIDTH, T), BF16),
        ] + _attn_scratch(tm),
        compiler_params=pltpu.CompilerParams(
            dimension_semantics=("arbitrary",), vmem_limit_bytes=VMEM_LIMIT),
        name="proj_attn_prompt",
    )(x, hist, w_in, pool_w, pool_scale2, after, *ffn_weights)


def _proj_attn_sample_kernel(x_ref, hist_ref, w_in_f32_ref, pool_w_ref, pool_scale_ref,
                             ck_ref, cv_ref, ck_hbm, cv_hbm,
                             k_ref, v_ref, p_ref, pool_ref, a_ref,
                             w_in_ref, q_all, kv_new, u_all, u_ext, ck_buf, cv_buf, qm_ref, decay_ref, acc_ref,
                             *, t_new, past, blk):
    b = pl.program_id(0)
    rows = pl.ds(pl.multiple_of(b * t_new, t_new), t_new)

    @pl.when(b == 0)
    def _():
        w_in_ref[...] = w_in_f32_ref[...].astype(BF16)
        u_ext[...] = jnp.zeros_like(u_ext)
        x = x_ref[...].astype(BF16)
        q_all[...] = (_proj_cols(x, w_in_ref, Q_COLS, SB_WIDTH) * Q_SCALE).astype(BF16)
        k = _proj_cols(x, w_in_ref, K_COLS, SB_WIDTH)
        k_ref[...] = k
        kv_new[:, 0:SB_WIDTH] = k.astype(BF16)
        v = _proj_cols(x, w_in_ref, V_COLS, SB_WIDTH)
        v_ref[...] = v
        kv_new[:, SB_WIDTH:2 * SB_WIDTH] = v.astype(BF16)
        u_all[...] = _proj_cols(x, w_in_ref, U_COLS, POOL_WIDTH)

    pooled = _pool_mix(u_all[rows, :], hist_ref[0], pool_w_ref, pool_scale_ref, u_ext, 0, tm=t_new, start_pos=past)
    for g, y in enumerate(pooled):
        p_ref[rows, g * POOL_GC:(g + 1) * POOL_GC] = y
    pool_ref[b] = _pool_state(u_ext, t_new)

    _store_masked_queries(q_all[rows, :], qm_ref)
    new_kv = lambda p: (kv_new[rows, p * LANES:(p + 1) * LANES],
                        kv_new[rows, SB_WIDTH + p * LANES:SB_WIDTH + (p + 1) * LANES])
    _sb_all_heads(qm_ref, new_kv, decay_ref, acc_ref, _strict_upper(t_new), _causal_mask(t_new), True)
    upper = _strict_upper(blk)
    recent = lambda p: (ck_ref[0, p * LANES:(p + 1) * LANES, :].astype(BF16),
                        cv_ref[0, p * LANES:(p + 1) * LANES, :].astype(BF16))
    least = _sb_all_heads(qm_ref, recent, decay_ref, acc_ref, upper, None, False, key_minor=True)

    def body(state):
        j, _ = state
        start = pl.multiple_of(j * blk, blk)
        pltpu.sync_copy(ck_hbm.at[b, :, pl.ds(start, blk)], ck_buf)
        pltpu.sync_copy(cv_hbm.at[b, :, pl.ds(start, blk)], cv_buf)
        older = lambda p: (ck_buf[p * LANES:(p + 1) * LANES, :].astype(BF16),
                           cv_buf[p * LANES:(p + 1) * LANES, :].astype(BF16))
        return j - 1, _sb_all_heads(qm_ref, older, decay_ref, acc_ref, upper, None, False, key_minor=True)

    lax.while_loop(_more_blocks, body, (past // blk - 2, least))
    for p in range(HEAD_PAIRS):
        a_ref[rows, p * LANES:(p + 1) * LANES] = acc_ref[p].astype(BF16)


def _proj_attn_sample(x, hist, w_in, pool_w, pool_scale2, cache_kt, cache_vt):
    B, t_new, _ = x.shape
    past = cache_kt.shape[2]
    blk = ATTN_BLOCK
    assert past % blk == 0 and t_new >= HIST_ROWS
    n = B * t_new
    kern = functools.partial(_proj_attn_sample_kernel, t_new=t_new, past=past, blk=blk)
    const = lambda *shape: pl.BlockSpec(shape, lambda b: (0,) * len(shape), pipeline_mode=pl.Buffered(1))
    whole = lambda *shape: pl.BlockSpec(shape, lambda b: (0,) * len(shape))
    recent = pl.BlockSpec((1, SB_WIDTH, blk), lambda b: (b, 0, past // blk - 1))
    k, v, p, pool, a = pl.pallas_call(
        kern,
        grid=(B,),
        in_specs=[
            const(n, D_MODEL),
            pl.BlockSpec((1, HIST_ROWS, POOL_WIDTH), lambda b: (b, 0, 0)),
            const(D_MODEL, 3 * SB_WIDTH + POOL_WIDTH),
            const(len(POOL_WINDOWS), POOL_GC, POOL_GC),
            const(1, POOL_WIDTH),
            recent, recent,
            pl.BlockSpec(memory_space=pl.ANY), pl.BlockSpec(memory_space=pl.ANY),
        ],
        out_specs=[whole(n, SB_WIDTH), whole(n, SB_WIDTH), whole(n, POOL_WIDTH),
                   whole(B, POOL_HIST, POOL_WIDTH), whole(n, SB_WIDTH)],
        out_shape=[
            jax.ShapeDtypeStruct((n, SB_WIDTH), F32),
            jax.ShapeDtypeStruct((n, SB_WIDTH), F32),
            jax.ShapeDtypeStruct((n, POOL_WIDTH), BF16),
            jax.ShapeDtypeStruct((B, POOL_HIST, POOL_WIDTH), F32),
            jax.ShapeDtypeStruct((n, SB_WIDTH), BF16),
        ],
        scratch_shapes=[
            pltpu.VMEM(w_in.shape, BF16),
            pltpu.VMEM((n, SB_WIDTH), BF16),
            pltpu.VMEM((n, 2 * SB_WIDTH), BF16),
            pltpu.VMEM((n, POOL_WIDTH), F32),
            pltpu.VMEM((HIST_ROWS + t_new, POOL_WIDTH), F32),
            pltpu.VMEM((SB_WIDTH, blk), F32),
            pltpu.VMEM((SB_WIDTH, blk), F32),
        ] + _attn_scratch(t_new),
        compiler_params=pltpu.CompilerParams(
            dimension_semantics=("arbitrary",), vmem_limit_bytes=VMEM_LIMIT),
        name="proj_attn_sample",
    )(x.reshape(n, D_MODEL), hist, w_in, pool_w, pool_scale2, cache_kt, cache_vt, cache_kt, cache_vt)
    return k, v, p, pool, a


def _layer_norm(x, g, b):
    mu = jnp.mean(x, axis=-1, keepdims=True)
    xc = x - mu
    var = jnp.mean(xc * xc, axis=-1, keepdims=True)
    return xc * lax.rsqrt(var + LN_EPS) * g + b


def _ffn_rows(x_ref, a_ref, p_ref, wo_ref, g1, b1, wg_ref, wu_ref, wd_ref, acc_ref, x1b_ref, filler):
    m = x_ref.shape[0]
    halves = tuple(slice(i * m // FFN_PARTS, (i + 1) * m // FFN_PARTS) for i in range(FFN_PARTS))
    chunks = [(sum(FF_CHUNKS[:i]), w) for i, w in enumerate(FF_CHUNKS)]

    def swiglu_chunk(x1b, chunk):
        c, w = chunk
        gate = _dot(x1b, wg_ref[:, c:c + w])
        up = _dot(x1b, wu_ref[:, c:c + w])
        hmid = (gate * jax.nn.sigmoid(gate) * up).astype(BF16)
        return _dot(hmid, wd_ref[c:c + w, :])

    mix = [_dot(a_ref[r, :], wo_ref[0:SB_WIDTH, :]) + _dot(p_ref[r, :], wo_ref[SB_WIDTH:, :]) for r in halves]
    for r, mx in zip(halves, mix):
        x1 = _layer_norm(ALPHA * x_ref[r, :] + mx, g1, b1)
        x1b_ref[r, :] = x1.astype(BF16)
        acc_ref[r, :] = ALPHA * x1 + swiglu_chunk(x1b_ref[r, :], chunks[0])
    total = None
    for j, c in enumerate(chunks[1:]):
        anchor = filler(j, len(chunks) - 1)
        x1b_ref[0:anchor.shape[0], 0:anchor.shape[1]] += anchor.astype(BF16)
        total = acc_ref[...] + swiglu_chunk(x1b_ref[...], c)
        if c is not chunks[-1]:
            acc_ref[...] = total
    return total


def _out_ffn_kernel(xp_ref, ap_ref, pp_ref, xs_ref, as_ref, ps_ref, wo_s, wg_s, wu_s, wd_s,
                    g1_ref, b1_ref, g2_ref, b2_ref, yp_ref, ys_ref,
                    accp_ref, accs_ref, pre_ref, x1b_ref, *, n_tiles):
    i = pl.program_id(0)

    @pl.when(i == 0)
    def _():
        pre_ref[...] = jnp.zeros_like(pre_ref)

    def finish_previous_tile(j, n):
        rows = yp_ref.shape[0] // n
        r = slice(j * rows, (j + 1) * rows)
        y = _layer_norm(pre_ref[r, :], g2_ref[...], b2_ref[...])
        yp_ref[r, :] = y
        top = jnp.max(y.reshape(rows // 16, 16, D_MODEL), axis=0)
        top = functools.reduce(jnp.maximum, [top[:, c:c + LANES] for c in range(0, D_MODEL, LANES)])
        return jnp.where(top > F32_MAX, top, 0.0)

    def rows(x_ref, a_ref, p_ref, acc_ref):
        m = x_ref.shape[0]
        return _ffn_rows(x_ref, a_ref, p_ref, wo_s, g1_ref[...], b1_ref[...], wg_s, wu_s, wd_s, acc_ref,
                         x1b_ref.at[0:m, :], finish_previous_tile)

    @pl.when(i < n_tiles)
    def _():
        pre_ref[...] = rows(xp_ref, ap_ref, pp_ref, accp_ref)

    @pl.when(i == n_tiles)
    def _():
        ys_ref[...] = _layer_norm(rows(xs_ref, as_ref, ps_ref, accs_ref), g2_ref[...], b2_ref[...])


def _out_ffn(xp, ap, pp, xs, a_s, ps, wo_b, wg_b, wu_b, wd_b, g1, b1, g2, b2, *, tm):
    Np, Ns = xp.shape[0], xs.shape[0]
    assert Np % tm == 0 and sum(FF_CHUNKS) == D_FF
    n_tiles = Np // tm
    tile = lambda i: jnp.clip(i, 0, n_tiles - 1)
    row = lambda width: pl.BlockSpec((tm, width), lambda i: (tile(i), 0))
    once = lambda *shape: pl.BlockSpec(shape, lambda i: (0,) * len(shape), pipeline_mode=pl.Buffered(1))
    kern = functools.partial(_out_ffn_kernel, n_tiles=n_tiles)
    return pl.pallas_call(
        kern,
        grid=(n_tiles + 1,),
        in_specs=[
            row(D_MODEL), row(SB_WIDTH), row(POOL_WIDTH),
            once(Ns, D_MODEL), once(Ns, SB_WIDTH), once(Ns, POOL_WIDTH),
            once(*wo_b.shape), once(*wg_b.shape), once(*wu_b.shape), once(*wd_b.shape),
            once(1, D_MODEL), once(1, D_MODEL), once(1, D_MODEL), once(1, D_MODEL),
        ],
        out_specs=[pl.BlockSpec((tm, D_MODEL), lambda i: (tile(i - 1), 0)),
                   pl.BlockSpec((Ns, D_MODEL), lambda i: (0, 0))],
        out_shape=[jax.ShapeDtypeStruct((Np, D_MODEL), F32), jax.ShapeDtypeStruct((Ns, D_MODEL), F32)],
        scratch_shapes=[pltpu.VMEM((tm, D_MODEL), F32), pltpu.VMEM((Ns, D_MODEL), F32),
                        pltpu.VMEM((tm, D_MODEL), F32), pltpu.VMEM((tm, D_MODEL), BF16)],
        compiler_params=pltpu.CompilerParams(
            dimension_semantics=("arbitrary",), vmem_limit_bytes=VMEM_LIMIT),
        name="out_proj_ffn",
    )(xp, ap, pp, xs, a_s, ps, wo_b, wg_b, wu_b, wd_b, g1, b1, g2, b2)


def kernel(x_prompt, x_sample, cache_k, cache_v, state_pool, w_in, pool_w, pool_scale, w_out,
           ln1_g, ln1_b, w_gate, w_up, w_down, ln2_g, ln2_b):
    Bp, Tp, _ = x_prompt.shape
    Bs, Ts, _ = x_sample.shape
    past = cache_k.shape[1]

    row2 = lambda a: a.reshape(1, -1).astype(F32)
    pool_scale2 = row2(pool_scale)

    hist_s = jnp.concatenate([jnp.zeros((Bs, HIST_ROWS - POOL_HIST, POOL_WIDTH), F32), state_pool.astype(F32)], axis=1)
    feature_major = lambda c: jnp.transpose(c, (0, 2, 3, 1)).reshape(Bs, SB_WIDTH, past)
    k_s, v_s, mix_s, pool_s, a_s = _proj_attn_sample(x_sample, hist_s, w_in, pool_w, pool_scale2,
                                                     feature_major(cache_k), feature_major(cache_v))

    hist_p = jnp.zeros((Bp, HIST_ROWS, POOL_WIDTH), F32)
    kt_p, vt_p, mix_p, pool_p, a_p, wo_b, wg_b, wu_b, wd_b = _proj_attn_prompt(
        x_prompt, hist_p, w_in, pool_w, pool_scale2, after=a_s, ffn_weights=(w_out, w_gate, w_up, w_down), n_conv=16)
    position_major = lambda c: jnp.transpose(c.reshape(Bp, SB_HEADS, SB_HEAD_DIM, Tp), (0, 3, 1, 2))

    y_p, y_s = _out_ffn(
        x_prompt.reshape(Bp * Tp, D_MODEL), a_p.reshape(Bp * Tp, SB_WIDTH), mix_p.reshape(Bp * Tp, POOL_WIDTH),
        x_sample.reshape(Bs * Ts, D_MODEL), a_s, mix_s,
        wo_b, wg_b, wu_b, wd_b, row2(ln1_g), row2(ln1_b), row2(ln2_g), row2(ln2_b), tm=512)

    heads = lambda a: a.reshape(Bs, Ts, SB_HEADS, SB_HEAD_DIM)
    return (y_p.reshape(Bp, Tp, D_MODEL), y_s.reshape(Bs, Ts, D_MODEL),
            position_major(kt_p), position_major(vt_p), pool_p, heads(k_s), heads(v_s), pool_s)
```

```python
import functools

import jax
import jax.numpy as jnp
from jax import lax
from jax.experimental import pallas as pl
from jax.experimental.pallas import tpu as pltpu

D_MODEL = 1024
SB_WIDTH = 512
SB_HEADS = 8
SB_HEAD_DIM = 64
POOL_WIDTH = 512
POOL_WINDOWS = (2, 4, 8, 16)
POOL_GC = 128
POOL_HIST = 15
HIST_ROWS = 16
D_FF = 2816
DEPTH = 1
ALPHA = (2 * DEPTH) ** 0.25
LN_EPS = 1e-5
LANES = 128
HEAD_PAIRS = SB_WIDTH // LANES
LOG2E = 1.4426950408889634
Q_SCALE = SB_HEAD_DIM ** -0.5 * LOG2E
SKIP_LOG2 = 160.0
SP_CLAMP = 32.0
ATTN_BLOCK = 256
FF_CHUNKS = (256, 768, 768, 768, 256)
FFN_PARTS = 2
VMEM_LIMIT = 56 * 1024 * 1024

F32 = jnp.float32
BF16 = jnp.bfloat16
F32_MAX = 3.4028234663852886e38


def _dot(a, b):
    return jnp.dot(a, b, preferred_element_type=F32)


def _dot_nt(a, b):
    return lax.dot_general(a, b, (((1,), (1,)), ((), ())), preferred_element_type=F32)


Q_COLS, K_COLS, V_COLS, U_COLS = (SB_WIDTH * i for i in range(4))


def _proj_cols(x, w_in_ref, start, width):
    return _dot(x, w_in_ref[:, start:start + width])


def _pool_mix(u, hist, pool_w_ref, pool_scale_ref, u_ext, t, *, tm, start_pos):
    u_ext[0:HIST_ROWS, :] = jnp.where(t == 0, hist, u_ext[tm:tm + HIST_ROWS, :])
    u_ext[HIST_ROWS:HIST_ROWS + tm, :] = u

    pos = start_pos + t * tm + lax.broadcasted_iota(jnp.int32, (tm, 1), 0)
    pooled = []
    for g, w in enumerate(POOL_WINDOWS):
        cols = slice(g * POOL_GC, (g + 1) * POOL_GC)
        s = u_ext[:, cols]
        span = 1
        while span < w:
            s = s + pltpu.roll(s, span, 0)
            span *= 2
        cnt = jnp.minimum(w, pos + 1).astype(F32)
        d = s[HIST_ROWS:, :] / cnt - u[:, cols]
        y = _dot(d.astype(BF16), pool_w_ref[g].astype(BF16)) * pool_scale_ref[:, cols]
        pooled.append(y.astype(BF16))
    return pooled


def _pool_state(u_ext, tm):
    return u_ext[tm + HIST_ROWS - POOL_HIST:tm + HIST_ROWS, :]


def _strict_upper(n):
    r = lax.broadcasted_iota(jnp.int32, (n, n), 0)
    c = lax.broadcasted_iota(jnp.int32, (n, n), 1)
    return (r > c).astype(BF16)


def _causal_mask(n):
    r = lax.broadcasted_iota(jnp.int32, (n, n), 0)
    c = lax.broadcasted_iota(jnp.int32, (n, n), 1)
    return c < r


def _store_masked_queries(q, qm_ref):
    m = q.shape[0]
    lower_half = lax.broadcasted_iota(jnp.int32, (m, LANES), 1) < SB_HEAD_DIM
    for p in range(HEAD_PAIRS):
        q2 = q[:, p * LANES:(p + 1) * LANES]
        zero = jnp.zeros_like(q2)
        qm_ref[p, 0:m, :] = jnp.where(lower_half, q2, zero)
        qm_ref[p, m:2 * m, :] = jnp.where(lower_half, zero, q2)


def _sb_all_heads(qm_ref, kv_block, decay_ref, acc_ref, upper, mask, first, key_minor=False, extra_decay=None,
                  fillers=(), rows=None):
    fillers = list(fillers)
    tile_rows = qm_ref.shape[1] // 2
    r0, r1 = (0, tile_rows) if rows is None else rows
    m = r1 - r0
    lower_half = lax.broadcasted_iota(jnp.int32, (m, LANES), 1) < SB_HEAD_DIM
    mask2 = None if mask is None else jnp.concatenate([mask, mask], axis=0)
    kv = [kv_block(p) for p in range(HEAD_PAIRS)]

    def queries(p):
        if rows is None:
            return qm_ref[p]
        return jnp.concatenate([qm_ref[p, r0:r1, :], qm_ref[p, tile_rows + r0:tile_rows + r1, :]], axis=0)

    logit, sps, rowsums = [], [], []
    for p in range(HEAD_PAIRS):
        z = (_dot if key_minor else _dot_nt)(queries(p), kv[p][0])
        sp = jnp.maximum(z, jnp.log(1.0 + jnp.exp2(jnp.minimum(z, SP_CLAMP))) * LOG2E)
        spm = sp if mask2 is None else jnp.where(mask2, sp, 0.0)
        sps.append(spm.astype(BF16))
        logit.append(z - sp)
        rowsums.append(jnp.sum(spm, axis=1, keepdims=True))
        filler = fillers.pop(0) if fillers else None
        if filler is not None:
            filler()

    later = _dot(jnp.concatenate(sps, axis=0), upper)

    least = None
    for p in range(HEAD_PAIRS):
        a = jnp.exp2(logit[p] - later[2 * m * p:2 * m * (p + 1)])
        if mask2 is not None:
            a = jnp.where(mask2, a, 0.0)
        pv = (_dot_nt if key_minor else _dot)(a.astype(BF16), kv[p][1])
        merged = jnp.where(lower_half, pv[0:m], pv[m:2 * m])
        rowsum = jnp.where(lower_half, rowsums[p][0:m], rowsums[p][m:2 * m])
        if first:
            decay = rowsum
            acc_ref[p, r0:r1, :] = merged
        else:
            prev = decay_ref[p, r0:r1, :]
            if extra_decay is not None:
                prev = prev + extra_decay
            decay = prev + rowsum
            acc_ref[p, r0:r1, :] += merged * jnp.exp2(-prev)
        decay_ref[p, r0:r1, :] = decay
        least = decay if least is None else jnp.minimum(least, decay)
    return jnp.min(least)


def _more_blocks(state):
    j, least = state
    return jnp.logical_and(j >= 0, least < SKIP_LOG2)


def _store_heads(acc_ref, o_ref):
    for p in range(HEAD_PAIRS):
        o_ref[0, :, p * LANES:(p + 1) * LANES] = acc_ref[p].astype(BF16)


def _attn_scratch(m):
    return [pltpu.VMEM((HEAD_PAIRS, 2 * m, LANES), BF16), pltpu.VMEM((HEAD_PAIRS, m, LANES), F32),
            pltpu.VMEM((HEAD_PAIRS, m, LANES), F32)]


def _proj_attn_kernel(x_ref, hist_ref, w_in_f32_ref, pool_w_ref, pool_scale_ref, after_ref,
                      wo_c, wg_c, wu_c, wd_c,
                      k_ref, v_ref, p_ref, pool_ref, a_ref, wo_b, wg_b, wu_b, wd_b,
                      w_in_ref, w_kvt_ref, u_ext, xb_ref, stage_q, stage_kvt, kvt_all, qm_ref, decay_ref, acc_ref,
                      *, tm, n_t, n_tiles, n_conv):
    s = pl.program_id(0)

    @pl.when(s < n_conv)
    def _():
        for c_ref, b_ref in ((wo_c, wo_b), (wg_c, wg_b), (wu_c, wu_b), (wd_c, wd_b)):
            b_ref[...] = c_ref[...].astype(BF16)

    @pl.when(s == 0)
    def _():
        w_in_ref[...] = w_in_f32_ref[...].astype(BF16)
        w_kvt_ref[...] = w_in_f32_ref[:, K_COLS:U_COLS].T.astype(BF16)
        stage_q[...] = jnp.zeros_like(stage_q)
        stage_kvt[...] = jnp.zeros_like(stage_kvt)
        u_ext[...] = jnp.zeros_like(u_ext)

    blk = ATTN_BLOCK
    upper = _strict_upper(blk)
    qi = lax.rem(jnp.maximum(s - 1, 0), n_t)
    b0 = (tm // blk) * qi
    lo, hi = (0, blk), (blk, tm)

    def kv_block(j):
        start = pl.multiple_of(j * blk, blk)
        return lambda p: (kvt_all[p * LANES:(p + 1) * LANES, pl.ds(start, blk)],
                          kvt_all[SB_WIDTH + p * LANES:SB_WIDTH + (p + 1) * LANES, pl.ds(start, blk)])

    def take_stage():
        kvt_all[:, pl.ds(pl.multiple_of(qi * tm, tm), tm)] = stage_kvt[...]
        _store_masked_queries(stage_q[...], qm_ref)

    def attend(block, rows, mask, first, fillers=(), extra_decay=None):
        return _sb_all_heads(qm_ref, kv_block(block), decay_ref, acc_ref, upper, mask, first, key_minor=True,
                             extra_decay=extra_decay, fillers=fillers, rows=rows)

    def attend_near(fillers):
        n = HEAD_PAIRS
        causal = _causal_mask(blk)
        attend(b0, lo, causal, True, fillers[0:n])
        void = jnp.where(qi == 0, 2.0 * SKIP_LOG2, 0.0)
        least_lo = attend(jnp.maximum(b0 - 1, 0), lo, None, False, fillers[n:2 * n], extra_decay=void)
        attend(b0 + 1, hi, causal, True, fillers[2 * n:3 * n])
        least_hi = attend(b0, hi, None, False, fillers[3 * n:4 * n])
        return least_lo, least_hi

    def attend_far(least_lo, least_hi):
        for rows, first_block, least in ((lo, b0 - 2, least_lo), (hi, b0 - 1, least_hi)):
            def body(state, rows=rows):
                j, _ = state
                return j - 1, attend(j, rows, None, False)

            lax.while_loop(_more_blocks, body, (first_block, least))
        _store_heads(acc_ref, a_ref)

    half = SB_WIDTH // 2

    def project_q(c):
        def piece():
            q = _proj_cols(xb_ref[...], w_in_ref, Q_COLS + c, half) * Q_SCALE
            stage_q[:, c:c + half] = q.astype(BF16)
        return piece

    def project_kv(out_ref, first_row, c):
        def piece():
            yt = _dot_nt(w_kvt_ref[first_row + c:first_row + c + half, :], xb_ref[...])
            out_ref[0, c:c + half, :] = yt
            stage_kvt[first_row + c:first_row + c + half, :] = yt.astype(BF16)
        return piece

    u_halves = []

    def project_u0():
        u_halves.append(_proj_cols(xb_ref[...], w_in_ref, U_COLS, half))

    def project_u1_pool():
        u_halves.append(_proj_cols(xb_ref[...], w_in_ref, U_COLS + half, half))
        pooled = _pool_mix(jnp.concatenate(u_halves, axis=1), hist_ref[0], pool_w_ref, pool_scale_ref, u_ext,
                           lax.rem(s, n_t), tm=tm, start_pos=0)
        for g, y in enumerate(pooled):
            p_ref[0, :, g * POOL_GC:(g + 1) * POOL_GC] = y
        pool_ref[0] = _pool_state(u_ext, tm)

    @pl.when(s < n_tiles)
    def _():
        take_stage()
        xb_ref[...] = x_ref[0].astype(BF16)
        pieces = [project_q(0), project_q(half),
                  project_kv(k_ref, 0, 0), project_kv(k_ref, 0, half),
                  project_kv(v_ref, SB_WIDTH, 0), project_kv(v_ref, SB_WIDTH, half),
                  project_u0, project_u1_pool]
        attend_far(*attend_near([f for piece in pieces for f in (piece, None)]))

    @pl.when(s == n_tiles)
    def _():
        take_stage()
        attend_far(*attend_near([]))


def _proj_attn_prompt(x, hist, w_in, pool_w, pool_scale2, after, ffn_weights, n_conv):
    B, T, _ = x.shape
    tm = 2 * ATTN_BLOCK
    assert T % tm == 0 and T >= POOL_HIST
    n_t = T // tm
    n_tiles = B * n_t
    assert n_conv <= n_tiles and all(w.shape[0] % (n_conv * 16) == 0 for w in ffn_weights)
    chunked = [pl.BlockSpec((w.shape[0] // n_conv, w.shape[1]), lambda s: (jnp.minimum(s, n_conv - 1), 0))
               for w in ffn_weights]
    kern = functools.partial(_proj_attn_kernel, tm=tm, n_t=n_t, n_tiles=n_tiles, n_conv=n_conv)
    proj = lambda s: jnp.minimum(s, n_tiles - 1)
    attn = lambda s: jnp.maximum(s - 1, 0)
    const = lambda *shape: pl.BlockSpec(shape, lambda s: (0,) * len(shape), pipeline_mode=pl.Buffered(1))
    proj_rows = lambda width: pl.BlockSpec((1, tm, width), lambda s: (proj(s) // n_t, proj(s) % n_t, 0))
    proj_cols = pl.BlockSpec((1, SB_WIDTH, tm), lambda s: (proj(s) // n_t, 0, proj(s) % n_t))
    return pl.pallas_call(
        kern,
        grid=(n_tiles + 1,),
        in_specs=[
            proj_rows(D_MODEL),
            pl.BlockSpec((1, HIST_ROWS, POOL_WIDTH), lambda s: (proj(s) // n_t, 0, 0)),
            const(D_MODEL, 3 * SB_WIDTH + POOL_WIDTH),
            const(len(POOL_WINDOWS), POOL_GC, POOL_GC),
            const(1, POOL_WIDTH),
            pl.BlockSpec(memory_space=pl.ANY),
        ] + chunked,
        out_specs=[
            proj_cols,
            proj_cols,
            proj_rows(POOL_WIDTH),
            pl.BlockSpec((1, POOL_HIST, POOL_WIDTH), lambda s: (proj(s) // n_t, 0, 0)),
            pl.BlockSpec((1, tm, SB_WIDTH), lambda s: (attn(s) // n_t, attn(s) % n_t, 0)),
        ] + chunked,
        out_shape=[
            jax.ShapeDtypeStruct((B, SB_WIDTH, T), F32),
            jax.ShapeDtypeStruct((B, SB_WIDTH, T), F32),
            jax.ShapeDtypeStruct((B, T, POOL_WIDTH), BF16),
            jax.ShapeDtypeStruct((B, POOL_HIST, POOL_WIDTH), F32),
            jax.ShapeDtypeStruct((B, T, SB_WIDTH), BF16),
        ] + [jax.ShapeDtypeStruct(w.shape, BF16) for w in ffn_weights],
        scratch_shapes=[
            pltpu.VMEM(w_in.shape, BF16),
            pltpu.VMEM((2 * SB_WIDTH, D_MODEL), BF16),
            pltpu.VMEM((HIST_ROWS + tm, POOL_WIDTH), F32),
            pltpu.VMEM((tm, D_MODEL), BF16),
            pltpu.VMEM((tm, SB_WIDTH), BF16),
            pltpu.VMEM((2 * SB_WIDTH, tm), BF16),
            pltpu.VMEM((2 * SB_WIDTH, T), BF16),
        ] + _attn_scratch(tm),
        compiler_params=pltpu.CompilerParams(
            dimension_semantics=("arbitrary",), vmem_limit_bytes=VMEM_LIMIT),
        name="proj_attn_prompt",
    )(x, hist, w_in, pool_w, pool_scale2, after, *ffn_weights)


def _proj_attn_sample_kernel(x_ref, hist_ref, w_in_f32_ref, pool_w_ref, pool_scale_ref,
                             ck_ref, cv_ref, ck_hbm, cv_hbm,
                             k_ref, v_ref, p_ref, pool_ref, a_ref,
                             w_in_ref, q_all, kv_new, u_all, u_ext, ck_buf, cv_buf, qm_ref, decay_ref, acc_ref,
                             *, t_new, past, blk):
    b = pl.program_id(0)
    rows = pl.ds(pl.multiple_of(b * t_new, t_new), t_new)

    @pl.when(b == 0)
    def _():
        w_in_ref[...] = w_in_f32_ref[...].astype(BF16)
        u_ext[...] = jnp.zeros_like(u_ext)
        x = x_ref[...].astype(BF16)
        q_all[...] = (_proj_cols(x, w_in_ref, Q_COLS, SB_WIDTH) * Q_SCALE).astype(BF16)
        k = _proj_cols(x, w_in_ref, K_COLS, SB_WIDTH)
        k_ref[...] = k
        kv_new[:, 0:SB_WIDTH] = k.astype(BF16)
        v = _proj_cols(x, w_in_ref, V_COLS, SB_WIDTH)
        v_ref[...] = v
        kv_new[:, SB_WIDTH:2 * SB_WIDTH] = v.astype(BF16)
        u_all[...] = _proj_cols(x, w_in_ref, U_COLS, POOL_WIDTH)

    pooled = _pool_mix(u_all[rows, :], hist_ref[0], pool_w_ref, pool_scale_ref, u_ext, 0, tm=t_new, start_pos=past)
    for g, y in enumerate(pooled):
        p_ref[rows, g * POOL_GC:(g + 1) * POOL_GC] = y
    pool_ref[b] = _pool_state(u_ext, t_new)

    _store_masked_queries(q_all[rows, :], qm_ref)
    new_kv = lambda p: (kv_new[rows, p * LANES:(p + 1) * LANES],
                        kv_new[rows, SB_WIDTH + p * LANES:SB_WIDTH + (p + 1) * LANES])
    _sb_all_heads(qm_ref, new_kv, decay_ref, acc_ref, _strict_upper(t_new), _causal_mask(t_new), True)
    upper = _strict_upper(blk)
    recent = lambda p: (ck_ref[0, p * LANES:(p + 1) * LANES, :].astype(BF16),
                        cv_ref[0, p * LANES:(p + 1) * LANES, :].astype(BF16))
    least = _sb_all_heads(qm_ref, recent, decay_ref, acc_ref, upper, None, False, key_minor=True)

    def body(state):
        j, _ = state
        start = pl.multiple_of(j * blk, blk)
        pltpu.sync_copy(ck_hbm.at[b, :, pl.ds(start, blk)], ck_buf)
        pltpu.sync_copy(cv_hbm.at[b, :, pl.ds(start, blk)], cv_buf)
        older = lambda p: (ck_buf[p * LANES:(p + 1) * LANES, :].astype(BF16),
                           cv_buf[p * LANES:(p + 1) * LANES, :].astype(BF16))
        return j - 1, _sb_all_heads(qm_ref, older, decay_ref, acc_ref, upper, None, False, key_minor=True)

    lax.while_loop(_more_blocks, body, (past // blk - 2, least))
    for p in range(HEAD_PAIRS):
        a_ref[rows, p * LANES:(p + 1) * LANES] = acc_ref[p].astype(BF16)


def _proj_attn_sample(x, hist, w_in, pool_w, pool_scale2, cache_kt, cache_vt):
    B, t_new, _ = x.shape
    past = cache_kt.shape[2]
    blk = ATTN_BLOCK
    assert past % blk == 0 and t_new >= HIST_ROWS
    n = B * t_new
    kern = functools.partial(_proj_attn_sample_kernel, t_new=t_new, past=past, blk=blk)
    const = lambda *shape: pl.BlockSpec(shape, lambda b: (0,) * len(shape), pipeline_mode=pl.Buffered(1))
    whole = lambda *shape: pl.BlockSpec(shape, lambda b: (0,) * len(shape))
    recent = pl.BlockSpec((1, SB_WIDTH, blk), lambda b: (b, 0, past // blk - 1))
    k, v, p, pool, a = pl.pallas_call(
        kern,
        grid=(B,),
        in_specs=[
            const(n, D_MODEL),
            pl.BlockSpec((1, HIST_ROWS, POOL_WIDTH), lambda b: (b, 0, 0)),
            const(D_MODEL, 3 * SB_WIDTH + POOL_WIDTH),
            const(len(POOL_WINDOWS), POOL_GC, POOL_GC),
            const(1, POOL_WIDTH),
            recent, recent,
            pl.BlockSpec(memory_space=pl.ANY), pl.BlockSpec(memory_space=pl.ANY),
        ],
        out_specs=[whole(n, SB_WIDTH), whole(n, SB_WIDTH), whole(n, POOL_WIDTH),
                   whole(B, POOL_HIST, POOL_WIDTH), whole(n, SB_WIDTH)],
        out_shape=[
            jax.ShapeDtypeStruct((n, SB_WIDTH), F32),
            jax.ShapeDtypeStruct((n, SB_WIDTH), F32),
            jax.ShapeDtypeStruct((n, POOL_WIDTH), BF16),
            jax.ShapeDtypeStruct((B, POOL_HIST, POOL_WIDTH), F32),
            jax.ShapeDtypeStruct((n, SB_WIDTH), BF16),
        ],
        scratch_shapes=[
            pltpu.VMEM(w_in.shape, BF16),
            pltpu.VMEM((n, SB_WIDTH), BF16),
            pltpu.VMEM((n, 2 * SB_WIDTH), BF16),
            pltpu.VMEM((n, POOL_WIDTH), F32),
            pltpu.VMEM((HIST_ROWS + t_new, POOL_WIDTH), F32),
            pltpu.VMEM((SB_WIDTH, blk), F32),
            pltpu.VMEM((SB_WIDTH, blk), F32),
        ] + _attn_scratch(t_new),
        compiler_params=pltpu.CompilerParams(
            dimension_semantics=("arbitrary",), vmem_limit_bytes=VMEM_LIMIT),
        name="proj_attn_sample",
    )(x.reshape(n, D_MODEL), hist, w_in, pool_w, pool_scale2, cache_kt, cache_vt, cache_kt, cache_vt)
    return k, v, p, pool, a


def _layer_norm(x, g, b):
    mu = jnp.mean(x, axis=-1, keepdims=True)
    xc = x - mu
    var = jnp.mean(xc * xc, axis=-1, keepdims=True)
    return xc * lax.rsqrt(var + LN_EPS) * g + b


def _ffn_rows(x_ref, a_ref, p_ref, wo_ref, g1, b1, wg_ref, wu_ref, wd_ref, acc_ref, x1b_ref, filler):
    m = x_ref.shape[0]
    halves = tuple(slice(i * m // FFN_PARTS, (i + 1) * m // FFN_PARTS) for i in range(FFN_PARTS))
    chunks = [(sum(FF_CHUNKS[:i]), w) for i, w in enumerate(FF_CHUNKS)]

    def swiglu_chunk(x1b, chunk):
        c, w = chunk
        gate = _dot(x1b, wg_ref[:, c:c + w])
        up = _dot(x1b, wu_ref[:, c:c + w])
        hmid = (gate * jax.nn.sigmoid(gate) * up).astype(BF16)
        return _dot(hmid, wd_ref[c:c + w, :])

    mix = [_dot(a_ref[r, :], wo_ref[0:SB_WIDTH, :]) + _dot(p_ref[r, :], wo_ref[SB_WIDTH:, :]) for r in halves]
    for r, mx in zip(halves, mix):
        x1 = _layer_norm(ALPHA * x_ref[r, :] + mx, g1, b1)
        x1b_ref[r, :] = x1.astype(BF16)
        acc_ref[r, :] = ALPHA * x1 + swiglu_chunk(x1b_ref[r, :], chunks[0])
    total = None
    for j, c in enumerate(chunks[1:]):
        anchor = filler(j, len(chunks) - 1)
        x1b_ref[0:anchor.shape[0], 0:anchor.shape[1]] += anchor.astype(BF16)
        total = acc_ref[...] + swiglu_chunk(x1b_ref[...], c)
        if c is not chunks[-1]:
            acc_ref[...] = total
    return total


def _out_ffn_kernel(xp_ref, ap_ref, pp_ref, xs_ref, as_ref, ps_ref, wo_s, wg_s, wu_s, wd_s,
                    g1_ref, b1_ref, g2_ref, b2_ref, yp_ref, ys_ref,
                    accp_ref, accs_ref, pre_ref, x1b_ref, *, n_tiles):
    i = pl.program_id(0)

    @pl.when(i == 0)
    def _():
        pre_ref[...] = jnp.zeros_like(pre_ref)

    def finish_previous_tile(j, n):
        rows = yp_ref.shape[0] // n
        r = slice(j * rows, (j + 1) * rows)
        y = _layer_norm(pre_ref[r, :], g2_ref[...], b2_ref[...])
        yp_ref[r, :] = y
        top = jnp.max(y.reshape(rows // 16, 16, D_MODEL), axis=0)
        top = functools.reduce(jnp.maximum, [top[:, c:c + LANES] for c in range(0, D_MODEL, LANES)])
        return jnp.where(top > F32_MAX, top, 0.0)

    def rows(x_ref, a_ref, p_ref, acc_ref):
        m = x_ref.shape[0]
        return _ffn_rows(x_ref, a_ref, p_ref, wo_s, g1_ref[...], b1_ref[...], wg_s, wu_s, wd_s, acc_ref,
                         x1b_ref.at[0:m, :], finish_previous_tile)

    @pl.when(i < n_tiles)
    def _():
        pre_ref[...] = rows(xp_ref, ap_ref, pp_ref, accp_ref)

    @pl.when(i == n_tiles)
    def _():
        ys_ref[...] = _layer_norm(rows(xs_ref, as_ref, ps_ref, accs_ref), g2_ref[...], b2_ref[...])


def _out_ffn(xp, ap, pp, xs, a_s, ps, wo_b, wg_b, wu_b, wd_b, g1, b1, g2, b2, *, tm):
    Np, Ns = xp.shape[0], xs.shape[0]
    assert Np % tm == 0 and sum(FF_CHUNKS) == D_FF
    n_tiles = Np // tm
    tile = lambda i: jnp.clip(i, 0, n_tiles - 1)
    row = lambda width: pl.BlockSpec((tm, width), lambda i: (tile(i), 0))
    once = lambda *shape: pl.BlockSpec(shape, lambda i: (0,) * len(shape), pipeline_mode=pl.Buffered(1))
    kern = functools.partial(_out_ffn_kernel, n_tiles=n_tiles)
    return pl.pallas_call(
        kern,
        grid=(n_tiles + 1,),
        in_specs=[
            row(D_MODEL), row(SB_WIDTH), row(POOL_WIDTH),
            once(Ns, D_MODEL), once(Ns, SB_WIDTH), once(Ns, POOL_WIDTH),
            once(*wo_b.shape), once(*wg_b.shape), once(*wu_b.shape), once(*wd_b.shape),
            once(1, D_MODEL), once(1, D_MODEL), once(1, D_MODEL), once(1, D_MODEL),
        ],
        out_specs=[pl.BlockSpec((tm, D_MODEL), lambda i: (tile(i - 1), 0)),
                   pl.BlockSpec((Ns, D_MODEL), lambda i: (0, 0))],
        out_shape=[jax.ShapeDtypeStruct((Np, D_MODEL), F32), jax.ShapeDtypeStruct((Ns, D_MODEL), F32)],
        scratch_shapes=[pltpu.VMEM((tm, D_MODEL), F32), pltpu.VMEM((Ns, D_MODEL), F32),
                        pltpu.VMEM((tm, D_MODEL), F32), pltpu.VMEM((tm, D_MODEL), BF16)],
        compiler_params=pltpu.CompilerParams(
            dimension_semantics=("arbitrary",), vmem_limit_bytes=VMEM_LIMIT),
        name="out_proj_ffn",
    )(xp, ap, pp, xs, a_s, ps, wo_b, wg_b, wu_b, wd_b, g1, b1, g2, b2)


def kernel(x_prompt, x_sample, cache_k, cache_v, state_pool, w_in, pool_w, pool_scale, w_out,
           ln1_g, ln1_b, w_gate, w_up, w_down, ln2_g, ln2_b):
    Bp, Tp, _ = x_prompt.shape
    Bs, Ts, _ = x_sample.shape
    past = cache_k.shape[1]

    row2 = lambda a: a.reshape(1, -1).astype(F32)
    pool_scale2 = row2(pool_scale)

    hist_s = jnp.concatenate([jnp.zeros((Bs, HIST_ROWS - POOL_HIST, POOL_WIDTH), F32), state_pool.astype(F32)], axis=1)
    feature_major = lambda c: jnp.transpose(c, (0, 2, 3, 1)).reshape(Bs, SB_WIDTH, past)
    k_s, v_s, mix_s, pool_s, a_s = _proj_attn_sample(x_sample, hist_s, w_in, pool_w, pool_scale2,
                                                     feature_major(cache_k), feature_major(cache_v))

    hist_p = jnp.zeros((Bp, HIST_ROWS, POOL_WIDTH), F32)
    kt_p, vt_p, mix_p, pool_p, a_p, wo_b, wg_b, wu_b, wd_b = _proj_attn_prompt(
        x_prompt, hist_p, w_in, pool_w, pool_scale2, after=a_s, ffn_weights=(w_out, w_gate, w_up, w_down), n_conv=16)
    position_major = lambda c: jnp.transpose(c.reshape(Bp, SB_HEADS, SB_HEAD_DIM, Tp), (0, 3, 1, 2))

    y_p, y_s = _out_ffn(
        x_prompt.reshape(Bp * Tp, D_MODEL), a_p.reshape(Bp * Tp, SB_WIDTH), mix_p.reshape(Bp * Tp, POOL_WIDTH),
        x_sample.reshape(Bs * Ts, D_MODEL), a_s, mix_s,
        wo_b, wg_b, wu_b, wd_b, row2(ln1_g), row2(ln1_b), row2(ln2_g), row2(ln2_b), tm=512)

    heads = lambda a: a.reshape(Bs, Ts, SB_HEADS, SB_HEAD_DIM)
    return (y_p.reshape(Bp, Tp, D_MODEL), y_s.reshape(Bs, Ts, D_MODEL),
            position_major(kt_p), position_major(vt_p), pool_p, heads(k_s), heads(v_s), pool_s)
```

```python
import functools

import jax
import jax.numpy as jnp
from jax import lax
from jax.experimental import pallas as pl
from jax.experimental.pallas import tpu as pltpu

D_MODEL = 1024
SB_WIDTH = 512
SB_HEADS = 8
SB_HEAD_DIM = 64
POOL_WIDTH = 512
POOL_WINDOWS = (2, 4, 8, 16)
POOL_GC = 128
POOL_HIST = 15
HIST_ROWS = 16
D_FF = 2816
DEPTH = 1
ALPHA = (2 * DEPTH) ** 0.25
LN_EPS = 1e-5
LANES = 128
HEAD_PAIRS = SB_WIDTH // LANES
LOG2E = 1.4426950408889634
Q_SCALE = SB_HEAD_DIM ** -0.5 * LOG2E
SKIP_LOG2 = 160.0
SP_CLAMP = 32.0
ATTN_BLOCK = 256
FF_CHUNKS = (256, 768, 768, 768, 256)
FFN_PARTS = 2
BF16_SUBLANES = 16
VMEM_LIMIT = 56 * 1024 * 1024

F32 = jnp.float32
BF16 = jnp.bfloat16
F32_MAX = 3.4028234663852886e38


def _dot(a, b):
    return jnp.dot(a, b, preferred_element_type=F32)


def _dot_nt(a, b):
    return lax.dot_general(a, b, (((1,), (1,)), ((), ())), preferred_element_type=F32)


Q_COLS, K_COLS, V_COLS, U_COLS = (SB_WIDTH * i for i in range(4))


def _proj_cols(x, w_in_ref, start, width):
    return _dot(x, w_in_ref[:, start:start + width])


def _pool_mix(u, hist, pool_w_ref, pool_scale_ref, u_ext, t, *, tm, start_pos):
    u_ext[0:HIST_ROWS, :] = jnp.where(t == 0, hist, u_ext[tm:tm + HIST_ROWS, :])
    u_ext[HIST_ROWS:HIST_ROWS + tm, :] = u

    pos = start_pos + t * tm + lax.broadcasted_iota(jnp.int32, (tm, 1), 0)
    pooled = []
    for g, w in enumerate(POOL_WINDOWS):
        cols = slice(g * POOL_GC, (g + 1) * POOL_GC)
        s = u_ext[:, cols]
        span = 1
        while span < w:
            s = s + pltpu.roll(s, span, 0)
            span *= 2
        cnt = jnp.minimum(w, pos + 1).astype(F32)
        d = s[HIST_ROWS:, :] / cnt - u[:, cols]
        y = _dot(d.astype(BF16), pool_w_ref[g].astype(BF16)) * pool_scale_ref[:, cols]
        pooled.append(y.astype(BF16))
    return pooled


def _pool_state(u_ext, tm):
    return u_ext[tm + HIST_ROWS - POOL_HIST:tm + HIST_ROWS, :]


def _strict_upper(n):
    r = lax.broadcasted_iota(jnp.int32, (n, n), 0)
    c = lax.broadcasted_iota(jnp.int32, (n, n), 1)
    return (r > c).astype(BF16)


def _causal_mask(n):
    r = lax.broadcasted_iota(jnp.int32, (n, n), 0)
    c = lax.broadcasted_iota(jnp.int32, (n, n), 1)
    return c < r


def _store_masked_queries(q, qm_ref):
    m = q.shape[0]
    lower_half = lax.broadcasted_iota(jnp.int32, (m, LANES), 1) < SB_HEAD_DIM
    for p in range(HEAD_PAIRS):
        q2 = q[:, p * LANES:(p + 1) * LANES]
        zero = jnp.zeros_like(q2)
        qm_ref[p, 0:m, :] = jnp.where(lower_half, q2, zero)
        qm_ref[p, m:2 * m, :] = jnp.where(lower_half, zero, q2)


def _sb_all_heads(qm_ref, kv_block, decay_ref, acc_ref, upper, mask, first, key_minor=False, extra_decay=None,
                  fillers=(), rows=None):
    fillers = list(fillers)
    tile_rows = qm_ref.shape[1] // 2
    r0, r1 = (0, tile_rows) if rows is None else rows
    m = r1 - r0
    lower_half = lax.broadcasted_iota(jnp.int32, (m, LANES), 1) < SB_HEAD_DIM
    mask2 = None if mask is None else jnp.concatenate([mask, mask], axis=0)
    kv = [kv_block(p) for p in range(HEAD_PAIRS)]

    def queries(p):
        if rows is None:
            return qm_ref[p]
        return jnp.concatenate([qm_ref[p, r0:r1, :], qm_ref[p, tile_rows + r0:tile_rows + r1, :]], axis=0)

    logit, sps, rowsums = [], [], []
    for p in range(HEAD_PAIRS):
        z = (_dot if key_minor else _dot_nt)(queries(p), kv[p][0])
        sp = jnp.maximum(z, jnp.log(1.0 + jnp.exp2(jnp.minimum(z, SP_CLAMP))) * LOG2E)
        spm = sp if mask2 is None else jnp.where(mask2, sp, 0.0)
        sps.append(spm.astype(BF16))
        logit.append(z - sp)
        rowsums.append(jnp.sum(spm, axis=1, keepdims=True))
        filler = fillers.pop(0) if fillers else None
        if filler is not None:
            filler()

    later = _dot(jnp.concatenate(sps, axis=0), upper)

    least = None
    for p in range(HEAD_PAIRS):
        a = jnp.exp2(logit[p] - later[2 * m * p:2 * m * (p + 1)])
        if mask2 is not None:
            a = jnp.where(mask2, a, 0.0)
        pv = (_dot_nt if key_minor else _dot)(a.astype(BF16), kv[p][1])
        merged = jnp.where(lower_half, pv[0:m], pv[m:2 * m])
        rowsum = jnp.where(lower_half, rowsums[p][0:m], rowsums[p][m:2 * m])
        if first:
            decay = rowsum
            acc_ref[p, r0:r1, :] = merged
        else:
            prev = decay_ref[p, r0:r1, :]
            if extra_decay is not None:
                prev = prev + extra_decay
            decay = prev + rowsum
            acc_ref[p, r0:r1, :] += merged * jnp.exp2(-prev)
        decay_ref[p, r0:r1, :] = decay
        least = decay if least is None else jnp.minimum(least, decay)
    return jnp.min(least)


def _more_blocks(state):
    j, least = state
    return jnp.logical_and(j >= 0, least < SKIP_LOG2)


def _store_heads(acc_ref, o_ref):
    for p in range(HEAD_PAIRS):
        o_ref[0, :, p * LANES:(p + 1) * LANES] = acc_ref[p].astype(BF16)


def _attn_scratch(m):
    return [pltpu.VMEM((HEAD_PAIRS, 2 * m, LANES), BF16), pltpu.VMEM((HEAD_PAIRS, m, LANES), F32),
            pltpu.VMEM((HEAD_PAIRS, m, LANES), F32)]


def _proj_attn_kernel(x_ref, hist_ref, w_in_f32_ref, pool_w_ref, pool_scale_ref,
                      wo_c, wg_c, wu_c, wd_c,
                      k_ref, v_ref, p_ref, pool_ref, a_ref, wo_b, wg_b, wu_b, wd_b,
                      w_in_ref, w_kvt_ref, u_ext, xb_ref, stage_q, stage_kvt, kvt_all, qm_ref, decay_ref, acc_ref,
                      *, tm, n_t, n_tiles, n_conv):
    s = pl.program_id(0)

    @pl.when(s < n_conv)
    def _():
        for c_ref, b_ref in ((wo_c, wo_b), (wg_c, wg_b), (wu_c, wu_b), (wd_c, wd_b)):
            b_ref[...] = c_ref[...].astype(BF16)

    @pl.when(s == 0)
    def _():
        w_in_ref[...] = w_in_f32_ref[...].astype(BF16)
        w_kvt_ref[...] = w_in_f32_ref[:, K_COLS:U_COLS].T.astype(BF16)
        stage_q[...] = jnp.zeros_like(stage_q)
        stage_kvt[...] = jnp.zeros_like(stage_kvt)
        u_ext[...] = jnp.zeros_like(u_ext)

    blk = ATTN_BLOCK
    upper = _strict_upper(blk)
    qi = lax.rem(jnp.maximum(s - 1, 0), n_t)
    b0 = (tm // blk) * qi
    lo, hi = (0, blk), (blk, tm)

    def kv_block(j):
        start = pl.multiple_of(j * blk, blk)
        return lambda p: (kvt_all[p * LANES:(p + 1) * LANES, pl.ds(start, blk)],
                          kvt_all[SB_WIDTH + p * LANES:SB_WIDTH + (p + 1) * LANES, pl.ds(start, blk)])

    def take_stage():
        kvt_all[:, pl.ds(pl.multiple_of(qi * tm, tm), tm)] = stage_kvt[...]
        _store_masked_queries(stage_q[...], qm_ref)

    def attend(block, rows, mask, first, fillers=(), extra_decay=None):
        return _sb_all_heads(qm_ref, kv_block(block), decay_ref, acc_ref, upper, mask, first, key_minor=True,
                             extra_decay=extra_decay, fillers=fillers, rows=rows)

    def attend_near(fillers):
        n = HEAD_PAIRS
        causal = _causal_mask(blk)
        attend(b0 + 1, hi, causal, True, fillers[0:n])
        attend(b0, lo, causal, True, fillers[n:2 * n])
        least_hi = attend(b0, hi, None, False, fillers[2 * n:3 * n])
        void = jnp.where(qi == 0, 2.0 * SKIP_LOG2, 0.0)
        least_lo = attend(jnp.maximum(b0 - 1, 0), lo, None, False, fillers[3 * n:4 * n], extra_decay=void)
        return least_lo, least_hi

    def attend_far(least_lo, least_hi):
        for rows, first_block, least in ((lo, b0 - 2, least_lo), (hi, b0 - 1, least_hi)):
            def body(state, rows=rows):
                j, _ = state
                return j - 1, attend(j, rows, None, False)

            lax.while_loop(_more_blocks, body, (first_block, least))
        _store_heads(acc_ref, a_ref)

    half = SB_WIDTH // 2

    def project_q(c):
        def piece():
            q = _proj_cols(xb_ref[...], w_in_ref, Q_COLS + c, half) * Q_SCALE
            stage_q[:, c:c + half] = q.astype(BF16)
        return piece

    def project_kv(out_ref, first_row, c):
        def piece():
            yt = _dot_nt(w_kvt_ref[first_row + c:first_row + c + half, :], xb_ref[...])
            out_ref[0, c:c + half, :] = yt
            stage_kvt[first_row + c:first_row + c + half, :] = yt.astype(BF16)
        return piece

    u_halves = []

    def project_u0():
        u_halves.append(_proj_cols(xb_ref[...], w_in_ref, U_COLS, half))

    def project_u1_pool():
        u_halves.append(_proj_cols(xb_ref[...], w_in_ref, U_COLS + half, half))
        pooled = _pool_mix(jnp.concatenate(u_halves, axis=1), hist_ref[0], pool_w_ref, pool_scale_ref, u_ext,
                           lax.rem(s, n_t), tm=tm, start_pos=0)
        for g, y in enumerate(pooled):
            p_ref[0, :, g * POOL_GC:(g + 1) * POOL_GC] = y
        pool_ref[0] = _pool_state(u_ext, tm)

    @pl.when(s < n_tiles)
    def _():
        take_stage()
        xb_ref[...] = x_ref[0].astype(BF16)
        pieces = [project_q(0), project_q(half),
                  project_kv(k_ref, 0, 0), project_kv(k_ref, 0, half),
                  project_kv(v_ref, SB_WIDTH, 0), project_kv(v_ref, SB_WIDTH, half),
                  project_u0, project_u1_pool]
        attend_far(*attend_near([f for piece in pieces for f in (piece, None)]))

    @pl.when(s == n_tiles)
    def _():
        take_stage()
        attend_far(*attend_near([]))


def _proj_attn_prompt(x, hist, w_in, pool_w, pool_scale2, ffn_weights, n_conv):
    B, T, _ = x.shape
    tm = 2 * ATTN_BLOCK
    assert T % tm == 0 and T >= POOL_HIST
    n_t = T // tm
    n_tiles = B * n_t
    assert n_conv <= n_tiles and all(w.shape[0] % (n_conv * BF16_SUBLANES) == 0 for w in ffn_weights)
    chunked = [pl.BlockSpec((w.shape[0] // n_conv, w.shape[1]), lambda s: (jnp.minimum(s, n_conv - 1), 0))
               for w in ffn_weights]
    kern = functools.partial(_proj_attn_kernel, tm=tm, n_t=n_t, n_tiles=n_tiles, n_conv=n_conv)
    proj = lambda s: jnp.minimum(s, n_tiles - 1)
    attn = lambda s: jnp.maximum(s - 1, 0)
    const = lambda *shape: pl.BlockSpec(shape, lambda s: (0,) * len(shape), pipeline_mode=pl.Buffered(1))
    proj_rows = lambda width: pl.BlockSpec((1, tm, width), lambda s: (proj(s) // n_t, proj(s) % n_t, 0))
    proj_cols = pl.BlockSpec((1, SB_WIDTH, tm), lambda s: (proj(s) // n_t, 0, proj(s) % n_t))
    return pl.pallas_call(
        kern,
        grid=(n_tiles + 1,),
        in_specs=[
            proj_rows(D_MODEL),
            pl.BlockSpec((1, HIST_ROWS, POOL_WIDTH), lambda s: (proj(s) // n_t, 0, 0)),
            const(D_MODEL, 3 * SB_WIDTH + POOL_WIDTH),
            const(len(POOL_WINDOWS), POOL_GC, POOL_GC),
            const(1, POOL_WIDTH),
        ] + chunked,
        out_specs=[
            proj_cols,
            proj_cols,
            proj_rows(POOL_WIDTH),
            pl.BlockSpec((1, POOL_HIST, POOL_WIDTH), lambda s: (proj(s) // n_t, 0, 0)),
            pl.BlockSpec((1, tm, SB_WIDTH), lambda s: (attn(s) // n_t, attn(s) % n_t, 0)),
        ] + chunked,
        out_shape=[
            jax.ShapeDtypeStruct((B, SB_WIDTH, T), F32),
            jax.ShapeDtypeStruct((B, SB_WIDTH, T), F32),
            jax.ShapeDtypeStruct((B, T, POOL_WIDTH), BF16),
            jax.ShapeDtypeStruct((B, POOL_HIST, POOL_WIDTH), F32),
            jax.ShapeDtypeStruct((B, T, SB_WIDTH), BF16),
        ] + [jax.ShapeDtypeStruct(w.shape, BF16) for w in ffn_weights],
        scratch_shapes=[
            pltpu.VMEM(w_in.shape, BF16),
            pltpu.VMEM((2 * SB_WIDTH, D_MODEL), BF16),
            pltpu.VMEM((HIST_ROWS + tm, POOL_WIDTH), F32),
            pltpu.VMEM((tm, D_MODEL), BF16),
            pltpu.VMEM((tm, SB_WIDTH), BF16),
            pltpu.VMEM((2 * SB_WIDTH, tm), BF16),
            pltpu.VMEM((2 * SB_WIDTH, T), BF16),
        ] + _attn_scratch(tm),
        compiler_params=pltpu.CompilerParams(
            dimension_semantics=("arbitrary",), vmem_limit_bytes=VMEM_LIMIT),
        name="proj_attn_prompt",
    )(x, hist, w_in, pool_w, pool_scale2, *ffn_weights)


def _proj_attn_sample_kernel(x_ref, hist_ref, w_in_f32_ref, pool_w_ref, pool_scale_ref,
                             ck_ref, cv_ref, ck_hbm, cv_hbm,
                             k_ref, v_ref, p_ref, pool_ref, a_ref,
                             w_in_ref, q_all, kv_new, u_all, u_ext, ck_buf, cv_buf, qm_ref, decay_ref, acc_ref,
                             *, t_new, past, blk):
    b = pl.program_id(0)
    rows = pl.ds(pl.multiple_of(b * t_new, t_new), t_new)

    @pl.when(b == 0)
    def _():
        w_in_ref[...] = w_in_f32_ref[...].astype(BF16)
        u_ext[...] = jnp.zeros_like(u_ext)
        x = x_ref[...].astype(BF16)
        q_all[...] = (_proj_cols(x, w_in_ref, Q_COLS, SB_WIDTH) * Q_SCALE).astype(BF16)
        k = _proj_cols(x, w_in_ref, K_COLS, SB_WIDTH)
        k_ref[...] = k
        kv_new[:, 0:SB_WIDTH] = k.astype(BF16)
        v = _proj_cols(x, w_in_ref, V_COLS, SB_WIDTH)
        v_ref[...] = v
        kv_new[:, SB_WIDTH:2 * SB_WIDTH] = v.astype(BF16)
        u_all[...] = _proj_cols(x, w_in_ref, U_COLS, POOL_WIDTH)

    pooled = _pool_mix(u_all[rows, :], hist_ref[0], pool_w_ref, pool_scale_ref, u_ext, 0, tm=t_new, start_pos=past)
    for g, y in enumerate(pooled):
        p_ref[rows, g * POOL_GC:(g + 1) * POOL_GC] = y
    pool_ref[b] = _pool_state(u_ext, t_new)

    _store_masked_queries(q_all[rows, :], qm_ref)
    new_kv = lambda p: (kv_new[rows, p * LANES:(p + 1) * LANES],
                        kv_new[rows, SB_WIDTH + p * LANES:SB_WIDTH + (p + 1) * LANES])
    _sb_all_heads(qm_ref, new_kv, decay_ref, acc_ref, _strict_upper(t_new), _causal_mask(t_new), True)
    upper = _strict_upper(blk)
    recent = lambda p: (ck_ref[0, p * LANES:(p + 1) * LANES, :].astype(BF16),
                        cv_ref[0, p * LANES:(p + 1) * LANES, :].astype(BF16))
    least = _sb_all_heads(qm_ref, recent, decay_ref, acc_ref, upper, None, False, key_minor=True)

    def body(state):
        j, _ = state
        start = pl.multiple_of(j * blk, blk)
        pltpu.sync_copy(ck_hbm.at[b, :, pl.ds(start, blk)], ck_buf)
        pltpu.sync_copy(cv_hbm.at[b, :, pl.ds(start, blk)], cv_buf)
        older = lambda p: (ck_buf[p * LANES:(p + 1) * LANES, :].astype(BF16),
                           cv_buf[p * LANES:(p + 1) * LANES, :].astype(BF16))
        return j - 1, _sb_all_heads(qm_ref, older, decay_ref, acc_ref, upper, None, False, key_minor=True)

    lax.while_loop(_more_blocks, body, (past // blk - 2, least))
    for p in range(HEAD_PAIRS):
        a_ref[rows, p * LANES:(p + 1) * LANES] = acc_ref[p].astype(BF16)


def _proj_attn_sample(x, hist, w_in, pool_w, pool_scale2, cache_kt, cache_vt):
    B, t_new, _ = x.shape
    past = cache_kt.shape[2]
    blk = ATTN_BLOCK
    assert past % blk == 0 and t_new >= HIST_ROWS
    n = B * t_new
    kern = functools.partial(_proj_attn_sample_kernel, t_new=t_new, past=past, blk=blk)
    const = lambda *shape: pl.BlockSpec(shape, lambda b: (0,) * len(shape), pipeline_mode=pl.Buffered(1))
    whole = lambda *shape: pl.BlockSpec(shape, lambda b: (0,) * len(shape))
    recent = pl.BlockSpec((1, SB_WIDTH, blk), lambda b: (b, 0, past // blk - 1))
    k, v, p, pool, a = pl.pallas_call(
        kern,
        grid=(B,),
        in_specs=[
            const(n, D_MODEL),
            pl.BlockSpec((1, HIST_ROWS, POOL_WIDTH), lambda b: (b, 0, 0)),
            const(D_MODEL, 3 * SB_WIDTH + POOL_WIDTH),
            const(len(POOL_WINDOWS), POOL_GC, POOL_GC),
            const(1, POOL_WIDTH),
            recent, recent,
            pl.BlockSpec(memory_space=pl.ANY), pl.BlockSpec(memory_space=pl.ANY),
        ],
        out_specs=[whole(n, SB_WIDTH), whole(n, SB_WIDTH), whole(n, POOL_WIDTH),
                   whole(B, POOL_HIST, POOL_WIDTH), whole(n, SB_WIDTH)],
        out_shape=[
            jax.ShapeDtypeStruct((n, SB_WIDTH), F32),
            jax.ShapeDtypeStruct((n, SB_WIDTH), F32),
            jax.ShapeDtypeStruct((n, POOL_WIDTH), BF16),
            jax.ShapeDtypeStruct((B, POOL_HIST, POOL_WIDTH), F32),
            jax.ShapeDtypeStruct((n, SB_WIDTH), BF16),
        ],
        scratch_shapes=[
            pltpu.VMEM(w_in.shape, BF16),
            pltpu.VMEM((n, SB_WIDTH), BF16),
            pltpu.VMEM((n, 2 * SB_WIDTH), BF16),
            pltpu.VMEM((n, POOL_WIDTH), F32),
            pltpu.VMEM((HIST_ROWS + t_new, POOL_WIDTH), F32),
            pltpu.VMEM((SB_WIDTH, blk), F32),
            pltpu.VMEM((SB_WIDTH, blk), F32),
        ] + _attn_scratch(t_new),
        compiler_params=pltpu.CompilerParams(
            dimension_semantics=("arbitrary",), vmem_limit_bytes=VMEM_LIMIT),
        name="proj_attn_sample",
    )(x.reshape(n, D_MODEL), hist, w_in, pool_w, pool_scale2, cache_kt, cache_vt, cache_kt, cache_vt)
    return k, v, p, pool, a


def _layer_norm(x, g, b):
    mu = jnp.mean(x, axis=-1, keepdims=True)
    xc = x - mu
    var = jnp.mean(xc * xc, axis=-1, keepdims=True)
    return xc * lax.rsqrt(var + LN_EPS) * g + b


def _ffn_rows(x_ref, a_ref, p_ref, wo_ref, g1, b1, wg_ref, wu_ref, wd_ref, acc_ref, x1b_ref, filler):
    m = x_ref.shape[0]
    halves = tuple(slice(i * m // FFN_PARTS, (i + 1) * m // FFN_PARTS) for i in range(FFN_PARTS))
    chunks = [(sum(FF_CHUNKS[:i]), w) for i, w in enumerate(FF_CHUNKS)]

    def swiglu_chunk(x1b, chunk):
        c, w = chunk
        gate = _dot(x1b, wg_ref[:, c:c + w])
        up = _dot(x1b, wu_ref[:, c:c + w])
        hmid = (gate * jax.nn.sigmoid(gate) * up).astype(BF16)
        return _dot(hmid, wd_ref[c:c + w, :])

    mix = [_dot(a_ref[r, :], wo_ref[0:SB_WIDTH, :]) + _dot(p_ref[r, :], wo_ref[SB_WIDTH:, :]) for r in halves]
    for r, mx in zip(halves, mix):
        x1 = _layer_norm(ALPHA * x_ref[r, :] + mx, g1, b1)
        x1b_ref[r, :] = x1.astype(BF16)
        acc_ref[r, :] = ALPHA * x1 + swiglu_chunk(x1b_ref[r, :], chunks[0])
    total = None
    for j, c in enumerate(chunks[1:]):
        anchor = filler(j, len(chunks) - 1)
        x1b_ref[0:anchor.shape[0], 0:anchor.shape[1]] += anchor.astype(BF16)
        total = acc_ref[...] + swiglu_chunk(x1b_ref[...], c)
        if c is not chunks[-1]:
            acc_ref[...] = total
    return total


def _out_ffn_kernel(xp_ref, ap_ref, pp_ref, xs_ref, as_ref, ps_ref, wo_s, wg_s, wu_s, wd_s,
                    g1_ref, b1_ref, g2_ref, b2_ref, yp_ref, ys_ref,
                    accp_ref, accs_ref, pre_ref, x1b_ref, *, n_tiles):
    i = pl.program_id(0)

    @pl.when(i == 0)
    def _():
        pre_ref[...] = jnp.zeros_like(pre_ref)

    def finish_previous_tile(j, n):
        rows = yp_ref.shape[0] // n
        r = slice(j * rows, (j + 1) * rows)
        y = _layer_norm(pre_ref[r, :], g2_ref[...], b2_ref[...])
        yp_ref[r, :] = y
        top = jnp.max(y.reshape(rows // BF16_SUBLANES, BF16_SUBLANES, D_MODEL), axis=0)
        top = functools.reduce(jnp.maximum, [top[:, c:c + LANES] for c in range(0, D_MODEL, LANES)])
        return jnp.where(top > F32_MAX, top, 0.0)

    def rows(x_ref, a_ref, p_ref, acc_ref):
        m = x_ref.shape[0]
        return _ffn_rows(x_ref, a_ref, p_ref, wo_s, g1_ref[...], b1_ref[...], wg_s, wu_s, wd_s, acc_ref,
                         x1b_ref.at[0:m, :], finish_previous_tile)

    @pl.when(i < n_tiles)
    def _():
        pre_ref[...] = rows(xp_ref, ap_ref, pp_ref, accp_ref)

    @pl.when(i == n_tiles)
    def _():
        ys_ref[...] = _layer_norm(rows(xs_ref, as_ref, ps_ref, accs_ref), g2_ref[...], b2_ref[...])


def _out_ffn(xp, ap, pp, xs, a_s, ps, wo_b, wg_b, wu_b, wd_b, g1, b1, g2, b2, *, tm):
    Np, Ns = xp.shape[0], xs.shape[0]
    assert Np % tm == 0 and sum(FF_CHUNKS) == D_FF
    n_tiles = Np // tm
    tile = lambda i: jnp.clip(i, 0, n_tiles - 1)
    row = lambda width: pl.BlockSpec((tm, width), lambda i: (tile(i), 0))
    once = lambda *shape: pl.BlockSpec(shape, lambda i: (0,) * len(shape), pipeline_mode=pl.Buffered(1))
    kern = functools.partial(_out_ffn_kernel, n_tiles=n_tiles)
    return pl.pallas_call(
        kern,
        grid=(n_tiles + 1,),
        in_specs=[
            row(D_MODEL), row(SB_WIDTH), row(POOL_WIDTH),
            once(Ns, D_MODEL), once(Ns, SB_WIDTH), once(Ns, POOL_WIDTH),
            once(*wo_b.shape), once(*wg_b.shape), once(*wu_b.shape), once(*wd_b.shape),
            once(1, D_MODEL), once(1, D_MODEL), once(1, D_MODEL), once(1, D_MODEL),
        ],
        out_specs=[pl.BlockSpec((tm, D_MODEL), lambda i: (tile(i - 1), 0)),
                   pl.BlockSpec((Ns, D_MODEL), lambda i: (0, 0))],
        out_shape=[jax.ShapeDtypeStruct((Np, D_MODEL), F32), jax.ShapeDtypeStruct((Ns, D_MODEL), F32)],
        scratch_shapes=[pltpu.VMEM((tm, D_MODEL), F32), pltpu.VMEM((Ns, D_MODEL), F32),
                        pltpu.VMEM((tm, D_MODEL), F32), pltpu.VMEM((tm, D_MODEL), BF16)],
        compiler_params=pltpu.CompilerParams(
            dimension_semantics=("arbitrary",), vmem_limit_bytes=VMEM_LIMIT),
        name="out_proj_ffn",
    )(xp, ap, pp, xs, a_s, ps, wo_b, wg_b, wu_b, wd_b, g1, b1, g2, b2)


def kernel(x_prompt, x_sample, cache_k, cache_v, state_pool, w_in, pool_w, pool_scale, w_out,
           ln1_g, ln1_b, w_gate, w_up, w_down, ln2_g, ln2_b):
    Bp, Tp, _ = x_prompt.shape
    Bs, Ts, _ = x_sample.shape
    past = cache_k.shape[1]

    row2 = lambda a: a.reshape(1, -1).astype(F32)
    pool_scale2 = row2(pool_scale)

    hist_s = jnp.concatenate([jnp.zeros((Bs, HIST_ROWS - POOL_HIST, POOL_WIDTH), F32), state_pool.astype(F32)], axis=1)
    feature_major = lambda c: jnp.transpose(c, (0, 2, 3, 1)).reshape(Bs, SB_WIDTH, past)
    k_s, v_s, mix_s, pool_s, a_s = _proj_attn_sample(x_sample, hist_s, w_in, pool_w, pool_scale2,
                                                     feature_major(cache_k), feature_major(cache_v))

    hist_p = jnp.zeros((Bp, HIST_ROWS, POOL_WIDTH), F32)
    kt_p, vt_p, mix_p, pool_p, a_p, wo_b, wg_b, wu_b, wd_b = _proj_attn_prompt(
        x_prompt, hist_p, w_in, pool_w, pool_scale2, ffn_weights=(w_out, w_gate, w_up, w_down), n_conv=16)
    position_major = lambda c: jnp.transpose(c.reshape(Bp, SB_HEADS, SB_HEAD_DIM, Tp), (0, 3, 1, 2))

    y_p, y_s = _out_ffn(
        x_prompt.reshape(Bp * Tp, D_MODEL), a_p.reshape(Bp * Tp, SB_WIDTH), mix_p.reshape(Bp * Tp, POOL_WIDTH),
        x_sample.reshape(Bs * Ts, D_MODEL), a_s, mix_s,
        wo_b, wg_b, wu_b, wd_b, row2(ln1_g), row2(ln1_b), row2(ln2_g), row2(ln2_b), tm=512)

    heads = lambda a: a.reshape(Bs, Ts, SB_HEADS, SB_HEAD_DIM)
    return (y_p.reshape(Bp, Tp, D_MODEL), y_s.reshape(Bs, Ts, D_MODEL),
            position_major(kt_p), position_major(vt_p), pool_p, heads(k_s), heads(v_s), pool_s)
```

```python
import functools

import jax
import jax.numpy as jnp
from jax import lax
from jax.experimental import pallas as pl
from jax.experimental.pallas import tpu as pltpu

D_MODEL = 1024
SB_WIDTH = 512
SB_HEADS = 8
SB_HEAD_DIM = 64
POOL_WIDTH = 512
POOL_WINDOWS = (2, 4, 8, 16)
POOL_GC = 128
POOL_HIST = 15
HIST_ROWS = 16
D_FF = 2816
DEPTH = 1
ALPHA = (2 * DEPTH) ** 0.25
LN_EPS = 1e-5
LANES = 128
HEAD_PAIRS = SB_WIDTH // LANES
LOG2E = 1.4426950408889634
Q_SCALE = SB_HEAD_DIM ** -0.5 * LOG2E
SKIP_LOG2 = 160.0
SP_CLAMP = 32.0
ATTN_BLOCK = 256
FF_CHUNKS = (256, 768, 768, 768, 256)
FFN_PARTS = 2
BF16_SUBLANES = 16
VMEM_LIMIT = 56 * 1024 * 1024

F32 = jnp.float32
BF16 = jnp.bfloat16
F32_MAX = 3.4028234663852886e38


def _dot(a, b):
    return jnp.dot(a, b, preferred_element_type=F32)


def _dot_nt(a, b):
    return lax.dot_general(a, b, (((1,), (1,)), ((), ())), preferred_element_type=F32)


Q_COLS, K_COLS, V_COLS, U_COLS = (SB_WIDTH * i for i in range(4))


def _proj_cols(x, w_in_ref, start, width):
    return _dot(x, w_in_ref[:, start:start + width])


def _pool_mix(u, hist, pool_w_ref, pool_scale_ref, u_ext, t, *, tm, start_pos):
    u_ext[0:HIST_ROWS, :] = jnp.where(t == 0, hist, u_ext[tm:tm + HIST_ROWS, :])
    u_ext[HIST_ROWS:HIST_ROWS + tm, :] = u

    pos = start_pos + t * tm + lax.broadcasted_iota(jnp.int32, (tm, 1), 0)
    pooled = []
    for g, w in enumerate(POOL_WINDOWS):
        cols = slice(g * POOL_GC, (g + 1) * POOL_GC)
        s = u_ext[:, cols]
        span = 1
        while span < w:
            s = s + pltpu.roll(s, span, 0)
            span *= 2
        cnt = jnp.minimum(w, pos + 1).astype(F32)
        d = s[HIST_ROWS:, :] / cnt - u[:, cols]
        y = _dot(d.astype(BF16), pool_w_ref[g].astype(BF16)) * pool_scale_ref[:, cols]
        pooled.append(y.astype(BF16))
    return pooled


def _pool_state(u_ext, tm):
    return u_ext[tm + HIST_ROWS - POOL_HIST:tm + HIST_ROWS, :]


def _strict_upper(n):
    r = lax.broadcasted_iota(jnp.int32, (n, n), 0)
    c = lax.broadcasted_iota(jnp.int32, (n, n), 1)
    return (r > c).astype(BF16)


def _causal_mask(n):
    r = lax.broadcasted_iota(jnp.int32, (n, n), 0)
    c = lax.broadcasted_iota(jnp.int32, (n, n), 1)
    return c < r


def _store_masked_queries(q, qm_ref):
    m = q.shape[0]
    lower_half = lax.broadcasted_iota(jnp.int32, (m, LANES), 1) < SB_HEAD_DIM
    for p in range(HEAD_PAIRS):
        q2 = q[:, p * LANES:(p + 1) * LANES]
        zero = jnp.zeros_like(q2)
        qm_ref[p, 0:m, :] = jnp.where(lower_half, q2, zero)
        qm_ref[p, m:2 * m, :] = jnp.where(lower_half, zero, q2)


def _sb_all_heads(qm_ref, kv_block, decay_ref, acc_ref, upper, mask, first, key_minor=False, extra_decay=None,
                  fillers=(), rows=None):
    fillers = list(fillers)
    tile_rows = qm_ref.shape[1] // 2
    r0, r1 = (0, tile_rows) if rows is None else rows
    m = r1 - r0
    lower_half = lax.broadcasted_iota(jnp.int32, (m, LANES), 1) < SB_HEAD_DIM
    mask2 = None if mask is None else jnp.concatenate([mask, mask], axis=0)
    kv = [kv_block(p) for p in range(HEAD_PAIRS)]

    def queries(p):
        if rows is None:
            return qm_ref[p]
        return jnp.concatenate([qm_ref[p, r0:r1, :], qm_ref[p, tile_rows + r0:tile_rows + r1, :]], axis=0)

    logit, sps, rowsums = [], [], []
    for p in range(HEAD_PAIRS):
        z = (_dot if key_minor else _dot_nt)(queries(p), kv[p][0])
        sp = jnp.maximum(z, jnp.log(1.0 + jnp.exp2(jnp.minimum(z, SP_CLAMP))) * LOG2E)
        spm = sp if mask2 is None else jnp.where(mask2, sp, 0.0)
        sps.append(spm.astype(BF16))
        logit.append(z - sp)
        rowsums.append(jnp.sum(spm, axis=1, keepdims=True))
        filler = fillers.pop(0) if fillers else None
        if filler is not None:
            filler()

    later = _dot(jnp.concatenate(sps, axis=0), upper)

    least = None
    for p in range(HEAD_PAIRS):
        a = jnp.exp2(logit[p] - later[2 * m * p:2 * m * (p + 1)])
        if mask2 is not None:
            a = jnp.where(mask2, a, 0.0)
        pv = (_dot_nt if key_minor else _dot)(a.astype(BF16), kv[p][1])
        merged = jnp.where(lower_half, pv[0:m], pv[m:2 * m])
        rowsum = jnp.where(lower_half, rowsums[p][0:m], rowsums[p][m:2 * m])
        if first:
            decay = rowsum
            acc_ref[p, r0:r1, :] = merged
        else:
            prev = decay_ref[p, r0:r1, :]
            if extra_decay is not None:
                prev = prev + extra_decay
            decay = prev + rowsum
            acc_ref[p, r0:r1, :] += merged * jnp.exp2(-prev)
        decay_ref[p, r0:r1, :] = decay
        least = decay if least is None else jnp.minimum(least, decay)
    return jnp.min(least)


def _more_blocks(state):
    j, least = state
    return jnp.logical_and(j >= 0, least < SKIP_LOG2)


def _store_heads(acc_ref, o_ref):
    for p in range(HEAD_PAIRS):
        o_ref[0, :, p * LANES:(p + 1) * LANES] = acc_ref[p].astype(BF16)


def _attn_scratch(m):
    return [pltpu.VMEM((HEAD_PAIRS, 2 * m, LANES), BF16), pltpu.VMEM((HEAD_PAIRS, m, LANES), F32),
            pltpu.VMEM((HEAD_PAIRS, m, LANES), F32)]


def _proj_attn_kernel(x_ref, hist_ref, w_in_f32_ref, pool_w_ref, pool_scale_ref,
                      wo_c, wg_c, wu_c, wd_c,
                      k_ref, v_ref, p_ref, pool_ref, a_ref, wo_b, wg_b, wu_b, wd_b,
                      w_in_ref, w_kvt_ref, u_ext, xb_ref, stage_q, stage_kvt, kvt_all, qm_ref, decay_ref, acc_ref,
                      *, tm, n_t, n_tiles, n_conv):
    s = pl.program_id(0)

    @pl.when(s < n_conv)
    def _():
        for c_ref, b_ref in ((wo_c, wo_b), (wg_c, wg_b), (wu_c, wu_b), (wd_c, wd_b)):
            b_ref[...] = c_ref[...].astype(BF16)

    @pl.when(s == 0)
    def _():
        w_in_ref[...] = w_in_f32_ref[...].astype(BF16)
        w_kvt_ref[...] = w_in_f32_ref[:, K_COLS:U_COLS].T.astype(BF16)
        u_ext[...] = jnp.zeros_like(u_ext)

    blk = ATTN_BLOCK
    upper = _strict_upper(blk)
    qi = lax.rem(jnp.maximum(s - 1, 0), n_t)
    b0 = (tm // blk) * qi
    lo, hi = (0, blk), (blk, tm)

    def kv_block(j):
        start = pl.multiple_of(j * blk, blk)
        return lambda p: (kvt_all[p * LANES:(p + 1) * LANES, pl.ds(start, blk)],
                          kvt_all[SB_WIDTH + p * LANES:SB_WIDTH + (p + 1) * LANES, pl.ds(start, blk)])

    def take_stage():
        kvt_all[:, pl.ds(pl.multiple_of(qi * tm, tm), tm)] = stage_kvt[...]
        _store_masked_queries(stage_q[...], qm_ref)

    def attend(block, rows, mask, first, fillers=(), extra_decay=None):
        return _sb_all_heads(qm_ref, kv_block(block), decay_ref, acc_ref, upper, mask, first, key_minor=True,
                             extra_decay=extra_decay, fillers=fillers, rows=rows)

    def attend_near(fillers):
        n = HEAD_PAIRS
        causal = _causal_mask(blk)
        attend(b0 + 1, hi, causal, True, fillers[0:n])
        attend(b0, lo, causal, True, fillers[n:2 * n])
        least_hi = attend(b0, hi, None, False, fillers[2 * n:3 * n])
        void = jnp.where(qi == 0, 2.0 * SKIP_LOG2, 0.0)
        least_lo = attend(jnp.maximum(b0 - 1, 0), lo, None, False, fillers[3 * n:4 * n], extra_decay=void)
        return least_lo, least_hi

    def attend_far(least_lo, least_hi):
        for rows, first_block, least in ((lo, b0 - 2, least_lo), (hi, b0 - 1, least_hi)):
            def body(state, rows=rows):
                j, _ = state
                return j - 1, attend(j, rows, None, False)

            lax.while_loop(_more_blocks, body, (first_block, least))
        _store_heads(acc_ref, a_ref)

    half = SB_WIDTH // 2

    def project_q(c):
        def piece():
            q = _proj_cols(xb_ref[...], w_in_ref, Q_COLS + c, half) * Q_SCALE
            stage_q[:, c:c + half] = q.astype(BF16)
        return piece

    def project_kv(out_ref, first_row, c):
        def piece():
            yt = _dot_nt(w_kvt_ref[first_row + c:first_row + c + half, :], xb_ref[...])
            out_ref[0, c:c + half, :] = yt
            stage_kvt[first_row + c:first_row + c + half, :] = yt.astype(BF16)
        return piece

    def projection_pieces():
        u_halves = []

        def project_u0():
            u_halves.append(_proj_cols(xb_ref[...], w_in_ref, U_COLS, half))

        def project_u1_pool():
            u_halves.append(_proj_cols(xb_ref[...], w_in_ref, U_COLS + half, half))
            pooled = _pool_mix(jnp.concatenate(u_halves, axis=1), hist_ref[0], pool_w_ref, pool_scale_ref, u_ext,
                               lax.rem(s, n_t), tm=tm, start_pos=0)
            for g, y in enumerate(pooled):
                p_ref[0, :, g * POOL_GC:(g + 1) * POOL_GC] = y
            pool_ref[0] = _pool_state(u_ext, tm)

        xb_ref[...] = x_ref[0].astype(BF16)
        return [project_q(0), project_q(half),
                project_kv(k_ref, 0, 0), project_kv(k_ref, 0, half),
                project_kv(v_ref, SB_WIDTH, 0), project_kv(v_ref, SB_WIDTH, half),
                project_u0, project_u1_pool]

    @pl.when(s == 0)
    def _():
        for piece in projection_pieces():
            piece()

    @pl.when(jnp.logical_and(s > 0, s < n_tiles))
    def _():
        take_stage()
        attend_far(*attend_near([f for piece in projection_pieces() for f in (piece, None)]))

    @pl.when(s == n_tiles)
    def _():
        take_stage()
        attend_far(*attend_near([]))


def _proj_attn_prompt(x, hist, w_in, pool_w, pool_scale2, ffn_weights, n_conv):
    B, T, _ = x.shape
    tm = 2 * ATTN_BLOCK
    assert T % tm == 0 and T >= POOL_HIST
    n_t = T // tm
    n_tiles = B * n_t
    assert n_conv <= n_tiles and all(w.shape[0] % (n_conv * BF16_SUBLANES) == 0 for w in ffn_weights)
    chunked = [pl.BlockSpec((w.shape[0] // n_conv, w.shape[1]), lambda s: (jnp.minimum(s, n_conv - 1), 0))
               for w in ffn_weights]
    kern = functools.partial(_proj_attn_kernel, tm=tm, n_t=n_t, n_tiles=n_tiles, n_conv=n_conv)
    proj = lambda s: jnp.minimum(s, n_tiles - 1)
    attn = lambda s: jnp.maximum(s - 1, 0)
    const = lambda *shape: pl.BlockSpec(shape, lambda s: (0,) * len(shape), pipeline_mode=pl.Buffered(1))
    proj_rows = lambda width: pl.BlockSpec((1, tm, width), lambda s: (proj(s) // n_t, proj(s) % n_t, 0))
    proj_cols = pl.BlockSpec((1, SB_WIDTH, tm), lambda s: (proj(s) // n_t, 0, proj(s) % n_t))
    return pl.pallas_call(
        kern,
        grid=(n_tiles + 1,),
        in_specs=[
            proj_rows(D_MODEL),
            pl.BlockSpec((1, HIST_ROWS, POOL_WIDTH), lambda s: (proj(s) // n_t, 0, 0)),
            const(D_MODEL, 3 * SB_WIDTH + POOL_WIDTH),
            const(len(POOL_WINDOWS), POOL_GC, POOL_GC),
            const(1, POOL_WIDTH),
        ] + chunked,
        out_specs=[
            proj_cols,
            proj_cols,
            proj_rows(POOL_WIDTH),
            pl.BlockSpec((1, POOL_HIST, POOL_WIDTH), lambda s: (proj(s) // n_t, 0, 0)),
            pl.BlockSpec((1, tm, SB_WIDTH), lambda s: (attn(s) // n_t, attn(s) % n_t, 0)),
        ] + chunked,
        out_shape=[
            jax.ShapeDtypeStruct((B, SB_WIDTH, T), F32),
            jax.ShapeDtypeStruct((B, SB_WIDTH, T), F32),
            jax.ShapeDtypeStruct((B, T, POOL_WIDTH), BF16),
            jax.ShapeDtypeStruct((B, POOL_HIST, POOL_WIDTH), F32),
            jax.ShapeDtypeStruct((B, T, SB_WIDTH), BF16),
        ] + [jax.ShapeDtypeStruct(w.shape, BF16) for w in ffn_weights],
        scratch_shapes=[
            pltpu.VMEM(w_in.shape, BF16),
            pltpu.VMEM((2 * SB_WIDTH, D_MODEL), BF16),
            pltpu.VMEM((HIST_ROWS + tm, POOL_WIDTH), F32),
            pltpu.VMEM((tm, D_MODEL), BF16),
            pltpu.VMEM((tm, SB_WIDTH), BF16),
            pltpu.VMEM((2 * SB_WIDTH, tm), BF16),
            pltpu.VMEM((2 * SB_WIDTH, T), BF16),
        ] + _attn_scratch(tm),
        compiler_params=pltpu.CompilerParams(
            dimension_semantics=("arbitrary",), vmem_limit_bytes=VMEM_LIMIT),
        name="proj_attn_prompt",
    )(x, hist, w_in, pool_w, pool_scale2, *ffn_weights)


def _proj_attn_sample_kernel(x_ref, hist_ref, w_in_f32_ref, pool_w_ref, pool_scale_ref,
                             ck_ref, cv_ref, ck_hbm, cv_hbm,
                             k_ref, v_ref, p_ref, pool_ref, a_ref,
                             w_in_ref, q_all, kv_new, u_all, u_ext, ck_buf, cv_buf, qm_ref, decay_ref, acc_ref,
                             *, t_new, past, blk):
    b = pl.program_id(0)
    rows = pl.ds(pl.multiple_of(b * t_new, t_new), t_new)

    @pl.when(b == 0)
    def _():
        w_in_ref[...] = w_in_f32_ref[...].astype(BF16)
        u_ext[...] = jnp.zeros_like(u_ext)
        x = x_ref[...].astype(BF16)
        q_all[...] = (_proj_cols(x, w_in_ref, Q_COLS, SB_WIDTH) * Q_SCALE).astype(BF16)
        k = _proj_cols(x, w_in_ref, K_COLS, SB_WIDTH)
        k_ref[...] = k
        kv_new[:, 0:SB_WIDTH] = k.astype(BF16)
        v = _proj_cols(x, w_in_ref, V_COLS, SB_WIDTH)
        v_ref[...] = v
        kv_new[:, SB_WIDTH:2 * SB_WIDTH] = v.astype(BF16)
        u_all[...] = _proj_cols(x, w_in_ref, U_COLS, POOL_WIDTH)

    pooled = _pool_mix(u_all[rows, :], hist_ref[0], pool_w_ref, pool_scale_ref, u_ext, 0, tm=t_new, start_pos=past)
    for g, y in enumerate(pooled):
        p_ref[rows, g * POOL_GC:(g + 1) * POOL_GC] = y
    pool_ref[b] = _pool_state(u_ext, t_new)

    _store_masked_queries(q_all[rows, :], qm_ref)
    new_kv = lambda p: (kv_new[rows, p * LANES:(p + 1) * LANES],
                        kv_new[rows, SB_WIDTH + p * LANES:SB_WIDTH + (p + 1) * LANES])
    _sb_all_heads(qm_ref, new_kv, decay_ref, acc_ref, _strict_upper(t_new), _causal_mask(t_new), True)
    upper = _strict_upper(blk)
    recent = lambda p: (ck_ref[0, p * LANES:(p + 1) * LANES, :].astype(BF16),
                        cv_ref[0, p * LANES:(p + 1) * LANES, :].astype(BF16))
    least = _sb_all_heads(qm_ref, recent, decay_ref, acc_ref, upper, None, False, key_minor=True)

    def body(state):
        j, _ = state
        start = pl.multiple_of(j * blk, blk)
        pltpu.sync_copy(ck_hbm.at[b, :, pl.ds(start, blk)], ck_buf)
        pltpu.sync_copy(cv_hbm.at[b, :, pl.ds(start, blk)], cv_buf)
        older = lambda p: (ck_buf[p * LANES:(p + 1) * LANES, :].astype(BF16),
                           cv_buf[p * LANES:(p + 1) * LANES, :].astype(BF16))
        return j - 1, _sb_all_heads(qm_ref, older, decay_ref, acc_ref, upper, None, False, key_minor=True)

    lax.while_loop(_more_blocks, body, (past // blk - 2, least))
    for p in range(HEAD_PAIRS):
        a_ref[rows, p * LANES:(p + 1) * LANES] = acc_ref[p].astype(BF16)


def _proj_attn_sample(x, hist, w_in, pool_w, pool_scale2, cache_kt, cache_vt):
    B, t_new, _ = x.shape
    past = cache_kt.shape[2]
    blk = ATTN_BLOCK
    assert past % blk == 0 and t_new >= HIST_ROWS
    n = B * t_new
    kern = functools.partial(_proj_attn_sample_kernel, t_new=t_new, past=past, blk=blk)
    const = lambda *shape: pl.BlockSpec(shape, lambda b: (0,) * len(shape), pipeline_mode=pl.Buffered(1))
    whole = lambda *shape: pl.BlockSpec(shape, lambda b: (0,) * len(shape))
    recent = pl.BlockSpec((1, SB_WIDTH, blk), lambda b: (b, 0, past // blk - 1))
    k, v, p, pool, a = pl.pallas_call(
        kern,
        grid=(B,),
        in_specs=[
            const(n, D_MODEL),
            pl.BlockSpec((1, HIST_ROWS, POOL_WIDTH), lambda b: (b, 0, 0)),
            const(D_MODEL, 3 * SB_WIDTH + POOL_WIDTH),
            const(len(POOL_WINDOWS), POOL_GC, POOL_GC),
            const(1, POOL_WIDTH),
            recent, recent,
            pl.BlockSpec(memory_space=pl.ANY), pl.BlockSpec(memory_space=pl.ANY),
        ],
        out_specs=[whole(n, SB_WIDTH), whole(n, SB_WIDTH), whole(n, POOL_WIDTH),
                   whole(B, POOL_HIST, POOL_WIDTH), whole(n, SB_WIDTH)],
        out_shape=[
            jax.ShapeDtypeStruct((n, SB_WIDTH), F32),
            jax.ShapeDtypeStruct((n, SB_WIDTH), F32),
            jax.ShapeDtypeStruct((n, POOL_WIDTH), BF16),
            jax.ShapeDtypeStruct((B, POOL_HIST, POOL_WIDTH), F32),
            jax.ShapeDtypeStruct((n, SB_WIDTH), BF16),
        ],
        scratch_shapes=[
            pltpu.VMEM(w_in.shape, BF16),
            pltpu.VMEM((n, SB_WIDTH), BF16),
            pltpu.VMEM((n, 2 * SB_WIDTH), BF16),
            pltpu.VMEM((n, POOL_WIDTH), F32),
            pltpu.VMEM((HIST_ROWS + t_new, POOL_WIDTH), F32),
            pltpu.VMEM((SB_WIDTH, blk), F32),
            pltpu.VMEM((SB_WIDTH, blk), F32),
        ] + _attn_scratch(t_new),
        compiler_params=pltpu.CompilerParams(
            dimension_semantics=("arbitrary",), vmem_limit_bytes=VMEM_LIMIT),
        name="proj_attn_sample",
    )(x.reshape(n, D_MODEL), hist, w_in, pool_w, pool_scale2, cache_kt, cache_vt, cache_kt, cache_vt)
    return k, v, p, pool, a


def _layer_norm(x, g, b):
    mu = jnp.mean(x, axis=-1, keepdims=True)
    xc = x - mu
    var = jnp.mean(xc * xc, axis=-1, keepdims=True)
    return xc * lax.rsqrt(var + LN_EPS) * g + b


def _ffn_rows(x_ref, a_ref, p_ref, wo_ref, g1, b1, wg_ref, wu_ref, wd_ref, acc_ref, x1b_ref, filler):
    m = x_ref.shape[0]
    halves = tuple(slice(i * m // FFN_PARTS, (i + 1) * m // FFN_PARTS) for i in range(FFN_PARTS))
    chunks = [(sum(FF_CHUNKS[:i]), w) for i, w in enumerate(FF_CHUNKS)]

    def swiglu_chunk(x1b, chunk):
        c, w = chunk
        gate = _dot(x1b, wg_ref[:, c:c + w])
        up = _dot(x1b, wu_ref[:, c:c + w])
        hmid = (gate * jax.nn.sigmoid(gate) * up).astype(BF16)
        return _dot(hmid, wd_ref[c:c + w, :])

    mix = [_dot(a_ref[r, :], wo_ref[0:SB_WIDTH, :]) + _dot(p_ref[r, :], wo_ref[SB_WIDTH:, :]) for r in halves]
    for r, mx in zip(halves, mix):
        x1 = _layer_norm(ALPHA * x_ref[r, :] + mx, g1, b1)
        x1b_ref[r, :] = x1.astype(BF16)
        acc_ref[r, :] = ALPHA * x1 + swiglu_chunk(x1b_ref[r, :], chunks[0])
    total = None
    for j, c in enumerate(chunks[1:]):
        anchor = filler(j, len(chunks) - 1)
        x1b_ref[0:anchor.shape[0], 0:anchor.shape[1]] += anchor.astype(BF16)
        total = acc_ref[...] + swiglu_chunk(x1b_ref[...], c)
        if c is not chunks[-1]:
            acc_ref[...] = total
    return total


def _out_ffn_kernel(xp_ref, ap_ref, pp_ref, xs_ref, as_ref, ps_ref, wo_s, wg_s, wu_s, wd_s,
                    g1_ref, b1_ref, g2_ref, b2_ref, yp_ref, ys_ref,
                    accp_ref, accs_ref, pre_ref, x1b_ref, *, n_tiles):
    i = pl.program_id(0)

    @pl.when(i == 0)
    def _():
        pre_ref[...] = jnp.zeros_like(pre_ref)

    def finish_previous_tile(j, n):
        rows = yp_ref.shape[0] // n
        r = slice(j * rows, (j + 1) * rows)
        y = _layer_norm(pre_ref[r, :], g2_ref[...], b2_ref[...])
        yp_ref[r, :] = y
        top = jnp.max(y.reshape(rows // BF16_SUBLANES, BF16_SUBLANES, D_MODEL), axis=0)
        top = functools.reduce(jnp.maximum, [top[:, c:c + LANES] for c in range(0, D_MODEL, LANES)])
        return jnp.where(top > F32_MAX, top, 0.0)

    def rows(x_ref, a_ref, p_ref, acc_ref):
        m = x_ref.shape[0]
        return _ffn_rows(x_ref, a_ref, p_ref, wo_s, g1_ref[...], b1_ref[...], wg_s, wu_s, wd_s, acc_ref,
                         x1b_ref.at[0:m, :], finish_previous_tile)

    @pl.when(i < n_tiles)
    def _():
        pre_ref[...] = rows(xp_ref, ap_ref, pp_ref, accp_ref)

    @pl.when(i == n_tiles)
    def _():
        ys_ref[...] = _layer_norm(rows(xs_ref, as_ref, ps_ref, accs_ref), g2_ref[...], b2_ref[...])


def _out_ffn(xp, ap, pp, xs, a_s, ps, wo_b, wg_b, wu_b, wd_b, g1, b1, g2, b2, *, tm):
    Np, Ns = xp.shape[0], xs.shape[0]
    assert Np % tm == 0 and sum(FF_CHUNKS) == D_FF
    n_tiles = Np // tm
    tile = lambda i: jnp.clip(i, 0, n_tiles - 1)
    row = lambda width: pl.BlockSpec((tm, width), lambda i: (tile(i), 0))
    once = lambda *shape: pl.BlockSpec(shape, lambda i: (0,) * len(shape), pipeline_mode=pl.Buffered(1))
    kern = functools.partial(_out_ffn_kernel, n_tiles=n_tiles)
    return pl.pallas_call(
        kern,
        grid=(n_tiles + 1,),
        in_specs=[
            row(D_MODEL), row(SB_WIDTH), row(POOL_WIDTH),
            once(Ns, D_MODEL), once(Ns, SB_WIDTH), once(Ns, POOL_WIDTH),
            once(*wo_b.shape), once(*wg_b.shape), once(*wu_b.shape), once(*wd_b.shape),
            once(1, D_MODEL), once(1, D_MODEL), once(1, D_MODEL), once(1, D_MODEL),
        ],
        out_specs=[pl.BlockSpec((tm, D_MODEL), lambda i: (tile(i - 1), 0)),
                   pl.BlockSpec((Ns, D_MODEL), lambda i: (0, 0))],
        out_shape=[jax.ShapeDtypeStruct((Np, D_MODEL), F32), jax.ShapeDtypeStruct((Ns, D_MODEL), F32)],
        scratch_shapes=[pltpu.VMEM((tm, D_MODEL), F32), pltpu.VMEM((Ns, D_MODEL), F32),
                        pltpu.VMEM((tm, D_MODEL), F32), pltpu.VMEM((tm, D_MODEL), BF16)],
        compiler_params=pltpu.CompilerParams(
            dimension_semantics=("arbitrary",), vmem_limit_bytes=VMEM_LIMIT),
        name="out_proj_ffn",
    )(xp, ap, pp, xs, a_s, ps, wo_b, wg_b, wu_b, wd_b, g1, b1, g2, b2)


def kernel(x_prompt, x_sample, cache_k, cache_v, state_pool, w_in, pool_w, pool_scale, w_out,
           ln1_g, ln1_b, w_gate, w_up, w_down, ln2_g, ln2_b):
    Bp, Tp, _ = x_prompt.shape
    Bs, Ts, _ = x_sample.shape
    past = cache_k.shape[1]

    row2 = lambda a: a.reshape(1, -1).astype(F32)
    pool_scale2 = row2(pool_scale)

    hist_s = jnp.concatenate([jnp.zeros((Bs, HIST_ROWS - POOL_HIST, POOL_WIDTH), F32), state_pool.astype(F32)], axis=1)
    feature_major = lambda c: jnp.transpose(c, (0, 2, 3, 1)).reshape(Bs, SB_WIDTH, past)
    k_s, v_s, mix_s, pool_s, a_s = _proj_attn_sample(x_sample, hist_s, w_in, pool_w, pool_scale2,
                                                     feature_major(cache_k), feature_major(cache_v))

    hist_p = jnp.zeros((Bp, HIST_ROWS, POOL_WIDTH), F32)
    kt_p, vt_p, mix_p, pool_p, a_p, wo_b, wg_b, wu_b, wd_b = _proj_attn_prompt(
        x_prompt, hist_p, w_in, pool_w, pool_scale2, ffn_weights=(w_out, w_gate, w_up, w_down), n_conv=16)
    position_major = lambda c: jnp.transpose(c.reshape(Bp, SB_HEADS, SB_HEAD_DIM, Tp), (0, 3, 1, 2))

    y_p, y_s = _out_ffn(
        x_prompt.reshape(Bp * Tp, D_MODEL), a_p.reshape(Bp * Tp, SB_WIDTH), mix_p.reshape(Bp * Tp, POOL_WIDTH),
        x_sample.reshape(Bs * Ts, D_MODEL), a_s, mix_s,
        wo_b, wg_b, wu_b, wd_b, row2(ln1_g), row2(ln1_b), row2(ln2_g), row2(ln2_b), tm=512)

    heads = lambda a: a.reshape(Bs, Ts, SB_HEADS, SB_HEAD_DIM)
    return (y_p.reshape(Bp, Tp, D_MODEL), y_s.reshape(Bs, Ts, D_MODEL),
            position_major(kt_p), position_major(vt_p), pool_p, heads(k_s), heads(v_s), pool_s)
```

```python
import functools

import jax
import jax.numpy as jnp
from jax import lax
from jax.experimental import pallas as pl
from jax.experimental.pallas import tpu as pltpu

D_MODEL = 1024
SB_WIDTH = 512
SB_HEADS = 8
SB_HEAD_DIM = 64
POOL_WIDTH = 512
POOL_WINDOWS = (2, 4, 8, 16)
POOL_GC = 128
POOL_HIST = 15
HIST_ROWS = 16
D_FF = 2816
DEPTH = 1
ALPHA = (2 * DEPTH) ** 0.25
LN_EPS = 1e-5
LANES = 128
HEAD_PAIRS = SB_WIDTH // LANES
LOG2E = 1.4426950408889634
Q_SCALE = SB_HEAD_DIM ** -0.5 * LOG2E
SKIP_LOG2 = 160.0
SP_CLAMP = 32.0
ATTN_BLOCK = 256
FF_CHUNKS = (256, 768, 768, 768, 256)
FFN_PARTS = 2
BF16_SUBLANES = 16
VMEM_LIMIT = 56 * 1024 * 1024

F32 = jnp.float32
BF16 = jnp.bfloat16
F32_MAX = 3.4028234663852886e38


def _dot(a, b):
    return jnp.dot(a, b, preferred_element_type=F32)


def _dot_nt(a, b):
    return lax.dot_general(a, b, (((1,), (1,)), ((), ())), preferred_element_type=F32)


Q_COLS, K_COLS, V_COLS, U_COLS = (SB_WIDTH * i for i in range(4))


def _proj_cols(x, w_in_ref, start, width):
    return _dot(x, w_in_ref[:, start:start + width])


def _pool_mix(u, hist, pool_w_ref, pool_scale_ref, u_ext, t, *, tm, start_pos):
    u_ext[0:HIST_ROWS, :] = jnp.where(t == 0, hist, u_ext[tm:tm + HIST_ROWS, :])
    u_ext[HIST_ROWS:HIST_ROWS + tm, :] = u

    pos = start_pos + t * tm + lax.broadcasted_iota(jnp.int32, (tm, 1), 0)
    pooled = []
    for g, w in enumerate(POOL_WINDOWS):
        cols = slice(g * POOL_GC, (g + 1) * POOL_GC)
        s = u_ext[:, cols]
        span = 1
        while span < w:
            s = s + pltpu.roll(s, span, 0)
            span *= 2
        cnt = jnp.minimum(w, pos + 1).astype(F32)
        d = s[HIST_ROWS:, :] / cnt - u[:, cols]
        y = _dot(d.astype(BF16), pool_w_ref[g].astype(BF16)) * pool_scale_ref[:, cols]
        pooled.append(y.astype(BF16))
    return pooled


def _pool_state(u_ext, tm):
    return u_ext[tm + HIST_ROWS - POOL_HIST:tm + HIST_ROWS, :]


def _strict_upper(n):
    r = lax.broadcasted_iota(jnp.int32, (n, n), 0)
    c = lax.broadcasted_iota(jnp.int32, (n, n), 1)
    return (r > c).astype(BF16)


def _causal_mask(n):
    r = lax.broadcasted_iota(jnp.int32, (n, n), 0)
    c = lax.broadcasted_iota(jnp.int32, (n, n), 1)
    return c < r


def _store_masked_queries(q, qm_ref):
    m = q.shape[0]
    lower_half = lax.broadcasted_iota(jnp.int32, (m, LANES), 1) < SB_HEAD_DIM
    for p in range(HEAD_PAIRS):
        q2 = q[:, p * LANES:(p + 1) * LANES]
        zero = jnp.zeros_like(q2)
        qm_ref[p, 0:m, :] = jnp.where(lower_half, q2, zero)
        qm_ref[p, m:2 * m, :] = jnp.where(lower_half, zero, q2)


def _sb_all_heads(qm_ref, kv_block, decay_ref, acc_ref, upper, mask, first, key_minor=False, extra_decay=None,
                  fillers=(), rows=None):
    fillers = list(fillers)
    tile_rows = qm_ref.shape[1] // 2
    r0, r1 = (0, tile_rows) if rows is None else rows
    m = r1 - r0
    lower_half = lax.broadcasted_iota(jnp.int32, (m, LANES), 1) < SB_HEAD_DIM
    mask2 = None if mask is None else jnp.concatenate([mask, mask], axis=0)
    kv = [kv_block(p) for p in range(HEAD_PAIRS)]

    def queries(p):
        if rows is None:
            return qm_ref[p]
        return jnp.concatenate([qm_ref[p, r0:r1, :], qm_ref[p, tile_rows + r0:tile_rows + r1, :]], axis=0)

    logit, sps, rowsums = [], [], []
    for p in range(HEAD_PAIRS):
        z = (_dot if key_minor else _dot_nt)(queries(p), kv[p][0])
        sp = jnp.maximum(z, jnp.log(1.0 + jnp.exp2(jnp.minimum(z, SP_CLAMP))) * LOG2E)
        spm = sp if mask2 is None else jnp.where(mask2, sp, 0.0)
        sps.append(spm.astype(BF16))
        logit.append(z - sp)
        rowsums.append(jnp.sum(spm, axis=1, keepdims=True))
        filler = fillers.pop(0) if fillers else None
        if filler is not None:
            filler()

    later = _dot(jnp.concatenate(sps, axis=0), upper)

    least = None
    for p in range(HEAD_PAIRS):
        a = jnp.exp2(logit[p] - later[2 * m * p:2 * m * (p + 1)])
        if mask2 is not None:
            a = jnp.where(mask2, a, 0.0)
        pv = (_dot_nt if key_minor else _dot)(a.astype(BF16), kv[p][1])
        merged = jnp.where(lower_half, pv[0:m], pv[m:2 * m])
        rowsum = jnp.where(lower_half, rowsums[p][0:m], rowsums[p][m:2 * m])
        if first:
            decay = rowsum
            acc_ref[p, r0:r1, :] = merged
        else:
            prev = decay_ref[p, r0:r1, :]
            if extra_decay is not None:
                prev = prev + extra_decay
            decay = prev + rowsum
            acc_ref[p, r0:r1, :] += merged * jnp.exp2(-prev)
        decay_ref[p, r0:r1, :] = decay
        least = decay if least is None else jnp.minimum(least, decay)
    return jnp.min(least)


def _more_blocks(state):
    j, least = state
    return jnp.logical_and(j >= 0, least < SKIP_LOG2)


def _store_heads(acc_ref, o_ref):
    for p in range(HEAD_PAIRS):
        o_ref[0, :, p * LANES:(p + 1) * LANES] = acc_ref[p].astype(BF16)


def _attn_scratch(m):
    return [pltpu.VMEM((HEAD_PAIRS, 2 * m, LANES), BF16), pltpu.VMEM((HEAD_PAIRS, m, LANES), F32),
            pltpu.VMEM((HEAD_PAIRS, m, LANES), F32)]


def _proj_attn_kernel(x_ref, hist_ref, w_in_f32_ref, pool_w_ref, pool_scale_ref,
                      wo_c, wg_c, wu_c, wd_c,
                      k_ref, v_ref, p_ref, pool_ref, a_ref, wo_b, wg_b, wu_b, wd_b,
                      w_in_ref, w_kvt_ref, u_ext, xb_ref, stage_q, stage_kvt, kvt_all, qm_ref, decay_ref, acc_ref,
                      *, tm, n_t, n_tiles, n_conv):
    s = pl.program_id(0)

    @pl.when(s < n_conv)
    def _():
        for c_ref, b_ref in ((wo_c, wo_b), (wg_c, wg_b), (wu_c, wu_b), (wd_c, wd_b)):
            b_ref[...] = c_ref[...].astype(BF16)

    @pl.when(s == 0)
    def _():
        w_in_ref[...] = w_in_f32_ref[...].astype(BF16)
        w_kvt_ref[...] = w_in_f32_ref[:, K_COLS:U_COLS].T.astype(BF16)
        stage_q[...] = jnp.zeros_like(stage_q)
        stage_kvt[...] = jnp.zeros_like(stage_kvt)
        u_ext[...] = jnp.zeros_like(u_ext)

    blk = ATTN_BLOCK
    upper = _strict_upper(blk)
    qi = lax.rem(jnp.maximum(s - 1, 0), n_t)
    b0 = (tm // blk) * qi
    lo, hi = (0, blk), (blk, tm)

    def kv_block(j):
        start = pl.multiple_of(j * blk, blk)
        return lambda p: (kvt_all[p * LANES:(p + 1) * LANES, pl.ds(start, blk)],
                          kvt_all[SB_WIDTH + p * LANES:SB_WIDTH + (p + 1) * LANES, pl.ds(start, blk)])

    def take_stage():
        kvt_all[:, pl.ds(pl.multiple_of(qi * tm, tm), tm)] = stage_kvt[...]
        _store_masked_queries(stage_q[...], qm_ref)

    def attend(block, rows, mask, first, fillers=(), extra_decay=None):
        return _sb_all_heads(qm_ref, kv_block(block), decay_ref, acc_ref, upper, mask, first, key_minor=True,
                             extra_decay=extra_decay, fillers=fillers, rows=rows)

    def attend_near(fillers):
        n = HEAD_PAIRS
        causal = _causal_mask(blk)
        attend(b0 + 1, hi, causal, True, fillers[0:n])
        attend(b0, lo, causal, True, fillers[n:2 * n])
        least_hi = attend(b0, hi, None, False, fillers[2 * n:3 * n])
        void = jnp.where(qi == 0, 2.0 * SKIP_LOG2, 0.0)
        least_lo = attend(jnp.maximum(b0 - 1, 0), lo, None, False, fillers[3 * n:4 * n], extra_decay=void)
        return least_lo, least_hi

    def attend_far(least_lo, least_hi):
        for rows, first_block, least in ((lo, b0 - 2, least_lo), (hi, b0 - 1, least_hi)):
            def body(state, rows=rows):
                j, _ = state
                return j - 1, attend(j, rows, None, False)

            lax.while_loop(_more_blocks, body, (first_block, least))
        _store_heads(acc_ref, a_ref)

    half = SB_WIDTH // 2

    def project_q(c, r):
        def piece():
            q = _proj_cols(xb_ref[r[0]:r[1], :], w_in_ref, Q_COLS + c, half) * Q_SCALE
            stage_q[r[0]:r[1], c:c + half] = q.astype(BF16)
        return piece

    def project_kv(out_ref, first_row, c, r):
        def piece():
            yt = _dot_nt(w_kvt_ref[first_row + c:first_row + c + half, :], xb_ref[r[0]:r[1], :])
            out_ref[0, c:c + half, r[0]:r[1]] = yt
            stage_kvt[first_row + c:first_row + c + half, r[0]:r[1]] = yt.astype(BF16)
        return piece

    u_parts = {}

    def project_u(c, r, last=False):
        def piece():
            u_parts[(r, c)] = _proj_cols(xb_ref[r[0]:r[1], :], w_in_ref, U_COLS + c, half)
            if last:
                u = jnp.concatenate([jnp.concatenate([u_parts[(rr, 0)], u_parts[(rr, half)]], axis=1)
                                     for rr in (lo, hi)], axis=0)
                pooled = _pool_mix(u, hist_ref[0], pool_w_ref, pool_scale_ref, u_ext, lax.rem(s, n_t), tm=tm,
                                   start_pos=0)
                for g, y in enumerate(pooled):
                    p_ref[0, :, g * POOL_GC:(g + 1) * POOL_GC] = y
                pool_ref[0] = _pool_state(u_ext, tm)
        return piece

    @pl.when(s < n_tiles)
    def _():
        take_stage()
        xb_ref[...] = x_ref[0].astype(BF16)
        pieces = [project_q(c, r) for r in (lo, hi) for c in (0, half)]
        pieces += [project_kv(k_ref, 0, c, r) for r in (lo, hi) for c in (0, half)]
        pieces += [project_kv(v_ref, SB_WIDTH, c, r) for r in (lo, hi) for c in (0, half)]
        pieces += [project_u(0, lo), project_u(half, lo), project_u(0, hi), project_u(half, hi, last=True)]
        attend_far(*attend_near(pieces))

    @pl.when(s == n_tiles)
    def _():
        take_stage()
        attend_far(*attend_near([]))


def _proj_attn_prompt(x, hist, w_in, pool_w, pool_scale2, ffn_weights, n_conv):
    B, T, _ = x.shape
    tm = 2 * ATTN_BLOCK
    assert T % tm == 0 and T >= POOL_HIST
    n_t = T // tm
    n_tiles = B * n_t
    assert n_conv <= n_tiles and all(w.shape[0] % (n_conv * BF16_SUBLANES) == 0 for w in ffn_weights)
    chunked = [pl.BlockSpec((w.shape[0] // n_conv, w.shape[1]), lambda s: (jnp.minimum(s, n_conv - 1), 0))
               for w in ffn_weights]
    kern = functools.partial(_proj_attn_kernel, tm=tm, n_t=n_t, n_tiles=n_tiles, n_conv=n_conv)
    proj = lambda s: jnp.minimum(s, n_tiles - 1)
    attn = lambda s: jnp.maximum(s - 1, 0)
    const = lambda *shape: pl.BlockSpec(shape, lambda s: (0,) * len(shape), pipeline_mode=pl.Buffered(1))
    proj_rows = lambda width: pl.BlockSpec((1, tm, width), lambda s: (proj(s) // n_t, proj(s) % n_t, 0))
    proj_cols = pl.BlockSpec((1, SB_WIDTH, tm), lambda s: (proj(s) // n_t, 0, proj(s) % n_t))
    return pl.pallas_call(
        kern,
        grid=(n_tiles + 1,),
        in_specs=[
            proj_rows(D_MODEL),
            pl.BlockSpec((1, HIST_ROWS, POOL_WIDTH), lambda s: (proj(s) // n_t, 0, 0)),
            const(D_MODEL, 3 * SB_WIDTH + POOL_WIDTH),
            const(len(POOL_WINDOWS), POOL_GC, POOL_GC),
            const(1, POOL_WIDTH),
        ] + chunked,
        out_specs=[
            proj_cols,
            proj_cols,
            proj_rows(POOL_WIDTH),
            pl.BlockSpec((1, POOL_HIST, POOL_WIDTH), lambda s: (proj(s) // n_t, 0, 0)),
            pl.BlockSpec((1, tm, SB_WIDTH), lambda s: (attn(s) // n_t, attn(s) % n_t, 0)),
        ] + chunked,
        out_shape=[
            jax.ShapeDtypeStruct((B, SB_WIDTH, T), F32),
            jax.ShapeDtypeStruct((B, SB_WIDTH, T), F32),
            jax.ShapeDtypeStruct((B, T, POOL_WIDTH), BF16),
            jax.ShapeDtypeStruct((B, POOL_HIST, POOL_WIDTH), F32),
            jax.ShapeDtypeStruct((B, T, SB_WIDTH), BF16),
        ] + [jax.ShapeDtypeStruct(w.shape, BF16) for w in ffn_weights],
        scratch_shapes=[
            pltpu.VMEM(w_in.shape, BF16),
            pltpu.VMEM((2 * SB_WIDTH, D_MODEL), BF16),
            pltpu.VMEM((HIST_ROWS + tm, POOL_WIDTH), F32),
            pltpu.VMEM((tm, D_MODEL), BF16),
            pltpu.VMEM((tm, SB_WIDTH), BF16),
            pltpu.VMEM((2 * SB_WIDTH, tm), BF16),
            pltpu.VMEM((2 * SB_WIDTH, T), BF16),
        ] + _attn_scratch(tm),
        compiler_params=pltpu.CompilerParams(
            dimension_semantics=("arbitrary",), vmem_limit_bytes=VMEM_LIMIT),
        name="proj_attn_prompt",
    )(x, hist, w_in, pool_w, pool_scale2, *ffn_weights)


def _proj_attn_sample_kernel(x_ref, hist_ref, w_in_f32_ref, pool_w_ref, pool_scale_ref,
                             ck_ref, cv_ref, ck_hbm, cv_hbm,
                             k_ref, v_ref, p_ref, pool_ref, a_ref,
                             w_in_ref, q_all, kv_new, u_all, u_ext, ck_buf, cv_buf, qm_ref, decay_ref, acc_ref,
                             *, t_new, past, blk):
    b = pl.program_id(0)
    rows = pl.ds(pl.multiple_of(b * t_new, t_new), t_new)

    @pl.when(b == 0)
    def _():
        w_in_ref[...] = w_in_f32_ref[...].astype(BF16)
        u_ext[...] = jnp.zeros_like(u_ext)
        x = x_ref[...].astype(BF16)
        q_all[...] = (_proj_cols(x, w_in_ref, Q_COLS, SB_WIDTH) * Q_SCALE).astype(BF16)
        k = _proj_cols(x, w_in_ref, K_COLS, SB_WIDTH)
        k_ref[...] = k
        kv_new[:, 0:SB_WIDTH] = k.astype(BF16)
        v = _proj_cols(x, w_in_ref, V_COLS, SB_WIDTH)
        v_ref[...] = v
        kv_new[:, SB_WIDTH:2 * SB_WIDTH] = v.astype(BF16)
        u_all[...] = _proj_cols(x, w_in_ref, U_COLS, POOL_WIDTH)

    pooled = _pool_mix(u_all[rows, :], hist_ref[0], pool_w_ref, pool_scale_ref, u_ext, 0, tm=t_new, start_pos=past)
    for g, y in enumerate(pooled):
        p_ref[rows, g * POOL_GC:(g + 1) * POOL_GC] = y
    pool_ref[b] = _pool_state(u_ext, t_new)

    _store_masked_queries(q_all[rows, :], qm_ref)
    new_kv = lambda p: (kv_new[rows, p * LANES:(p + 1) * LANES],
                        kv_new[rows, SB_WIDTH + p * LANES:SB_WIDTH + (p + 1) * LANES])
    _sb_all_heads(qm_ref, new_kv, decay_ref, acc_ref, _strict_upper(t_new), _causal_mask(t_new), True)
    upper = _strict_upper(blk)
    recent = lambda p: (ck_ref[0, p * LANES:(p + 1) * LANES, :].astype(BF16),
                        cv_ref[0, p * LANES:(p + 1) * LANES, :].astype(BF16))
    least = _sb_all_heads(qm_ref, recent, decay_ref, acc_ref, upper, None, False, key_minor=True)

    def body(state):
        j, _ = state
        start = pl.multiple_of(j * blk, blk)
        pltpu.sync_copy(ck_hbm.at[b, :, pl.ds(start, blk)], ck_buf)
        pltpu.sync_copy(cv_hbm.at[b, :, pl.ds(start, blk)], cv_buf)
        older = lambda p: (ck_buf[p * LANES:(p + 1) * LANES, :].astype(BF16),
                           cv_buf[p * LANES:(p + 1) * LANES, :].astype(BF16))
        return j - 1, _sb_all_heads(qm_ref, older, decay_ref, acc_ref, upper, None, False, key_minor=True)

    lax.while_loop(_more_blocks, body, (past // blk - 2, least))
    for p in range(HEAD_PAIRS):
        a_ref[rows, p * LANES:(p + 1) * LANES] = acc_ref[p].astype(BF16)


def _proj_attn_sample(x, hist, w_in, pool_w, pool_scale2, cache_kt, cache_vt):
    B, t_new, _ = x.shape
    past = cache_kt.shape[2]
    blk = ATTN_BLOCK
    assert past % blk == 0 and t_new >= HIST_ROWS
    n = B * t_new
    kern = functools.partial(_proj_attn_sample_kernel, t_new=t_new, past=past, blk=blk)
    const = lambda *shape: pl.BlockSpec(shape, lambda b: (0,) * len(shape), pipeline_mode=pl.Buffered(1))
    whole = lambda *shape: pl.BlockSpec(shape, lambda b: (0,) * len(shape))
    recent = pl.BlockSpec((1, SB_WIDTH, blk), lambda b: (b, 0, past // blk - 1))
    k, v, p, pool, a = pl.pallas_call(
        kern,
        grid=(B,),
        in_specs=[
            const(n, D_MODEL),
            pl.BlockSpec((1, HIST_ROWS, POOL_WIDTH), lambda b: (b, 0, 0)),
            const(D_MODEL, 3 * SB_WIDTH + POOL_WIDTH),
            const(len(POOL_WINDOWS), POOL_GC, POOL_GC),
            const(1, POOL_WIDTH),
            recent, recent,
            pl.BlockSpec(memory_space=pl.ANY), pl.BlockSpec(memory_space=pl.ANY),
        ],
        out_specs=[whole(n, SB_WIDTH), whole(n, SB_WIDTH), whole(n, POOL_WIDTH),
                   whole(B, POOL_HIST, POOL_WIDTH), whole(n, SB_WIDTH)],
        out_shape=[
            jax.ShapeDtypeStruct((n, SB_WIDTH), F32),
            jax.ShapeDtypeStruct((n, SB_WIDTH), F32),
            jax.ShapeDtypeStruct((n, POOL_WIDTH), BF16),
            jax.ShapeDtypeStruct((B, POOL_HIST, POOL_WIDTH), F32),
            jax.ShapeDtypeStruct((n, SB_WIDTH), BF16),
        ],
        scratch_shapes=[
            pltpu.VMEM(w_in.shape, BF16),
            pltpu.VMEM((n, SB_WIDTH), BF16),
            pltpu.VMEM((n, 2 * SB_WIDTH), BF16),
            pltpu.VMEM((n, POOL_WIDTH), F32),
            pltpu.VMEM((HIST_ROWS + t_new, POOL_WIDTH), F32),
            pltpu.VMEM((SB_WIDTH, blk), F32),
            pltpu.VMEM((SB_WIDTH, blk), F32),
        ] + _attn_scratch(t_new),
        compiler_params=pltpu.CompilerParams(
            dimension_semantics=("arbitrary",), vmem_limit_bytes=VMEM_LIMIT),
        name="proj_attn_sample",
    )(x.reshape(n, D_MODEL), hist, w_in, pool_w, pool_scale2, cache_kt, cache_vt, cache_kt, cache_vt)
    return k, v, p, pool, a


def _layer_norm(x, g, b):
    mu = jnp.mean(x, axis=-1, keepdims=True)
    xc = x - mu
    var = jnp.mean(xc * xc, axis=-1, keepdims=True)
    return xc * lax.rsqrt(var + LN_EPS) * g + b


def _ffn_rows(x_ref, a_ref, p_ref, wo_ref, g1, b1, wg_ref, wu_ref, wd_ref, acc_ref, x1b_ref, filler):
    m = x_ref.shape[0]
    halves = tuple(slice(i * m // FFN_PARTS, (i + 1) * m // FFN_PARTS) for i in range(FFN_PARTS))
    chunks = [(sum(FF_CHUNKS[:i]), w) for i, w in enumerate(FF_CHUNKS)]

    def swiglu_chunk(x1b, chunk):
        c, w = chunk
        gate = _dot(x1b, wg_ref[:, c:c + w])
        up = _dot(x1b, wu_ref[:, c:c + w])
        hmid = (gate * jax.nn.sigmoid(gate) * up).astype(BF16)
        return _dot(hmid, wd_ref[c:c + w, :])

    mix = [_dot(a_ref[r, :], wo_ref[0:SB_WIDTH, :]) + _dot(p_ref[r, :], wo_ref[SB_WIDTH:, :]) for r in halves]
    for r, mx in zip(halves, mix):
        x1 = _layer_norm(ALPHA * x_ref[r, :] + mx, g1, b1)
        x1b_ref[r, :] = x1.astype(BF16)
        acc_ref[r, :] = ALPHA * x1 + swiglu_chunk(x1b_ref[r, :], chunks[0])
    total = None
    for j, c in enumerate(chunks[1:]):
        anchor = filler(j, len(chunks) - 1)
        x1b_ref[0:anchor.shape[0], 0:anchor.shape[1]] += anchor.astype(BF16)
        total = acc_ref[...] + swiglu_chunk(x1b_ref[...], c)
        if c is not chunks[-1]:
            acc_ref[...] = total
    return total


def _out_ffn_kernel(xp_ref, ap_ref, pp_ref, xs_ref, as_ref, ps_ref, wo_s, wg_s, wu_s, wd_s,
                    g1_ref, b1_ref, g2_ref, b2_ref, yp_ref, ys_ref,
                    accp_ref, accs_ref, pre_ref, x1b_ref, *, n_tiles):
    i = pl.program_id(0)

    @pl.when(i == 0)
    def _():
        pre_ref[...] = jnp.zeros_like(pre_ref)

    def finish_previous_tile(j, n):
        rows = yp_ref.shape[0] // n
        r = slice(j * rows, (j + 1) * rows)
        y = _layer_norm(pre_ref[r, :], g2_ref[...], b2_ref[...])
        yp_ref[r, :] = y
        top = jnp.max(y.reshape(rows // BF16_SUBLANES, BF16_SUBLANES, D_MODEL), axis=0)
        top = functools.reduce(jnp.maximum, [top[:, c:c + LANES] for c in range(0, D_MODEL, LANES)])
        return jnp.where(top > F32_MAX, top, 0.0)

    def rows(x_ref, a_ref, p_ref, acc_ref):
        m = x_ref.shape[0]
        return _ffn_rows(x_ref, a_ref, p_ref, wo_s, g1_ref[...], b1_ref[...], wg_s, wu_s, wd_s, acc_ref,
                         x1b_ref.at[0:m, :], finish_previous_tile)

    @pl.when(i < n_tiles)
    def _():
        pre_ref[...] = rows(xp_ref, ap_ref, pp_ref, accp_ref)

    @pl.when(i == n_tiles)
    def _():
        ys_ref[...] = _layer_norm(rows(xs_ref, as_ref, ps_ref, accs_ref), g2_ref[...], b2_ref[...])


def _out_ffn(xp, ap, pp, xs, a_s, ps, wo_b, wg_b, wu_b, wd_b, g1, b1, g2, b2, *, tm):
    Np, Ns = xp.shape[0], xs.shape[0]
    assert Np % tm == 0 and sum(FF_CHUNKS) == D_FF
    n_tiles = Np // tm
    tile = lambda i: jnp.clip(i, 0, n_tiles - 1)
    row = lambda width: pl.BlockSpec((tm, width), lambda i: (tile(i), 0))
    once = lambda *shape: pl.BlockSpec(shape, lambda i: (0,) * len(shape), pipeline_mode=pl.Buffered(1))
    kern = functools.partial(_out_ffn_kernel, n_tiles=n_tiles)
    return pl.pallas_call(
        kern,
        grid=(n_tiles + 1,),
        in_specs=[
            row(D_MODEL), row(SB_WIDTH), row(POOL_WIDTH),
            once(Ns, D_MODEL), once(Ns, SB_WIDTH), once(Ns, POOL_WIDTH),
            once(*wo_b.shape), once(*wg_b.shape), once(*wu_b.shape), once(*wd_b.shape),
            once(1, D_MODEL), once(1, D_MODEL), once(1, D_MODEL), once(1, D_MODEL),
        ],
        out_specs=[pl.BlockSpec((tm, D_MODEL), lambda i: (tile(i - 1), 0)),
                   pl.BlockSpec((Ns, D_MODEL), lambda i: (0, 0))],
        out_shape=[jax.ShapeDtypeStruct((Np, D_MODEL), F32), jax.ShapeDtypeStruct((Ns, D_MODEL), F32)],
        scratch_shapes=[pltpu.VMEM((tm, D_MODEL), F32), pltpu.VMEM((Ns, D_MODEL), F32),
                        pltpu.VMEM((tm, D_MODEL), F32), pltpu.VMEM((tm, D_MODEL), BF16)],
        compiler_params=pltpu.CompilerParams(
            dimension_semantics=("arbitrary",), vmem_limit_bytes=VMEM_LIMIT),
        name="out_proj_ffn",
    )(xp, ap, pp, xs, a_s, ps, wo_b, wg_b, wu_b, wd_b, g1, b1, g2, b2)


def kernel(x_prompt, x_sample, cache_k, cache_v, state_pool, w_in, pool_w, pool_scale, w_out,
           ln1_g, ln1_b, w_gate, w_up, w_down, ln2_g, ln2_b):
    Bp, Tp, _ = x_prompt.shape
    Bs, Ts, _ = x_sample.shape
    past = cache_k.shape[1]

    row2 = lambda a: a.reshape(1, -1).astype(F32)
    pool_scale2 = row2(pool_scale)

    hist_s = jnp.concatenate([jnp.zeros((Bs, HIST_ROWS - POOL_HIST, POOL_WIDTH), F32), state_pool.astype(F32)], axis=1)
    feature_major = lambda c: jnp.transpose(c, (0, 2, 3, 1)).reshape(Bs, SB_WIDTH, past)
    k_s, v_s, mix_s, pool_s, a_s = _proj_attn_sample(x_sample, hist_s, w_in, pool_w, pool_scale2,
                                                     feature_major(cache_k), feature_major(cache_v))

    hist_p = jnp.zeros((Bp, HIST_ROWS, POOL_WIDTH), F32)
    kt_p, vt_p, mix_p, pool_p, a_p, wo_b, wg_b, wu_b, wd_b = _proj_attn_prompt(
        x_prompt, hist_p, w_in, pool_w, pool_scale2, ffn_weights=(w_out, w_gate, w_up, w_down), n_conv=16)
    position_major = lambda c: jnp.transpose(c.reshape(Bp, SB_HEADS, SB_HEAD_DIM, Tp), (0, 3, 1, 2))

    y_p, y_s = _out_ffn(
        x_prompt.reshape(Bp * Tp, D_MODEL), a_p.reshape(Bp * Tp, SB_WIDTH), mix_p.reshape(Bp * Tp, POOL_WIDTH),
        x_sample.reshape(Bs * Ts, D_MODEL), a_s, mix_s,
        wo_b, wg_b, wu_b, wd_b, row2(ln1_g), row2(ln1_b), row2(ln2_g), row2(ln2_b), tm=512)

    heads = lambda a: a.reshape(Bs, Ts, SB_HEADS, SB_HEAD_DIM)
    return (y_p.reshape(Bp, Tp, D_MODEL), y_s.reshape(Bs, Ts, D_MODEL),
            position_major(kt_p), position_major(vt_p), pool_p, heads(k_s), heads(v_s), pool_s)
```

```python
import functools

import jax
import jax.numpy as jnp
from jax import lax
from jax.experimental import pallas as pl
from jax.experimental.pallas import tpu as pltpu

D_MODEL = 1024
SB_WIDTH = 512
SB_HEADS = 8
SB_HEAD_DIM = 64
POOL_WIDTH = 512
POOL_WINDOWS = (2, 4, 8, 16)
POOL_GC = 128
POOL_HIST = 15
HIST_ROWS = 16
D_FF = 2816
DEPTH = 1
ALPHA = (2 * DEPTH) ** 0.25
LN_EPS = 1e-5
LANES = 128
HEAD_PAIRS = SB_WIDTH // LANES
LOG2E = 1.4426950408889634
Q_SCALE = SB_HEAD_DIM ** -0.5 * LOG2E
SKIP_LOG2 = 160.0
SP_CLAMP = 32.0
ATTN_BLOCK = 256
FF_CHUNKS = (256, 768, 768, 768, 256)
FFN_PARTS = 2
BF16_SUBLANES = 16
VMEM_LIMIT = 56 * 1024 * 1024

F32 = jnp.float32
BF16 = jnp.bfloat16
F32_MAX = 3.4028234663852886e38


def _dot(a, b):
    return jnp.dot(a, b, preferred_element_type=F32)


def _dot_nt(a, b):
    return lax.dot_general(a, b, (((1,), (1,)), ((), ())), preferred_element_type=F32)


Q_COLS, K_COLS, V_COLS, U_COLS = (SB_WIDTH * i for i in range(4))


def _proj_cols(x, w_in_ref, start, width):
    return _dot(x, w_in_ref[:, start:start + width])


def _pool_mix(u, hist, pool_w_ref, pool_scale_ref, u_ext, t, *, tm, start_pos):
    u_ext[0:HIST_ROWS, :] = jnp.where(t == 0, hist, u_ext[tm:tm + HIST_ROWS, :])
    u_ext[HIST_ROWS:HIST_ROWS + tm, :] = u

    pos = start_pos + t * tm + lax.broadcasted_iota(jnp.int32, (tm, 1), 0)
    pooled = []
    for g, w in enumerate(POOL_WINDOWS):
        cols = slice(g * POOL_GC, (g + 1) * POOL_GC)
        s = u_ext[:, cols]
        span = 1
        while span < w:
            s = s + pltpu.roll(s, span, 0)
            span *= 2
        cnt = jnp.minimum(w, pos + 1).astype(F32)
        d = s[HIST_ROWS:, :] / cnt - u[:, cols]
        y = _dot(d.astype(BF16), pool_w_ref[g].astype(BF16)) * pool_scale_ref[:, cols]
        pooled.append(y.astype(BF16))
    return pooled


def _pool_state(u_ext, tm):
    return u_ext[tm + HIST_ROWS - POOL_HIST:tm + HIST_ROWS, :]


def _strict_upper(n):
    r = lax.broadcasted_iota(jnp.int32, (n, n), 0)
    c = lax.broadcasted_iota(jnp.int32, (n, n), 1)
    return (r > c).astype(BF16)


def _causal_mask(n):
    r = lax.broadcasted_iota(jnp.int32, (n, n), 0)
    c = lax.broadcasted_iota(jnp.int32, (n, n), 1)
    return c < r


def _store_masked_queries(q, qm_ref):
    m = q.shape[0]
    lower_half = lax.broadcasted_iota(jnp.int32, (m, LANES), 1) < SB_HEAD_DIM
    for p in range(HEAD_PAIRS):
        q2 = q[:, p * LANES:(p + 1) * LANES]
        zero = jnp.zeros_like(q2)
        qm_ref[p, 0:m, :] = jnp.where(lower_half, q2, zero)
        qm_ref[p, m:2 * m, :] = jnp.where(lower_half, zero, q2)


def _sb_all_heads(qm_ref, kv_block, decay_ref, acc_ref, upper, mask, first, key_minor=False, extra_decay=None,
                  fillers=(), rows=None):
    fillers = list(fillers)
    tile_rows = qm_ref.shape[1] // 2
    r0, r1 = (0, tile_rows) if rows is None else rows
    m = r1 - r0
    lower_half = lax.broadcasted_iota(jnp.int32, (m, LANES), 1) < SB_HEAD_DIM
    mask2 = None if mask is None else jnp.concatenate([mask, mask], axis=0)
    kv = [kv_block(p) for p in range(HEAD_PAIRS)]

    def queries(p):
        if rows is None:
            return qm_ref[p]
        return jnp.concatenate([qm_ref[p, r0:r1, :], qm_ref[p, tile_rows + r0:tile_rows + r1, :]], axis=0)

    logit, sps, first_terms = [], [], []
    for p in range(HEAD_PAIRS):
        z = (_dot if key_minor else _dot_nt)(queries(p), kv[p][0])
        sp = jnp.maximum(z, jnp.log(1.0 + jnp.exp2(jnp.minimum(z, SP_CLAMP))) * LOG2E)
        spm = sp if mask2 is None else jnp.where(mask2, sp, 0.0)
        sps.append(spm.astype(BF16))
        logit.append(z - sp)
        first_terms.append(spm[:, 0:1])
        filler = fillers.pop(0) if fillers else None
        if filler is not None:
            filler()

    later = _dot(jnp.concatenate(sps, axis=0), upper)

    least = None
    for p in range(HEAD_PAIRS):
        later_p = later[2 * m * p:2 * m * (p + 1)]
        total = later_p[:, 0:1] + first_terms[p]
        a = jnp.exp2(logit[p] - later_p)
        if mask2 is not None:
            a = jnp.where(mask2, a, 0.0)
        pv = (_dot_nt if key_minor else _dot)(a.astype(BF16), kv[p][1])
        merged = jnp.where(lower_half, pv[0:m], pv[m:2 * m])
        rowsum = jnp.where(lower_half, total[0:m], total[m:2 * m])
        if first:
            decay = rowsum
            acc_ref[p, r0:r1, :] = merged
        else:
            prev = decay_ref[p, r0:r1, :]
            if extra_decay is not None:
                prev = prev + extra_decay
            decay = prev + rowsum
            acc_ref[p, r0:r1, :] += merged * jnp.exp2(-prev)
        decay_ref[p, r0:r1, :] = decay
        least = decay if least is None else jnp.minimum(least, decay)
    return jnp.min(least)


def _more_blocks(state):
    j, least = state
    return jnp.logical_and(j >= 0, least < SKIP_LOG2)


def _store_heads(acc_ref, o_ref):
    for p in range(HEAD_PAIRS):
        o_ref[0, :, p * LANES:(p + 1) * LANES] = acc_ref[p].astype(BF16)


def _attn_scratch(m):
    return [pltpu.VMEM((HEAD_PAIRS, 2 * m, LANES), BF16), pltpu.VMEM((HEAD_PAIRS, m, LANES), F32),
            pltpu.VMEM((HEAD_PAIRS, m, LANES), F32)]


def _proj_attn_kernel(x_ref, hist_ref, w_in_f32_ref, pool_w_ref, pool_scale_ref,
                      wo_c, wg_c, wu_c, wd_c,
                      k_ref, v_ref, p_ref, pool_ref, a_ref, wo_b, wg_b, wu_b, wd_b,
                      w_in_ref, w_kvt_ref, u_ext, xb_ref, stage_q, stage_kvt, kvt_all, qm_ref, decay_ref, acc_ref,
                      *, tm, n_t, n_tiles, n_conv):
    s = pl.program_id(0)

    @pl.when(s < n_conv)
    def _():
        for c_ref, b_ref in ((wo_c, wo_b), (wg_c, wg_b), (wu_c, wu_b), (wd_c, wd_b)):
            b_ref[...] = c_ref[...].astype(BF16)

    @pl.when(s == 0)
    def _():
        w_in_ref[...] = w_in_f32_ref[...].astype(BF16)
        w_kvt_ref[...] = w_in_f32_ref[:, K_COLS:U_COLS].T.astype(BF16)
        stage_q[...] = jnp.zeros_like(stage_q)
        stage_kvt[...] = jnp.zeros_like(stage_kvt)
        u_ext[...] = jnp.zeros_like(u_ext)

    blk = ATTN_BLOCK
    upper = _strict_upper(blk)
    qi = lax.rem(jnp.maximum(s - 1, 0), n_t)
    b0 = (tm // blk) * qi
    lo, hi = (0, blk), (blk, tm)

    def kv_block(j):
        start = pl.multiple_of(j * blk, blk)
        return lambda p: (kvt_all[p * LANES:(p + 1) * LANES, pl.ds(start, blk)],
                          kvt_all[SB_WIDTH + p * LANES:SB_WIDTH + (p + 1) * LANES, pl.ds(start, blk)])

    def take_stage():
        kvt_all[:, pl.ds(pl.multiple_of(qi * tm, tm), tm)] = stage_kvt[...]
        _store_masked_queries(stage_q[...], qm_ref)

    def attend(block, rows, mask, first, fillers=(), extra_decay=None):
        return _sb_all_heads(qm_ref, kv_block(block), decay_ref, acc_ref, upper, mask, first, key_minor=True,
                             extra_decay=extra_decay, fillers=fillers, rows=rows)

    def attend_near(fillers):
        n = HEAD_PAIRS
        causal = _causal_mask(blk)
        attend(b0 + 1, hi, causal, True, fillers[0:n])
        attend(b0, lo, causal, True, fillers[n:2 * n])
        least_hi = attend(b0, hi, None, False, fillers[2 * n:3 * n])
        void = jnp.where(qi == 0, 2.0 * SKIP_LOG2, 0.0)
        least_lo = attend(jnp.maximum(b0 - 1, 0), lo, None, False, fillers[3 * n:4 * n], extra_decay=void)
        return least_lo, least_hi

    def attend_far(least_lo, least_hi):
        for rows, first_block, least in ((lo, b0 - 2, least_lo), (hi, b0 - 1, least_hi)):
            def body(state, rows=rows):
                j, _ = state
                return j - 1, attend(j, rows, None, False)

            lax.while_loop(_more_blocks, body, (first_block, least))
        _store_heads(acc_ref, a_ref)

    half = SB_WIDTH // 2

    def project_q(c, r):
        def piece():
            q = _proj_cols(xb_ref[r[0]:r[1], :], w_in_ref, Q_COLS + c, half) * Q_SCALE
            stage_q[r[0]:r[1], c:c + half] = q.astype(BF16)
        return piece

    def project_kv(out_ref, first_row, c, r):
        def piece():
            yt = _dot_nt(w_kvt_ref[first_row + c:first_row + c + half, :], xb_ref[r[0]:r[1], :])
            out_ref[0, c:c + half, r[0]:r[1]] = yt
            stage_kvt[first_row + c:first_row + c + half, r[0]:r[1]] = yt.astype(BF16)
        return piece

    u_parts = {}

    def project_u(c, r, last=False):
        def piece():
            u_parts[(r, c)] = _proj_cols(xb_ref[r[0]:r[1], :], w_in_ref, U_COLS + c, half)
            if last:
                u = jnp.concatenate([jnp.concatenate([u_parts[(rr, 0)], u_parts[(rr, half)]], axis=1)
                                     for rr in (lo, hi)], axis=0)
                pooled = _pool_mix(u, hist_ref[0], pool_w_ref, pool_scale_ref, u_ext, lax.rem(s, n_t), tm=tm,
                                   start_pos=0)
                for g, y in enumerate(pooled):
                    p_ref[0, :, g * POOL_GC:(g + 1) * POOL_GC] = y
                pool_ref[0] = _pool_state(u_ext, tm)
        return piece

    @pl.when(s < n_tiles)
    def _():
        take_stage()
        xb_ref[...] = x_ref[0].astype(BF16)
        pieces = [project_q(c, r) for r in (lo, hi) for c in (0, half)]
        pieces += [project_kv(k_ref, 0, c, r) for r in (lo, hi) for c in (0, half)]
        pieces += [project_kv(v_ref, SB_WIDTH, c, r) for r in (lo, hi) for c in (0, half)]
        pieces += [project_u(0, lo), project_u(half, lo), project_u(0, hi), project_u(half, hi, last=True)]
        attend_far(*attend_near(pieces))

    @pl.when(s == n_tiles)
    def _():
        take_stage()
        attend_far(*attend_near([]))


def _proj_attn_prompt(x, hist, w_in, pool_w, pool_scale2, ffn_weights, n_conv):
    B, T, _ = x.shape
    tm = 2 * ATTN_BLOCK
    assert T % tm == 0 and T >= POOL_HIST
    n_t = T // tm
    n_tiles = B * n_t
    assert n_conv <= n_tiles and all(w.shape[0] % (n_conv * BF16_SUBLANES) == 0 for w in ffn_weights)
    chunked = [pl.BlockSpec((w.shape[0] // n_conv, w.shape[1]), lambda s: (jnp.minimum(s, n_conv - 1), 0))
               for w in ffn_weights]
    kern = functools.partial(_proj_attn_kernel, tm=tm, n_t=n_t, n_tiles=n_tiles, n_conv=n_conv)
    proj = lambda s: jnp.minimum(s, n_tiles - 1)
    attn = lambda s: jnp.maximum(s - 1, 0)
    const = lambda *shape: pl.BlockSpec(shape, lambda s: (0,) * len(shape), pipeline_mode=pl.Buffered(1))
    proj_rows = lambda width: pl.BlockSpec((1, tm, width), lambda s: (proj(s) // n_t, proj(s) % n_t, 0))
    proj_cols = pl.BlockSpec((1, SB_WIDTH, tm), lambda s: (proj(s) // n_t, 0, proj(s) % n_t))
    return pl.pallas_call(
        kern,
        grid=(n_tiles + 1,),
        in_specs=[
            proj_rows(D_MODEL),
            pl.BlockSpec((1, HIST_ROWS, POOL_WIDTH), lambda s: (proj(s) // n_t, 0, 0)),
            const(D_MODEL, 3 * SB_WIDTH + POOL_WIDTH),
            const(len(POOL_WINDOWS), POOL_GC, POOL_GC),
            const(1, POOL_WIDTH),
        ] + chunked,
        out_specs=[
            proj_cols,
            proj_cols,
            proj_rows(POOL_WIDTH),
            pl.BlockSpec((1, POOL_HIST, POOL_WIDTH), lambda s: (proj(s) // n_t, 0, 0)),
            pl.BlockSpec((1, tm, SB_WIDTH), lambda s: (attn(s) // n_t, attn(s) % n_t, 0)),
        ] + chunked,
        out_shape=[
            jax.ShapeDtypeStruct((B, SB_WIDTH, T), F32),
            jax.ShapeDtypeStruct((B, SB_WIDTH, T), F32),
            jax.ShapeDtypeStruct((B, T, POOL_WIDTH), BF16),
            jax.ShapeDtypeStruct((B, POOL_HIST, POOL_WIDTH), F32),
            jax.ShapeDtypeStruct((B, T, SB_WIDTH), BF16),
        ] + [jax.ShapeDtypeStruct(w.shape, BF16) for w in ffn_weights],
        scratch_shapes=[
            pltpu.VMEM(w_in.shape, BF16),
            pltpu.VMEM((2 * SB_WIDTH, D_MODEL), BF16),
            pltpu.VMEM((HIST_ROWS + tm, POOL_WIDTH), F32),
            pltpu.VMEM((tm, D_MODEL), BF16),
            pltpu.VMEM((tm, SB_WIDTH), BF16),
            pltpu.VMEM((2 * SB_WIDTH, tm), BF16),
            pltpu.VMEM((2 * SB_WIDTH, T), BF16),
        ] + _attn_scratch(tm),
        compiler_params=pltpu.CompilerParams(
            dimension_semantics=("arbitrary",), vmem_limit_bytes=VMEM_LIMIT),
        name="proj_attn_prompt",
    )(x, hist, w_in, pool_w, pool_scale2, *ffn_weights)


def _proj_attn_sample_kernel(x_ref, hist_ref, w_in_f32_ref, pool_w_ref, pool_scale_ref,
                             ck_ref, cv_ref, ck_hbm, cv_hbm,
                             k_ref, v_ref, p_ref, pool_ref, a_ref,
                             w_in_ref, q_all, kv_new, u_all, u_ext, ck_buf, cv_buf, qm_ref, decay_ref, acc_ref,
                             *, t_new, past, blk):
    b = pl.program_id(0)
    rows = pl.ds(pl.multiple_of(b * t_new, t_new), t_new)

    @pl.when(b == 0)
    def _():
        w_in_ref[...] = w_in_f32_ref[...].astype(BF16)
        u_ext[...] = jnp.zeros_like(u_ext)
        x = x_ref[...].astype(BF16)
        q_all[...] = (_proj_cols(x, w_in_ref, Q_COLS, SB_WIDTH) * Q_SCALE).astype(BF16)
        k = _proj_cols(x, w_in_ref, K_COLS, SB_WIDTH)
        k_ref[...] = k
        kv_new[:, 0:SB_WIDTH] = k.astype(BF16)
        v = _proj_cols(x, w_in_ref, V_COLS, SB_WIDTH)
        v_ref[...] = v
        kv_new[:, SB_WIDTH:2 * SB_WIDTH] = v.astype(BF16)
        u_all[...] = _proj_cols(x, w_in_ref, U_COLS, POOL_WIDTH)

    pooled = _pool_mix(u_all[rows, :], hist_ref[0], pool_w_ref, pool_scale_ref, u_ext, 0, tm=t_new, start_pos=past)
    for g, y in enumerate(pooled):
        p_ref[rows, g * POOL_GC:(g + 1) * POOL_GC] = y
    pool_ref[b] = _pool_state(u_ext, t_new)

    _store_masked_queries(q_all[rows, :], qm_ref)
    new_kv = lambda p: (kv_new[rows, p * LANES:(p + 1) * LANES],
                        kv_new[rows, SB_WIDTH + p * LANES:SB_WIDTH + (p + 1) * LANES])
    _sb_all_heads(qm_ref, new_kv, decay_ref, acc_ref, _strict_upper(t_new), _causal_mask(t_new), True)
    upper = _strict_upper(blk)
    recent = lambda p: (ck_ref[0, p * LANES:(p + 1) * LANES, :].astype(BF16),
                        cv_ref[0, p * LANES:(p + 1) * LANES, :].astype(BF16))
    least = _sb_all_heads(qm_ref, recent, decay_ref, acc_ref, upper, None, False, key_minor=True)

    def body(state):
        j, _ = state
        start = pl.multiple_of(j * blk, blk)
        pltpu.sync_copy(ck_hbm.at[b, :, pl.ds(start, blk)], ck_buf)
        pltpu.sync_copy(cv_hbm.at[b, :, pl.ds(start, blk)], cv_buf)
        older = lambda p: (ck_buf[p * LANES:(p + 1) * LANES, :].astype(BF16),
                           cv_buf[p * LANES:(p + 1) * LANES, :].astype(BF16))
        return j - 1, _sb_all_heads(qm_ref, older, decay_ref, acc_ref, upper, None, False, key_minor=True)

    lax.while_loop(_more_blocks, body, (past // blk - 2, least))
    for p in range(HEAD_PAIRS):
        a_ref[rows, p * LANES:(p + 1) * LANES] = acc_ref[p].astype(BF16)


def _proj_attn_sample(x, hist, w_in, pool_w, pool_scale2, cache_kt, cache_vt):
    B, t_new, _ = x.shape
    past = cache_kt.shape[2]
    blk = ATTN_BLOCK
    assert past % blk == 0 and t_new >= HIST_ROWS
    n = B * t_new
    kern = functools.partial(_proj_attn_sample_kernel, t_new=t_new, past=past, blk=blk)
    const = lambda *shape: pl.BlockSpec(shape, lambda b: (0,) * len(shape), pipeline_mode=pl.Buffered(1))
    whole = lambda *shape: pl.BlockSpec(shape, lambda b: (0,) * len(shape))
    recent = pl.BlockSpec((1, SB_WIDTH, blk), lambda b: (b, 0, past // blk - 1))
    k, v, p, pool, a = pl.pallas_call(
        kern,
        grid=(B,),
        in_specs=[
            const(n, D_MODEL),
            pl.BlockSpec((1, HIST_ROWS, POOL_WIDTH), lambda b: (b, 0, 0)),
            const(D_MODEL, 3 * SB_WIDTH + POOL_WIDTH),
            const(len(POOL_WINDOWS), POOL_GC, POOL_GC),
            const(1, POOL_WIDTH),
            recent, recent,
            pl.BlockSpec(memory_space=pl.ANY), pl.BlockSpec(memory_space=pl.ANY),
        ],
        out_specs=[whole(n, SB_WIDTH), whole(n, SB_WIDTH), whole(n, POOL_WIDTH),
                   whole(B, POOL_HIST, POOL_WIDTH), whole(n, SB_WIDTH)],
        out_shape=[
            jax.ShapeDtypeStruct((n, SB_WIDTH), F32),
            jax.ShapeDtypeStruct((n, SB_WIDTH), F32),
            jax.ShapeDtypeStruct((n, POOL_WIDTH), BF16),
            jax.ShapeDtypeStruct((B, POOL_HIST, POOL_WIDTH), F32),
            jax.ShapeDtypeStruct((n, SB_WIDTH), BF16),
        ],
        scratch_shapes=[
            pltpu.VMEM(w_in.shape, BF16),
            pltpu.VMEM((n, SB_WIDTH), BF16),
            pltpu.VMEM((n, 2 * SB_WIDTH), BF16),
            pltpu.VMEM((n, POOL_WIDTH), F32),
            pltpu.VMEM((HIST_ROWS + t_new, POOL_WIDTH), F32),
            pltpu.VMEM((SB_WIDTH, blk), F32),
            pltpu.VMEM((SB_WIDTH, blk), F32),
        ] + _attn_scratch(t_new),
        compiler_params=pltpu.CompilerParams(
            dimension_semantics=("arbitrary",), vmem_limit_bytes=VMEM_LIMIT),
        name="proj_attn_sample",
    )(x.reshape(n, D_MODEL), hist, w_in, pool_w, pool_scale2, cache_kt, cache_vt, cache_kt, cache_vt)
    return k, v, p, pool, a


def _layer_norm(x, g, b):
    mu = jnp.mean(x, axis=-1, keepdims=True)
    xc = x - mu
    var = jnp.mean(xc * xc, axis=-1, keepdims=True)
    return xc * lax.rsqrt(var + LN_EPS) * g + b


def _ffn_rows(x_ref, a_ref, p_ref, wo_ref, g1, b1, wg_ref, wu_ref, wd_ref, acc_ref, x1b_ref, filler):
    m = x_ref.shape[0]
    halves = tuple(slice(i * m // FFN_PARTS, (i + 1) * m // FFN_PARTS) for i in range(FFN_PARTS))
    chunks = [(sum(FF_CHUNKS[:i]), w) for i, w in enumerate(FF_CHUNKS)]

    def swiglu_chunk(x1b, chunk):
        c, w = chunk
        gate = _dot(x1b, wg_ref[:, c:c + w])
        up = _dot(x1b, wu_ref[:, c:c + w])
        hmid = (gate * jax.nn.sigmoid(gate) * up).astype(BF16)
        return _dot(hmid, wd_ref[c:c + w, :])

    mix = [_dot(a_ref[r, :], wo_ref[0:SB_WIDTH, :]) + _dot(p_ref[r, :], wo_ref[SB_WIDTH:, :]) for r in halves]
    for r, mx in zip(halves, mix):
        x1 = _layer_norm(ALPHA * x_ref[r, :] + mx, g1, b1)
        x1b_ref[r, :] = x1.astype(BF16)
        acc_ref[r, :] = ALPHA * x1 + swiglu_chunk(x1b_ref[r, :], chunks[0])
    total = None
    for j, c in enumerate(chunks[1:]):
        anchor = filler(j, len(chunks) - 1)
        x1b_ref[0:anchor.shape[0], 0:anchor.shape[1]] += anchor.astype(BF16)
        total = acc_ref[...] + swiglu_chunk(x1b_ref[...], c)
        if c is not chunks[-1]:
            acc_ref[...] = total
    return total


def _out_ffn_kernel(xp_ref, ap_ref, pp_ref, xs_ref, as_ref, ps_ref, wo_s, wg_s, wu_s, wd_s,
                    g1_ref, b1_ref, g2_ref, b2_ref, yp_ref, ys_ref,
                    accp_ref, accs_ref, pre_ref, x1b_ref, *, n_tiles):
    i = pl.program_id(0)

    @pl.when(i == 0)
    def _():
        pre_ref[...] = jnp.zeros_like(pre_ref)

    def finish_previous_tile(j, n):
        rows = yp_ref.shape[0] // n
        r = slice(j * rows, (j + 1) * rows)
        y = _layer_norm(pre_ref[r, :], g2_ref[...], b2_ref[...])
        yp_ref[r, :] = y
        top = jnp.max(y.reshape(rows // BF16_SUBLANES, BF16_SUBLANES, D_MODEL), axis=0)
        top = functools.reduce(jnp.maximum, [top[:, c:c + LANES] for c in range(0, D_MODEL, LANES)])
        return jnp.where(top > F32_MAX, top, 0.0)

    def rows(x_ref, a_ref, p_ref, acc_ref):
        m = x_ref.shape[0]
        return _ffn_rows(x_ref, a_ref, p_ref, wo_s, g1_ref[...], b1_ref[...], wg_s, wu_s, wd_s, acc_ref,
                         x1b_ref.at[0:m, :], finish_previous_tile)

    @pl.when(i < n_tiles)
    def _():
        pre_ref[...] = rows(xp_ref, ap_ref, pp_ref, accp_ref)

    @pl.when(i == n_tiles)
    def _():
        ys_ref[...] = _layer_norm(rows(xs_ref, as_ref, ps_ref, accs_ref), g2_ref[...], b2_ref[...])


def _out_ffn(xp, ap, pp, xs, a_s, ps, wo_b, wg_b, wu_b, wd_b, g1, b1, g2, b2, *, tm):
    Np, Ns = xp.shape[0], xs.shape[0]
    assert Np % tm == 0 and sum(FF_CHUNKS) == D_FF
    n_tiles = Np // tm
    tile = lambda i: jnp.clip(i, 0, n_tiles - 1)
    row = lambda width: pl.BlockSpec((tm, width), lambda i: (tile(i), 0))
    once = lambda *shape: pl.BlockSpec(shape, lambda i: (0,) * len(shape), pipeline_mode=pl.Buffered(1))
    kern = functools.partial(_out_ffn_kernel, n_tiles=n_tiles)
    return pl.pallas_call(
        kern,
        grid=(n_tiles + 1,),
        in_specs=[
            row(D_MODEL), row(SB_WIDTH), row(POOL_WIDTH),
            once(Ns, D_MODEL), once(Ns, SB_WIDTH), once(Ns, POOL_WIDTH),
            once(*wo_b.shape), once(*wg_b.shape), once(*wu_b.shape), once(*wd_b.shape),
            once(1, D_MODEL), once(1, D_MODEL), once(1, D_MODEL), once(1, D_MODEL),
        ],
        out_specs=[pl.BlockSpec((tm, D_MODEL), lambda i: (tile(i - 1), 0)),
                   pl.BlockSpec((Ns, D_MODEL), lambda i: (0, 0))],
        out_shape=[jax.ShapeDtypeStruct((Np, D_MODEL), F32), jax.ShapeDtypeStruct((Ns, D_MODEL), F32)],
        scratch_shapes=[pltpu.VMEM((tm, D_MODEL), F32), pltpu.VMEM((Ns, D_MODEL), F32),
                        pltpu.VMEM((tm, D_MODEL), F32), pltpu.VMEM((tm, D_MODEL), BF16)],
        compiler_params=pltpu.CompilerParams(
            dimension_semantics=("arbitrary",), vmem_limit_bytes=VMEM_LIMIT),
        name="out_proj_ffn",
    )(xp, ap, pp, xs, a_s, ps, wo_b, wg_b, wu_b, wd_b, g1, b1, g2, b2)


def kernel(x_prompt, x_sample, cache_k, cache_v, state_pool, w_in, pool_w, pool_scale, w_out,
           ln1_g, ln1_b, w_gate, w_up, w_down, ln2_g, ln2_b):
    Bp, Tp, _ = x_prompt.shape
    Bs, Ts, _ = x_sample.shape
    past = cache_k.shape[1]

    row2 = lambda a: a.reshape(1, -1).astype(F32)
    pool_scale2 = row2(pool_scale)

    hist_s = jnp.concatenate([jnp.zeros((Bs, HIST_ROWS - POOL_HIST, POOL_WIDTH), F32), state_pool.astype(F32)], axis=1)
    feature_major = lambda c: jnp.transpose(c, (0, 2, 3, 1)).reshape(Bs, SB_WIDTH, past)
    k_s, v_s, mix_s, pool_s, a_s = _proj_attn_sample(x_sample, hist_s, w_in, pool_w, pool_scale2,
                                                     feature_major(cache_k), feature_major(cache_v))

    hist_p = jnp.zeros((Bp, HIST_ROWS, POOL_WIDTH), F32)
    kt_p, vt_p, mix_p, pool_p, a_p, wo_b, wg_b, wu_b, wd_b = _proj_attn_prompt(
        x_prompt, hist_p, w_in, pool_w, pool_scale2, ffn_weights=(w_out, w_gate, w_up, w_down), n_conv=16)
    position_major = lambda c: jnp.transpose(c.reshape(Bp, SB_HEADS, SB_HEAD_DIM, Tp), (0, 3, 1, 2))

    y_p, y_s = _out_ffn(
        x_prompt.reshape(Bp * Tp, D_MODEL), a_p.reshape(Bp * Tp, SB_WIDTH), mix_p.reshape(Bp * Tp, POOL_WIDTH),
        x_sample.reshape(Bs * Ts, D_MODEL), a_s, mix_s,
        wo_b, wg_b, wu_b, wd_b, row2(ln1_g), row2(ln1_b), row2(ln2_g), row2(ln2_b), tm=512)

    heads = lambda a: a.reshape(Bs, Ts, SB_HEADS, SB_HEAD_DIM)
    return (y_p.reshape(Bp, Tp, D_MODEL), y_s.reshape(Bs, Ts, D_MODEL),
            position_major(kt_p), position_major(vt_p), pool_p, heads(k_s), heads(v_s), pool_s)
```

```python
import functools

import jax
import jax.numpy as jnp
from jax import lax
from jax.experimental import pallas as pl
from jax.experimental.pallas import tpu as pltpu

D_MODEL = 1024
SB_WIDTH = 512
SB_HEADS = 8
SB_HEAD_DIM = 64
POOL_WIDTH = 512
POOL_WINDOWS = (2, 4, 8, 16)
POOL_GC = 128
POOL_HIST = 15
HIST_ROWS = 16
D_FF = 2816
DEPTH = 1
ALPHA = (2 * DEPTH) ** 0.25
LN_EPS = 1e-5
LANES = 128
HEAD_PAIRS = SB_WIDTH // LANES
LOG2E = 1.4426950408889634
Q_SCALE = SB_HEAD_DIM ** -0.5 * LOG2E
SKIP_LOG2 = 160.0
SP_CLAMP = 32.0
ATTN_BLOCK = 256
FF_CHUNKS = (256, 768, 768, 768, 256)
FFN_PARTS = 2
BF16_SUBLANES = 16
VMEM_LIMIT = 56 * 1024 * 1024

F32 = jnp.float32
BF16 = jnp.bfloat16
F32_MAX = 3.4028234663852886e38


def _dot(a, b):
    return jnp.dot(a, b, preferred_element_type=F32)


def _dot_nt(a, b):
    return lax.dot_general(a, b, (((1,), (1,)), ((), ())), preferred_element_type=F32)


Q_COLS, K_COLS, V_COLS, U_COLS = (SB_WIDTH * i for i in range(4))


def _proj_cols(x, w_in_ref, start, width):
    return _dot(x, w_in_ref[:, start:start + width])


def _pool_mix(u, hist, pool_w_ref, pool_scale_ref, u_ext, t, *, tm, start_pos):
    u_ext[0:HIST_ROWS, :] = jnp.where(t == 0, hist, u_ext[tm:tm + HIST_ROWS, :])
    u_ext[HIST_ROWS:HIST_ROWS + tm, :] = u

    pos = start_pos + t * tm + lax.broadcasted_iota(jnp.int32, (tm, 1), 0)
    pooled = []
    for g, w in enumerate(POOL_WINDOWS):
        cols = slice(g * POOL_GC, (g + 1) * POOL_GC)
        s = u_ext[:, cols]
        span = 1
        while span < w:
            s = s + pltpu.roll(s, span, 0)
            span *= 2
        cnt = jnp.minimum(w, pos + 1).astype(F32)
        d = s[HIST_ROWS:, :] / cnt - u[:, cols]
        y = _dot(d.astype(BF16), pool_w_ref[g].astype(BF16)) * pool_scale_ref[:, cols]
        pooled.append(y.astype(BF16))
    return pooled


def _pool_state(u_ext, tm):
    return u_ext[tm + HIST_ROWS - POOL_HIST:tm + HIST_ROWS, :]


def _strict_upper(n):
    r = lax.broadcasted_iota(jnp.int32, (n, n), 0)
    c = lax.broadcasted_iota(jnp.int32, (n, n), 1)
    return (r > c).astype(BF16)


def _causal_mask(n, first_row=0, rows=None):
    rows = n if rows is None else rows
    r = lax.broadcasted_iota(jnp.int32, (rows, n), 0) + first_row
    c = lax.broadcasted_iota(jnp.int32, (rows, n), 1)
    return c < r


def _store_masked_queries(q, qm_ref):
    m = q.shape[0]
    lower_half = lax.broadcasted_iota(jnp.int32, (m, LANES), 1) < SB_HEAD_DIM
    for p in range(HEAD_PAIRS):
        q2 = q[:, p * LANES:(p + 1) * LANES]
        zero = jnp.zeros_like(q2)
        qm_ref[p, 0:m, :] = jnp.where(lower_half, q2, zero)
        qm_ref[p, m:2 * m, :] = jnp.where(lower_half, zero, q2)


def _sb_all_heads(qm_ref, kv_block, decay_ref, acc_ref, upper, mask, first, key_minor=False, extra_decay=None,
                  fillers=(), rows=None):
    fillers = list(fillers)
    tile_rows = qm_ref.shape[1] // 2
    r0, r1 = (0, tile_rows) if rows is None else rows
    m = r1 - r0
    lower_half = lax.broadcasted_iota(jnp.int32, (m, LANES), 1) < SB_HEAD_DIM
    mask2 = None if mask is None else jnp.concatenate([mask, mask], axis=0)
    kv = [kv_block(p) for p in range(HEAD_PAIRS)]

    def queries(p):
        if rows is None:
            return qm_ref[p]
        return jnp.concatenate([qm_ref[p, r0:r1, :], qm_ref[p, tile_rows + r0:tile_rows + r1, :]], axis=0)

    logit, sps, rowsums = [], [], []
    for p in range(HEAD_PAIRS):
        z = (_dot if key_minor else _dot_nt)(queries(p), kv[p][0])
        sp = jnp.maximum(z, jnp.log(1.0 + jnp.exp2(jnp.minimum(z, SP_CLAMP))) * LOG2E)
        spm = sp if mask2 is None else jnp.where(mask2, sp, 0.0)
        sps.append(spm.astype(BF16))
        logit.append(z - sp)
        rowsums.append(jnp.sum(spm, axis=1, keepdims=True))
        filler = fillers.pop(0) if fillers else None
        if filler is not None:
            filler()

    later = _dot(jnp.concatenate(sps, axis=0), upper)

    least = None
    for p in range(HEAD_PAIRS):
        a = jnp.exp2(logit[p] - later[2 * m * p:2 * m * (p + 1)])
        if mask2 is not None:
            a = jnp.where(mask2, a, 0.0)
        pv = (_dot_nt if key_minor else _dot)(a.astype(BF16), kv[p][1])
        merged = jnp.where(lower_half, pv[0:m], pv[m:2 * m])
        rowsum = jnp.where(lower_half, rowsums[p][0:m], rowsums[p][m:2 * m])
        if first:
            decay = rowsum
            acc_ref[p, r0:r1, :] = merged
        else:
            prev = decay_ref[p, r0:r1, :]
            if extra_decay is not None:
                prev = prev + extra_decay
            decay = prev + rowsum
            acc_ref[p, r0:r1, :] += merged * jnp.exp2(-prev)
        decay_ref[p, r0:r1, :] = decay
        least = decay if least is None else jnp.minimum(least, decay)
    return jnp.min(least)


def _more_blocks(state):
    j, least = state
    return jnp.logical_and(j >= 0, least < SKIP_LOG2)


def _store_heads(acc_ref, o_ref):
    for p in range(HEAD_PAIRS):
        o_ref[0, :, p * LANES:(p + 1) * LANES] = acc_ref[p].astype(BF16)


def _attn_scratch(m):
    return [pltpu.VMEM((HEAD_PAIRS, 2 * m, LANES), BF16), pltpu.VMEM((HEAD_PAIRS, m, LANES), F32),
            pltpu.VMEM((HEAD_PAIRS, m, LANES), F32)]


def _proj_attn_kernel(x_ref, hist_ref, w_in_f32_ref, pool_w_ref, pool_scale_ref,
                      wo_c, wg_c, wu_c, wd_c,
                      k_ref, v_ref, p_ref, pool_ref, a_ref, wo_b, wg_b, wu_b, wd_b,
                      w_in_ref, w_kvt_ref, u_ext, xb_ref, stage_q, stage_kvt, kvt_all, qm_ref, decay_ref, acc_ref,
                      *, tm, n_t, n_tiles, n_conv):
    s = pl.program_id(0)

    @pl.when(s < n_conv)
    def _():
        for c_ref, b_ref in ((wo_c, wo_b), (wg_c, wg_b), (wu_c, wu_b), (wd_c, wd_b)):
            b_ref[...] = c_ref[...].astype(BF16)

    @pl.when(s == 0)
    def _():
        w_in_ref[...] = w_in_f32_ref[...].astype(BF16)
        w_kvt_ref[...] = w_in_f32_ref[:, K_COLS:U_COLS].T.astype(BF16)
        stage_q[...] = jnp.zeros_like(stage_q)
        stage_kvt[...] = jnp.zeros_like(stage_kvt)
        u_ext[...] = jnp.zeros_like(u_ext)

    blk = ATTN_BLOCK
    upper = _strict_upper(blk)
    qi = lax.rem(jnp.maximum(s - 1, 0), n_t)
    b0 = (tm // blk) * qi
    lo, hi = (0, blk), (blk, tm)

    def kv_block(j, keys=blk):
        start = pl.multiple_of(j * blk, blk)
        return lambda p: (kvt_all[p * LANES:(p + 1) * LANES, pl.ds(start, keys)],
                          kvt_all[SB_WIDTH + p * LANES:SB_WIDTH + (p + 1) * LANES, pl.ds(start, keys)])

    def take_stage():
        kvt_all[:, pl.ds(pl.multiple_of(qi * tm, tm), tm)] = stage_kvt[...]
        _store_masked_queries(stage_q[...], qm_ref)

    def attend(block, rows, mask, first, fillers=(), extra_decay=None, keys=blk):
        return _sb_all_heads(qm_ref, kv_block(block, keys), decay_ref, acc_ref,
                             upper if keys == blk else _strict_upper(keys), mask, first, key_minor=True,
                             extra_decay=extra_decay, fillers=fillers, rows=rows)

    def attend_own_block(block, rows, fillers):
        sub = blk // 2
        r0, r1 = rows
        f = list(fillers) or [None] * HEAD_PAIRS
        attend(block, (r0 + sub, r1), _causal_mask(blk, sub, sub), True, [f[0], None, f[1], None])
        attend(block, (r0, r0 + sub), _causal_mask(sub), True, [f[2], None, f[3], None], keys=sub)

    def attend_near(fillers):
        n = HEAD_PAIRS
        attend_own_block(b0 + 1, hi, fillers[0:n])
        attend_own_block(b0, lo, fillers[n:2 * n])
        least_hi = attend(b0, hi, None, False, fillers[2 * n:3 * n])
        void = jnp.where(qi == 0, 2.0 * SKIP_LOG2, 0.0)
        least_lo = attend(jnp.maximum(b0 - 1, 0), lo, None, False, fillers[3 * n:4 * n], extra_decay=void)
        return least_lo, least_hi

    def attend_far(least_lo, least_hi):
        for rows, first_block, least in ((lo, b0 - 2, least_lo), (hi, b0 - 1, least_hi)):
            def body(state, rows=rows):
                j, _ = state
                return j - 1, attend(j, rows, None, False)

            lax.while_loop(_more_blocks, body, (first_block, least))
        _store_heads(acc_ref, a_ref)

    half = SB_WIDTH // 2

    def project_q(c, r):
        def piece():
            q = _proj_cols(xb_ref[r[0]:r[1], :], w_in_ref, Q_COLS + c, half) * Q_SCALE
            stage_q[r[0]:r[1], c:c + half] = q.astype(BF16)
        return piece

    def project_kv(out_ref, first_row, c, r):
        def piece():
            yt = _dot_nt(w_kvt_ref[first_row + c:first_row + c + half, :], xb_ref[r[0]:r[1], :])
            out_ref[0, c:c + half, r[0]:r[1]] = yt
            stage_kvt[first_row + c:first_row + c + half, r[0]:r[1]] = yt.astype(BF16)
        return piece

    u_parts = {}

    def project_u(c, r, last=False):
        def piece():
            u_parts[(r, c)] = _proj_cols(xb_ref[r[0]:r[1], :], w_in_ref, U_COLS + c, half)
            if last:
                u = jnp.concatenate([jnp.concatenate([u_parts[(rr, 0)], u_parts[(rr, half)]], axis=1)
                                     for rr in (lo, hi)], axis=0)
                pooled = _pool_mix(u, hist_ref[0], pool_w_ref, pool_scale_ref, u_ext, lax.rem(s, n_t), tm=tm,
                                   start_pos=0)
                for g, y in enumerate(pooled):
                    p_ref[0, :, g * POOL_GC:(g + 1) * POOL_GC] = y
                pool_ref[0] = _pool_state(u_ext, tm)
        return piece

    @pl.when(s < n_tiles)
    def _():
        take_stage()
        xb_ref[...] = x_ref[0].astype(BF16)
        pieces = [project_q(c, r) for r in (lo, hi) for c in (0, half)]
        pieces += [project_kv(k_ref, 0, c, r) for r in (lo, hi) for c in (0, half)]
        pieces += [project_kv(v_ref, SB_WIDTH, c, r) for r in (lo, hi) for c in (0, half)]
        pieces += [project_u(0, lo), project_u(half, lo), project_u(0, hi), project_u(half, hi, last=True)]
        attend_far(*attend_near(pieces))

    @pl.when(s == n_tiles)
    def _():
        take_stage()
        attend_far(*attend_near([]))


def _proj_attn_prompt(x, hist, w_in, pool_w, pool_scale2, ffn_weights, n_conv):
    B, T, _ = x.shape
    tm = 2 * ATTN_BLOCK
    assert T % tm == 0 and T >= POOL_HIST
    n_t = T // tm
    n_tiles = B * n_t
    assert n_conv <= n_tiles and all(w.shape[0] % (n_conv * BF16_SUBLANES) == 0 for w in ffn_weights)
    chunked = [pl.BlockSpec((w.shape[0] // n_conv, w.shape[1]), lambda s: (jnp.minimum(s, n_conv - 1), 0))
               for w in ffn_weights]
    kern = functools.partial(_proj_attn_kernel, tm=tm, n_t=n_t, n_tiles=n_tiles, n_conv=n_conv)
    proj = lambda s: jnp.minimum(s, n_tiles - 1)
    attn = lambda s: jnp.maximum(s - 1, 0)
    const = lambda *shape: pl.BlockSpec(shape, lambda s: (0,) * len(shape), pipeline_mode=pl.Buffered(1))
    proj_rows = lambda width: pl.BlockSpec((1, tm, width), lambda s: (proj(s) // n_t, proj(s) % n_t, 0))
    proj_cols = pl.BlockSpec((1, SB_WIDTH, tm), lambda s: (proj(s) // n_t, 0, proj(s) % n_t))
    return pl.pallas_call(
        kern,
        grid=(n_tiles + 1,),
        in_specs=[
            proj_rows(D_MODEL),
            pl.BlockSpec((1, HIST_ROWS, POOL_WIDTH), lambda s: (proj(s) // n_t, 0, 0)),
            const(D_MODEL, 3 * SB_WIDTH + POOL_WIDTH),
            const(len(POOL_WINDOWS), POOL_GC, POOL_GC),
            const(1, POOL_WIDTH),
        ] + chunked,
        out_specs=[
            proj_cols,
            proj_cols,
            proj_rows(POOL_WIDTH),
            pl.BlockSpec((1, POOL_HIST, POOL_WIDTH), lambda s: (proj(s) // n_t, 0, 0)),
            pl.BlockSpec((1, tm, SB_WIDTH), lambda s: (attn(s) // n_t, attn(s) % n_t, 0)),
        ] + chunked,
        out_shape=[
            jax.ShapeDtypeStruct((B, SB_WIDTH, T), F32),
            jax.ShapeDtypeStruct((B, SB_WIDTH, T), F32),
            jax.ShapeDtypeStruct((B, T, POOL_WIDTH), BF16),
            jax.ShapeDtypeStruct((B, POOL_HIST, POOL_WIDTH), F32),
            jax.ShapeDtypeStruct((B, T, SB_WIDTH), BF16),
        ] + [jax.ShapeDtypeStruct(w.shape, BF16) for w in ffn_weights],
        scratch_shapes=[
            pltpu.VMEM(w_in.shape, BF16),
            pltpu.VMEM((2 * SB_WIDTH, D_MODEL), BF16),
            pltpu.VMEM((HIST_ROWS + tm, POOL_WIDTH), F32),
            pltpu.VMEM((tm, D_MODEL), BF16),
            pltpu.VMEM((tm, SB_WIDTH), BF16),
            pltpu.VMEM((2 * SB_WIDTH, tm), BF16),
            pltpu.VMEM((2 * SB_WIDTH, T), BF16),
        ] + _attn_scratch(tm),
        compiler_params=pltpu.CompilerParams(
            dimension_semantics=("arbitrary",), vmem_limit_bytes=VMEM_LIMIT),
        name="proj_attn_prompt",
    )(x, hist, w_in, pool_w, pool_scale2, *ffn_weights)


def _proj_attn_sample_kernel(x_ref, hist_ref, w_in_f32_ref, pool_w_ref, pool_scale_ref,
                             ck_ref, cv_ref, ck_hbm, cv_hbm,
                             k_ref, v_ref, p_ref, pool_ref, a_ref,
                             w_in_ref, q_all, kv_new, u_all, u_ext, ck_buf, cv_buf, qm_ref, decay_ref, acc_ref,
                             *, t_new, past, blk):
    b = pl.program_id(0)
    rows = pl.ds(pl.multiple_of(b * t_new, t_new), t_new)

    @pl.when(b == 0)
    def _():
        w_in_ref[...] = w_in_f32_ref[...].astype(BF16)
        u_ext[...] = jnp.zeros_like(u_ext)
        x = x_ref[...].astype(BF16)
        q_all[...] = (_proj_cols(x, w_in_ref, Q_COLS, SB_WIDTH) * Q_SCALE).astype(BF16)
        k = _proj_cols(x, w_in_ref, K_COLS, SB_WIDTH)
        k_ref[...] = k
        kv_new[:, 0:SB_WIDTH] = k.astype(BF16)
        v = _proj_cols(x, w_in_ref, V_COLS, SB_WIDTH)
        v_ref[...] = v
        kv_new[:, SB_WIDTH:2 * SB_WIDTH] = v.astype(BF16)
        u_all[...] = _proj_cols(x, w_in_ref, U_COLS, POOL_WIDTH)

    pooled = _pool_mix(u_all[rows, :], hist_ref[0], pool_w_ref, pool_scale_ref, u_ext, 0, tm=t_new, start_pos=past)
    for g, y in enumerate(pooled):
        p_ref[rows, g * POOL_GC:(g + 1) * POOL_GC] = y
    pool_ref[b] = _pool_state(u_ext, t_new)

    _store_masked_queries(q_all[rows, :], qm_ref)
    new_kv = lambda p: (kv_new[rows, p * LANES:(p + 1) * LANES],
                        kv_new[rows, SB_WIDTH + p * LANES:SB_WIDTH + (p + 1) * LANES])
    _sb_all_heads(qm_ref, new_kv, decay_ref, acc_ref, _strict_upper(t_new), _causal_mask(t_new), True)
    upper = _strict_upper(blk)
    recent = lambda p: (ck_ref[0, p * LANES:(p + 1) * LANES, :].astype(BF16),
                        cv_ref[0, p * LANES:(p + 1) * LANES, :].astype(BF16))
    least = _sb_all_heads(qm_ref, recent, decay_ref, acc_ref, upper, None, False, key_minor=True)

    def body(state):
        j, _ = state
        start = pl.multiple_of(j * blk, blk)
        pltpu.sync_copy(ck_hbm.at[b, :, pl.ds(start, blk)], ck_buf)
        pltpu.sync_copy(cv_hbm.at[b, :, pl.ds(start, blk)], cv_buf)
        older = lambda p: (ck_buf[p * LANES:(p + 1) * LANES, :].astype(BF16),
                           cv_buf[p * LANES:(p + 1) * LANES, :].astype(BF16))
        return j - 1, _sb_all_heads(qm_ref, older, decay_ref, acc_ref, upper, None, False, key_minor=True)

    lax.while_loop(_more_blocks, body, (past // blk - 2, least))
    for p in range(HEAD_PAIRS):
        a_ref[rows, p * LANES:(p + 1) * LANES] = acc_ref[p].astype(BF16)


def _proj_attn_sample(x, hist, w_in, pool_w, pool_scale2, cache_kt, cache_vt):
    B, t_new, _ = x.shape
    past = cache_kt.shape[2]
    blk = ATTN_BLOCK
    assert past % blk == 0 and t_new >= HIST_ROWS
    n = B * t_new
    kern = functools.partial(_proj_attn_sample_kernel, t_new=t_new, past=past, blk=blk)
    const = lambda *shape: pl.BlockSpec(shape, lambda b: (0,) * len(shape), pipeline_mode=pl.Buffered(1))
    whole = lambda *shape: pl.BlockSpec(shape, lambda b: (0,) * len(shape))
    recent = pl.BlockSpec((1, SB_WIDTH, blk), lambda b: (b, 0, past // blk - 1))
    k, v, p, pool, a = pl.pallas_call(
        kern,
        grid=(B,),
        in_specs=[
            const(n, D_MODEL),
            pl.BlockSpec((1, HIST_ROWS, POOL_WIDTH), lambda b: (b, 0, 0)),
            const(D_MODEL, 3 * SB_WIDTH + POOL_WIDTH),
            const(len(POOL_WINDOWS), POOL_GC, POOL_GC),
            const(1, POOL_WIDTH),
            recent, recent,
            pl.BlockSpec(memory_space=pl.ANY), pl.BlockSpec(memory_space=pl.ANY),
        ],
        out_specs=[whole(n, SB_WIDTH), whole(n, SB_WIDTH), whole(n, POOL_WIDTH),
                   whole(B, POOL_HIST, POOL_WIDTH), whole(n, SB_WIDTH)],
        out_shape=[
            jax.ShapeDtypeStruct((n, SB_WIDTH), F32),
            jax.ShapeDtypeStruct((n, SB_WIDTH), F32),
            jax.ShapeDtypeStruct((n, POOL_WIDTH), BF16),
            jax.ShapeDtypeStruct((B, POOL_HIST, POOL_WIDTH), F32),
            jax.ShapeDtypeStruct((n, SB_WIDTH), BF16),
        ],
        scratch_shapes=[
            pltpu.VMEM(w_in.shape, BF16),
            pltpu.VMEM((n, SB_WIDTH), BF16),
            pltpu.VMEM((n, 2 * SB_WIDTH), BF16),
            pltpu.VMEM((n, POOL_WIDTH), F32),
            pltpu.VMEM((HIST_ROWS + t_new, POOL_WIDTH), F32),
            pltpu.VMEM((SB_WIDTH, blk), F32),
            pltpu.VMEM((SB_WIDTH, blk), F32),
        ] + _attn_scratch(t_new),
        compiler_params=pltpu.CompilerParams(
            dimension_semantics=("arbitrary",), vmem_limit_bytes=VMEM_LIMIT),
        name="proj_attn_sample",
    )(x.reshape(n, D_MODEL), hist, w_in, pool_w, pool_scale2, cache_kt, cache_vt, cache_kt, cache_vt)
    return k, v, p, pool, a


def _layer_norm(x, g, b):
    mu = jnp.mean(x, axis=-1, keepdims=True)
    xc = x - mu
    var = jnp.mean(xc * xc, axis=-1, keepdims=True)
    return xc * lax.rsqrt(var + LN_EPS) * g + b


def _ffn_rows(x_ref, a_ref, p_ref, wo_ref, g1, b1, wg_ref, wu_ref, wd_ref, acc_ref, x1b_ref, filler):
    m = x_ref.shape[0]
    halves = tuple(slice(i * m // FFN_PARTS, (i + 1) * m // FFN_PARTS) for i in range(FFN_PARTS))
    chunks = [(sum(FF_CHUNKS[:i]), w) for i, w in enumerate(FF_CHUNKS)]

    def swiglu_chunk(x1b, chunk):
        c, w = chunk
        gate = _dot(x1b, wg_ref[:, c:c + w])
        up = _dot(x1b, wu_ref[:, c:c + w])
        hmid = (gate * jax.nn.sigmoid(gate) * up).astype(BF16)
        return _dot(hmid, wd_ref[c:c + w, :])

    mix = [_dot(a_ref[r, :], wo_ref[0:SB_WIDTH, :]) + _dot(p_ref[r, :], wo_ref[SB_WIDTH:, :]) for r in halves]
    for r, mx in zip(halves, mix):
        x1 = _layer_norm(ALPHA * x_ref[r, :] + mx, g1, b1)
        x1b_ref[r, :] = x1.astype(BF16)
        acc_ref[r, :] = ALPHA * x1 + swiglu_chunk(x1b_ref[r, :], chunks[0])
    total = None
    for j, c in enumerate(chunks[1:]):
        anchor = filler(j, len(chunks) - 1)
        x1b_ref[0:anchor.shape[0], 0:anchor.shape[1]] += anchor.astype(BF16)
        total = acc_ref[...] + swiglu_chunk(x1b_ref[...], c)
        if c is not chunks[-1]:
            acc_ref[...] = total
    return total


def _out_ffn_kernel(xp_ref, ap_ref, pp_ref, xs_ref, as_ref, ps_ref, wo_s, wg_s, wu_s, wd_s,
                    g1_ref, b1_ref, g2_ref, b2_ref, yp_ref, ys_ref,
                    accp_ref, accs_ref, pre_ref, x1b_ref, *, n_tiles):
    i = pl.program_id(0)

    @pl.when(i == 0)
    def _():
        pre_ref[...] = jnp.zeros_like(pre_ref)

    def finish_previous_tile(j, n):
        rows = yp_ref.shape[0] // n
        r = slice(j * rows, (j + 1) * rows)
        y = _layer_norm(pre_ref[r, :], g2_ref[...], b2_ref[...])
        yp_ref[r, :] = y
        top = jnp.max(y.reshape(rows // BF16_SUBLANES, BF16_SUBLANES, D_MODEL), axis=0)
        top = functools.reduce(jnp.maximum, [top[:, c:c + LANES] for c in range(0, D_MODEL, LANES)])
        return jnp.where(top > F32_MAX, top, 0.0)

    def rows(x_ref, a_ref, p_ref, acc_ref):
        m = x_ref.shape[0]
        return _ffn_rows(x_ref, a_ref, p_ref, wo_s, g1_ref[...], b1_ref[...], wg_s, wu_s, wd_s, acc_ref,
                         x1b_ref.at[0:m, :], finish_previous_tile)

    @pl.when(i < n_tiles)
    def _():
        pre_ref[...] = rows(xp_ref, ap_ref, pp_ref, accp_ref)

    @pl.when(i == n_tiles)
    def _():
        ys_ref[...] = _layer_norm(rows(xs_ref, as_ref, ps_ref, accs_ref), g2_ref[...], b2_ref[...])


def _out_ffn(xp, ap, pp, xs, a_s, ps, wo_b, wg_b, wu_b, wd_b, g1, b1, g2, b2, *, tm):
    Np, Ns = xp.shape[0], xs.shape[0]
    assert Np % tm == 0 and sum(FF_CHUNKS) == D_FF
    n_tiles = Np // tm
    tile = lambda i: jnp.clip(i, 0, n_tiles - 1)
    row = lambda width: pl.BlockSpec((tm, width), lambda i: (tile(i), 0))
    once = lambda *shape: pl.BlockSpec(shape, lambda i: (0,) * len(shape), pipeline_mode=pl.Buffered(1))
    kern = functools.partial(_out_ffn_kernel, n_tiles=n_tiles)
    return pl.pallas_call(
        kern,
        grid=(n_tiles + 1,),
        in_specs=[
            row(D_MODEL), row(SB_WIDTH), row(POOL_WIDTH),
            once(Ns, D_MODEL), once(Ns, SB_WIDTH), once(Ns, POOL_WIDTH),
            once(*wo_b.shape), once(*wg_b.shape), once(*wu_b.shape), once(*wd_b.shape),
            once(1, D_MODEL), once(1, D_MODEL), once(1, D_MODEL), once(1, D_MODEL),
        ],
        out_specs=[pl.BlockSpec((tm, D_MODEL), lambda i: (tile(i - 1), 0)),
                   pl.BlockSpec((Ns, D_MODEL), lambda i: (0, 0))],
        out_shape=[jax.ShapeDtypeStruct((Np, D_MODEL), F32), jax.ShapeDtypeStruct((Ns, D_MODEL), F32)],
        scratch_shapes=[pltpu.VMEM((tm, D_MODEL), F32), pltpu.VMEM((Ns, D_MODEL), F32),
                        pltpu.VMEM((tm, D_MODEL), F32), pltpu.VMEM((tm, D_MODEL), BF16)],
        compiler_params=pltpu.CompilerParams(
            dimension_semantics=("arbitrary",), vmem_limit_bytes=VMEM_LIMIT),
        name="out_proj_ffn",
    )(xp, ap, pp, xs, a_s, ps, wo_b, wg_b, wu_b, wd_b, g1, b1, g2, b2)


def kernel(x_prompt, x_sample, cache_k, cache_v, state_pool, w_in, pool_w, pool_scale, w_out,
           ln1_g, ln1_b, w_gate, w_up, w_down, ln2_g, ln2_b):
    Bp, Tp, _ = x_prompt.shape
    Bs, Ts, _ = x_sample.shape
    past = cache_k.shape[1]

    row2 = lambda a: a.reshape(1, -1).astype(F32)
    pool_scale2 = row2(pool_scale)

    hist_s = jnp.concatenate([jnp.zeros((Bs, HIST_ROWS - POOL_HIST, POOL_WIDTH), F32), state_pool.astype(F32)], axis=1)
    feature_major = lambda c: jnp.transpose(c, (0, 2, 3, 1)).reshape(Bs, SB_WIDTH, past)
    k_s, v_s, mix_s, pool_s, a_s = _proj_attn_sample(x_sample, hist_s, w_in, pool_w, pool_scale2,
                                                     feature_major(cache_k), feature_major(cache_v))

    hist_p = jnp.zeros((Bp, HIST_ROWS, POOL_WIDTH), F32)
    kt_p, vt_p, mix_p, pool_p, a_p, wo_b, wg_b, wu_b, wd_b = _proj_attn_prompt(
        x_prompt, hist_p, w_in, pool_w, pool_scale2, ffn_weights=(w_out, w_gate, w_up, w_down), n_conv=16)
    position_major = lambda c: jnp.transpose(c.reshape(Bp, SB_HEADS, SB_HEAD_DIM, Tp), (0, 3, 1, 2))

    y_p, y_s = _out_ffn(
        x_prompt.reshape(Bp * Tp, D_MODEL), a_p.reshape(Bp * Tp, SB_WIDTH), mix_p.reshape(Bp * Tp, POOL_WIDTH),
        x_sample.reshape(Bs * Ts, D_MODEL), a_s, mix_s,
        wo_b, wg_b, wu_b, wd_b, row2(ln1_g), row2(ln1_b), row2(ln2_g), row2(ln2_b), tm=512)

    heads = lambda a: a.reshape(Bs, Ts, SB_HEADS, SB_HEAD_DIM)
    return (y_p.reshape(Bp, Tp, D_MODEL), y_s.reshape(Bs, Ts, D_MODEL),
            position_major(kt_p), position_major(vt_p), pool_p, heads(k_s), heads(v_s), pool_s)
```

```python
import functools

import jax
import jax.numpy as jnp
from jax import lax
from jax.experimental import pallas as pl
from jax.experimental.pallas import tpu as pltpu

D_MODEL = 1024
SB_WIDTH = 512
SB_HEADS = 8
SB_HEAD_DIM = 64
POOL_WIDTH = 512
POOL_WINDOWS = (2, 4, 8, 16)
POOL_GC = 128
POOL_HIST = 15
HIST_ROWS = 16
D_FF = 2816
DEPTH = 1
ALPHA = (2 * DEPTH) ** 0.25
LN_EPS = 1e-5
LANES = 128
HEAD_PAIRS = SB_WIDTH // LANES
LOG2E = 1.4426950408889634
Q_SCALE = SB_HEAD_DIM ** -0.5 * LOG2E
SKIP_LOG2 = 160.0
SP_CLAMP = 32.0
ATTN_BLOCK = 256
FF_CHUNKS = (256, 768, 768, 768, 256)
FFN_PARTS = 2
BF16_SUBLANES = 16
VMEM_LIMIT = 60 * 1024 * 1024

F32 = jnp.float32
BF16 = jnp.bfloat16
F32_MAX = 3.4028234663852886e38


def _dot(a, b):
    return jnp.dot(a, b, preferred_element_type=F32)


def _dot_nt(a, b):
    return lax.dot_general(a, b, (((1,), (1,)), ((), ())), preferred_element_type=F32)


Q_COLS, K_COLS, V_COLS, U_COLS = (SB_WIDTH * i for i in range(4))


def _proj_cols(x, w_in_ref, start, width):
    return _dot(x, w_in_ref[:, start:start + width])


def _pool_mix(u, hist, pool_w_ref, pool_scale_ref, u_ext, t, *, tm, start_pos):
    u_ext[0:HIST_ROWS, :] = jnp.where(t == 0, hist, u_ext[tm:tm + HIST_ROWS, :])
    u_ext[HIST_ROWS:HIST_ROWS + tm, :] = u

    pos = start_pos + t * tm + lax.broadcasted_iota(jnp.int32, (tm, 1), 0)
    pooled = []
    for g, w in enumerate(POOL_WINDOWS):
        cols = slice(g * POOL_GC, (g + 1) * POOL_GC)
        s = u_ext[:, cols]
        span = 1
        while span < w:
            s = s + pltpu.roll(s, span, 0)
            span *= 2
        cnt = jnp.minimum(w, pos + 1).astype(F32)
        d = s[HIST_ROWS:, :] / cnt - u[:, cols]
        y = _dot(d.astype(BF16), pool_w_ref[g].astype(BF16)) * pool_scale_ref[:, cols]
        pooled.append(y.astype(BF16))
    return pooled


def _pool_state(u_ext, tm):
    return u_ext[tm + HIST_ROWS - POOL_HIST:tm + HIST_ROWS, :]


def _strict_upper(n):
    r = lax.broadcasted_iota(jnp.int32, (n, n), 0)
    c = lax.broadcasted_iota(jnp.int32, (n, n), 1)
    return (r > c).astype(BF16)


def _causal_mask(n):
    r = lax.broadcasted_iota(jnp.int32, (n, n), 0)
    c = lax.broadcasted_iota(jnp.int32, (n, n), 1)
    return c < r


def _store_masked_queries(q, qm_ref):
    m = q.shape[0]
    lower_half = lax.broadcasted_iota(jnp.int32, (m, LANES), 1) < SB_HEAD_DIM
    for p in range(HEAD_PAIRS):
        q2 = q[:, p * LANES:(p + 1) * LANES]
        zero = jnp.zeros_like(q2)
        qm_ref[p, 0:m, :] = jnp.where(lower_half, q2, zero)
        qm_ref[p, m:2 * m, :] = jnp.where(lower_half, zero, q2)


def _sb_all_heads(qm_ref, kv_block, decay_ref, acc_ref, upper, mask, first, key_minor=False, extra_decay=None,
                  fillers=(), rows=None):
    fillers = list(fillers)
    tile_rows = qm_ref.shape[1] // 2
    r0, r1 = (0, tile_rows) if rows is None else rows
    m = r1 - r0
    lower_half = lax.broadcasted_iota(jnp.int32, (m, LANES), 1) < SB_HEAD_DIM
    mask2 = None if mask is None else jnp.concatenate([mask, mask], axis=0)
    kv = [kv_block(p) for p in range(HEAD_PAIRS)]

    def queries(p):
        if rows is None:
            return qm_ref[p]
        return jnp.concatenate([qm_ref[p, r0:r1, :], qm_ref[p, tile_rows + r0:tile_rows + r1, :]], axis=0)

    logit, sps, rowsums = [], [], []
    for p in range(HEAD_PAIRS):
        z = (_dot if key_minor else _dot_nt)(queries(p), kv[p][0])
        sp = jnp.maximum(z, jnp.log(1.0 + jnp.exp2(jnp.minimum(z, SP_CLAMP))) * LOG2E)
        spm = sp if mask2 is None else jnp.where(mask2, sp, 0.0)
        sps.append(spm.astype(BF16))
        logit.append(z - sp)
        rowsums.append(jnp.sum(spm, axis=1, keepdims=True))
        filler = fillers.pop(0) if fillers else None
        if filler is not None:
            filler()

    later = _dot(jnp.concatenate(sps, axis=0), upper)

    least = None
    for p in range(HEAD_PAIRS):
        a = jnp.exp2(logit[p] - later[2 * m * p:2 * m * (p + 1)])
        if mask2 is not None:
            a = jnp.where(mask2, a, 0.0)
        pv = (_dot_nt if key_minor else _dot)(a.astype(BF16), kv[p][1])
        merged = jnp.where(lower_half, pv[0:m], pv[m:2 * m])
        rowsum = jnp.where(lower_half, rowsums[p][0:m], rowsums[p][m:2 * m])
        if first:
            decay = rowsum
            acc_ref[p, r0:r1, :] = merged
        else:
            prev = decay_ref[p, r0:r1, :]
            if extra_decay is not None:
                prev = prev + extra_decay
            decay = prev + rowsum
            acc_ref[p, r0:r1, :] += merged * jnp.exp2(-prev)
        decay_ref[p, r0:r1, :] = decay
        least = decay if least is None else jnp.minimum(least, decay)
    return jnp.min(least)


def _more_blocks(state):
    j, least = state
    return jnp.logical_and(j >= 0, least < SKIP_LOG2)


def _store_heads(acc_ref, o_ref):
    for p in range(HEAD_PAIRS):
        o_ref[0, :, p * LANES:(p + 1) * LANES] = acc_ref[p].astype(BF16)


def _attn_scratch(m):
    return [pltpu.VMEM((HEAD_PAIRS, 2 * m, LANES), BF16), pltpu.VMEM((HEAD_PAIRS, m, LANES), F32),
            pltpu.VMEM((HEAD_PAIRS, m, LANES), F32)]


def _proj_attn_kernel(x_ref, hist_ref, w_in_f32_ref, pool_w_ref, pool_scale_ref,
                      wo_c, wg_c, wu_c, wd_c,
                      k_ref, v_ref, p_ref, pool_ref, a_ref, wo_b, wg_b, wu_b, wd_b,
                      w_in_ref, w_kvt_ref, u_ext, xb_ref, stage_q, stage_kvt, kvt_all, qm_ref, decay_ref, acc_ref,
                      *, tm, n_t, n_tiles, n_conv):
    s = pl.program_id(0)

    @pl.when(s < n_conv)
    def _():
        for c_ref, b_ref in ((wo_c, wo_b), (wg_c, wg_b), (wu_c, wu_b), (wd_c, wd_b)):
            b_ref[...] = c_ref[...].astype(BF16)

    @pl.when(s == 0)
    def _():
        w_in_ref[...] = w_in_f32_ref[...].astype(BF16)
        w_kvt_ref[...] = w_in_f32_ref[:, K_COLS:U_COLS].T.astype(BF16)
        stage_q[...] = jnp.zeros_like(stage_q)
        stage_kvt[...] = jnp.zeros_like(stage_kvt)
        u_ext[...] = jnp.zeros_like(u_ext)

    blk = ATTN_BLOCK
    upper = _strict_upper(blk)
    qi = lax.rem(jnp.maximum(s - 1, 0), n_t)
    b0 = (tm // blk) * qi
    lo, hi = (0, blk), (blk, tm)

    def kv_block(j):
        start = pl.multiple_of(j * blk, blk)
        return lambda p: (kvt_all[p * LANES:(p + 1) * LANES, pl.ds(start, blk)],
                          kvt_all[SB_WIDTH + p * LANES:SB_WIDTH + (p + 1) * LANES, pl.ds(start, blk)])

    def take_stage():
        kvt_all[:, pl.ds(pl.multiple_of(qi * tm, tm), tm)] = stage_kvt[...]
        _store_masked_queries(stage_q[...], qm_ref)

    def attend(block, rows, mask, first, fillers=(), extra_decay=None):
        return _sb_all_heads(qm_ref, kv_block(block), decay_ref, acc_ref, upper, mask, first, key_minor=True,
                             extra_decay=extra_decay, fillers=fillers, rows=rows)

    def attend_near(fillers):
        n = HEAD_PAIRS
        causal = _causal_mask(blk)
        attend(b0 + 1, hi, causal, True, fillers[0:n])
        decay_ref[:, lo[0]:lo[1], :] = jnp.zeros((n, blk, LANES), F32)
        acc_ref[:, lo[0]:lo[1], :] = jnp.zeros((n, blk, LANES), F32)
        tile_mask = (lax.broadcasted_iota(jnp.int32, (tm, blk), 1) < lax.broadcasted_iota(jnp.int32, (tm, blk), 0))
        mid = list(fillers[n:3 * n])
        both = [(lambda a=mid[2 * i], b=mid[2 * i + 1]: (a(), b())) for i in range(len(mid) // 2)]
        least_hi = attend(b0, None, tile_mask, False, both)
        void = jnp.where(qi == 0, 2.0 * SKIP_LOG2, 0.0)
        least_lo = attend(jnp.maximum(b0 - 1, 0), lo, None, False, fillers[3 * n:4 * n], extra_decay=void)
        return least_lo, least_hi

    def attend_far(least_lo, least_hi):
        for rows, first_block, least in ((lo, b0 - 2, least_lo), (hi, b0 - 1, least_hi)):
            def body(state, rows=rows):
                j, _ = state
                return j - 1, attend(j, rows, None, False)

            lax.while_loop(_more_blocks, body, (first_block, least))
        _store_heads(acc_ref, a_ref)

    half = SB_WIDTH // 2

    def project_q(c, r):
        def piece():
            q = _proj_cols(xb_ref[r[0]:r[1], :], w_in_ref, Q_COLS + c, half) * Q_SCALE
            stage_q[r[0]:r[1], c:c + half] = q.astype(BF16)
        return piece

    def project_kv(out_ref, first_row, c, r):
        def piece():
            yt = _dot_nt(w_kvt_ref[first_row + c:first_row + c + half, :], xb_ref[r[0]:r[1], :])
            out_ref[0, c:c + half, r[0]:r[1]] = yt
            stage_kvt[first_row + c:first_row + c + half, r[0]:r[1]] = yt.astype(BF16)
        return piece

    u_parts = {}

    def project_u(c, r, last=False):
        def piece():
            u_parts[(r, c)] = _proj_cols(xb_ref[r[0]:r[1], :], w_in_ref, U_COLS + c, half)
            if last:
                u = jnp.concatenate([jnp.concatenate([u_parts[(rr, 0)], u_parts[(rr, half)]], axis=1)
                                     for rr in (lo, hi)], axis=0)
                pooled = _pool_mix(u, hist_ref[0], pool_w_ref, pool_scale_ref, u_ext, lax.rem(s, n_t), tm=tm,
                                   start_pos=0)
                for g, y in enumerate(pooled):
                    p_ref[0, :, g * POOL_GC:(g + 1) * POOL_GC] = y
                pool_ref[0] = _pool_state(u_ext, tm)
        return piece

    @pl.when(s < n_tiles)
    def _():
        take_stage()
        xb_ref[...] = x_ref[0].astype(BF16)
        pieces = [project_q(c, r) for r in (lo, hi) for c in (0, half)]
        pieces += [project_kv(k_ref, 0, c, r) for r in (lo, hi) for c in (0, half)]
        pieces += [project_kv(v_ref, SB_WIDTH, c, r) for r in (lo, hi) for c in (0, half)]
        pieces += [project_u(0, lo), project_u(half, lo), project_u(0, hi), project_u(half, hi, last=True)]
        attend_far(*attend_near(pieces))

    @pl.when(s == n_tiles)
    def _():
        take_stage()
        attend_far(*attend_near([]))


def _proj_attn_prompt(x, hist, w_in, pool_w, pool_scale2, ffn_weights, n_conv):
    B, T, _ = x.shape
    tm = 2 * ATTN_BLOCK
    assert T % tm == 0 and T >= POOL_HIST
    n_t = T // tm
    n_tiles = B * n_t
    assert n_conv <= n_tiles and all(w.shape[0] % (n_conv * BF16_SUBLANES) == 0 for w in ffn_weights)
    chunked = [pl.BlockSpec((w.shape[0] // n_conv, w.shape[1]), lambda s: (jnp.minimum(s, n_conv - 1), 0))
               for w in ffn_weights]
    kern = functools.partial(_proj_attn_kernel, tm=tm, n_t=n_t, n_tiles=n_tiles, n_conv=n_conv)
    proj = lambda s: jnp.minimum(s, n_tiles - 1)
    attn = lambda s: jnp.maximum(s - 1, 0)
    const = lambda *shape: pl.BlockSpec(shape, lambda s: (0,) * len(shape), pipeline_mode=pl.Buffered(1))
    proj_rows = lambda width: pl.BlockSpec((1, tm, width), lambda s: (proj(s) // n_t, proj(s) % n_t, 0))
    proj_cols = pl.BlockSpec((1, SB_WIDTH, tm), lambda s: (proj(s) // n_t, 0, proj(s) % n_t))
    return pl.pallas_call(
        kern,
        grid=(n_tiles + 1,),
        in_specs=[
            proj_rows(D_MODEL),
            pl.BlockSpec((1, HIST_ROWS, POOL_WIDTH), lambda s: (proj(s) // n_t, 0, 0)),
            const(D_MODEL, 3 * SB_WIDTH + POOL_WIDTH),
            const(len(POOL_WINDOWS), POOL_GC, POOL_GC),
            const(1, POOL_WIDTH),
        ] + chunked,
        out_specs=[
            proj_cols,
            proj_cols,
            proj_rows(POOL_WIDTH),
            pl.BlockSpec((1, POOL_HIST, POOL_WIDTH), lambda s: (proj(s) // n_t, 0, 0)),
            pl.BlockSpec((1, tm, SB_WIDTH), lambda s: (attn(s) // n_t, attn(s) % n_t, 0)),
        ] + chunked,
        out_shape=[
            jax.ShapeDtypeStruct((B, SB_WIDTH, T), F32),
            jax.ShapeDtypeStruct((B, SB_WIDTH, T), F32),
            jax.ShapeDtypeStruct((B, T, POOL_WIDTH), BF16),
            jax.ShapeDtypeStruct((B, POOL_HIST, POOL_WIDTH), F32),
            jax.ShapeDtypeStruct((B, T, SB_WIDTH), BF16),
        ] + [jax.ShapeDtypeStruct(w.shape, BF16) for w in ffn_weights],
        scratch_shapes=[
            pltpu.VMEM(w_in.shape, BF16),
            pltpu.VMEM((2 * SB_WIDTH, D_MODEL), BF16),
            pltpu.VMEM((HIST_ROWS + tm, POOL_WIDTH), F32),
            pltpu.VMEM((tm, D_MODEL), BF16),
            pltpu.VMEM((tm, SB_WIDTH), BF16),
            pltpu.VMEM((2 * SB_WIDTH, tm), BF16),
            pltpu.VMEM((2 * SB_WIDTH, T), BF16),
        ] + _attn_scratch(tm),
        compiler_params=pltpu.CompilerParams(
            dimension_semantics=("arbitrary",), vmem_limit_bytes=VMEM_LIMIT),
        name="proj_attn_prompt",
    )(x, hist, w_in, pool_w, pool_scale2, *ffn_weights)


def _proj_attn_sample_kernel(x_ref, hist_ref, w_in_f32_ref, pool_w_ref, pool_scale_ref,
                             ck_ref, cv_ref, ck_hbm, cv_hbm,
                             k_ref, v_ref, p_ref, pool_ref, a_ref,
                             w_in_ref, q_all, kv_new, u_all, u_ext, ck_buf, cv_buf, qm_ref, decay_ref, acc_ref,
                             *, t_new, past, blk):
    b = pl.program_id(0)
    rows = pl.ds(pl.multiple_of(b * t_new, t_new), t_new)

    @pl.when(b == 0)
    def _():
        w_in_ref[...] = w_in_f32_ref[...].astype(BF16)
        u_ext[...] = jnp.zeros_like(u_ext)
        x = x_ref[...].astype(BF16)
        q_all[...] = (_proj_cols(x, w_in_ref, Q_COLS, SB_WIDTH) * Q_SCALE).astype(BF16)
        k = _proj_cols(x, w_in_ref, K_COLS, SB_WIDTH)
        k_ref[...] = k
        kv_new[:, 0:SB_WIDTH] = k.astype(BF16)
        v = _proj_cols(x, w_in_ref, V_COLS, SB_WIDTH)
        v_ref[...] = v
        kv_new[:, SB_WIDTH:2 * SB_WIDTH] = v.astype(BF16)
        u_all[...] = _proj_cols(x, w_in_ref, U_COLS, POOL_WIDTH)

    pooled = _pool_mix(u_all[rows, :], hist_ref[0], pool_w_ref, pool_scale_ref, u_ext, 0, tm=t_new, start_pos=past)
    for g, y in enumerate(pooled):
        p_ref[rows, g * POOL_GC:(g + 1) * POOL_GC] = y
    pool_ref[b] = _pool_state(u_ext, t_new)

    _store_masked_queries(q_all[rows, :], qm_ref)
    new_kv = lambda p: (kv_new[rows, p * LANES:(p + 1) * LANES],
                        kv_new[rows, SB_WIDTH + p * LANES:SB_WIDTH + (p + 1) * LANES])
    _sb_all_heads(qm_ref, new_kv, decay_ref, acc_ref, _strict_upper(t_new), _causal_mask(t_new), True)
    upper = _strict_upper(blk)
    recent = lambda p: (ck_ref[0, p * LANES:(p + 1) * LANES, :].astype(BF16),
                        cv_ref[0, p * LANES:(p + 1) * LANES, :].astype(BF16))
    least = _sb_all_heads(qm_ref, recent, decay_ref, acc_ref, upper, None, False, key_minor=True)

    def body(state):
        j, _ = state
        start = pl.multiple_of(j * blk, blk)
        pltpu.sync_copy(ck_hbm.at[b, :, pl.ds(start, blk)], ck_buf)
        pltpu.sync_copy(cv_hbm.at[b, :, pl.ds(start, blk)], cv_buf)
        older = lambda p: (ck_buf[p * LANES:(p + 1) * LANES, :].astype(BF16),
                           cv_buf[p * LANES:(p + 1) * LANES, :].astype(BF16))
        return j - 1, _sb_all_heads(qm_ref, older, decay_ref, acc_ref, upper, None, False, key_minor=True)

    lax.while_loop(_more_blocks, body, (past // blk - 2, least))
    for p in range(HEAD_PAIRS):
        a_ref[rows, p * LANES:(p + 1) * LANES] = acc_ref[p].astype(BF16)


def _proj_attn_sample(x, hist, w_in, pool_w, pool_scale2, cache_kt, cache_vt):
    B, t_new, _ = x.shape
    past = cache_kt.shape[2]
    blk = ATTN_BLOCK
    assert past % blk == 0 and t_new >= HIST_ROWS
    n = B * t_new
    kern = functools.partial(_proj_attn_sample_kernel, t_new=t_new, past=past, blk=blk)
    const = lambda *shape: pl.BlockSpec(shape, lambda b: (0,) * len(shape), pipeline_mode=pl.Buffered(1))
    whole = lambda *shape: pl.BlockSpec(shape, lambda b: (0,) * len(shape))
    recent = pl.BlockSpec((1, SB_WIDTH, blk), lambda b: (b, 0, past // blk - 1))
    k, v, p, pool, a = pl.pallas_call(
        kern,
        grid=(B,),
        in_specs=[
            const(n, D_MODEL),
            pl.BlockSpec((1, HIST_ROWS, POOL_WIDTH), lambda b: (b, 0, 0)),
            const(D_MODEL, 3 * SB_WIDTH + POOL_WIDTH),
            const(len(POOL_WINDOWS), POOL_GC, POOL_GC),
            const(1, POOL_WIDTH),
            recent, recent,
            pl.BlockSpec(memory_space=pl.ANY), pl.BlockSpec(memory_space=pl.ANY),
        ],
        out_specs=[whole(n, SB_WIDTH), whole(n, SB_WIDTH), whole(n, POOL_WIDTH),
                   whole(B, POOL_HIST, POOL_WIDTH), whole(n, SB_WIDTH)],
        out_shape=[
            jax.ShapeDtypeStruct((n, SB_WIDTH), F32),
            jax.ShapeDtypeStruct((n, SB_WIDTH), F32),
            jax.ShapeDtypeStruct((n, POOL_WIDTH), BF16),
            jax.ShapeDtypeStruct((B, POOL_HIST, POOL_WIDTH), F32),
            jax.ShapeDtypeStruct((n, SB_WIDTH), BF16),
        ],
        scratch_shapes=[
            pltpu.VMEM(w_in.shape, BF16),
            pltpu.VMEM((n, SB_WIDTH), BF16),
            pltpu.VMEM((n, 2 * SB_WIDTH), BF16),
            pltpu.VMEM((n, POOL_WIDTH), F32),
            pltpu.VMEM((HIST_ROWS + t_new, POOL_WIDTH), F32),
            pltpu.VMEM((SB_WIDTH, blk), F32),
            pltpu.VMEM((SB_WIDTH, blk), F32),
        ] + _attn_scratch(t_new),
        compiler_params=pltpu.CompilerParams(
            dimension_semantics=("arbitrary",), vmem_limit_bytes=VMEM_LIMIT),
        name="proj_attn_sample",
    )(x.reshape(n, D_MODEL), hist, w_in, pool_w, pool_scale2, cache_kt, cache_vt, cache_kt, cache_vt)
    return k, v, p, pool, a


def _layer_norm(x, g, b):
    mu = jnp.mean(x, axis=-1, keepdims=True)
    xc = x - mu
    var = jnp.mean(xc * xc, axis=-1, keepdims=True)
    return xc * lax.rsqrt(var + LN_EPS) * g + b


def _ffn_rows(x_ref, a_ref, p_ref, wo_ref, g1, b1, wg_ref, wu_ref, wd_ref, acc_ref, x1b_ref, filler):
    m = x_ref.shape[0]
    halves = tuple(slice(i * m // FFN_PARTS, (i + 1) * m // FFN_PARTS) for i in range(FFN_PARTS))
    chunks = [(sum(FF_CHUNKS[:i]), w) for i, w in enumerate(FF_CHUNKS)]

    def swiglu_chunk(x1b, chunk):
        c, w = chunk
        gate = _dot(x1b, wg_ref[:, c:c + w])
        up = _dot(x1b, wu_ref[:, c:c + w])
        hmid = (gate * jax.nn.sigmoid(gate) * up).astype(BF16)
        return _dot(hmid, wd_ref[c:c + w, :])

    mix = [_dot(a_ref[r, :], wo_ref[0:SB_WIDTH, :]) + _dot(p_ref[r, :], wo_ref[SB_WIDTH:, :]) for r in halves]
    for r, mx in zip(halves, mix):
        x1 = _layer_norm(ALPHA * x_ref[r, :] + mx, g1, b1)
        x1b_ref[r, :] = x1.astype(BF16)
        acc_ref[r, :] = ALPHA * x1 + swiglu_chunk(x1b_ref[r, :], chunks[0])
    total = None
    for j, c in enumerate(chunks[1:]):
        anchor = filler(j, len(chunks) - 1)
        x1b_ref[0:anchor.shape[0], 0:anchor.shape[1]] += anchor.astype(BF16)
        total = acc_ref[...] + swiglu_chunk(x1b_ref[...], c)
        if c is not chunks[-1]:
            acc_ref[...] = total
    return total


def _out_ffn_kernel(xp_ref, ap_ref, pp_ref, xs_ref, as_ref, ps_ref, wo_s, wg_s, wu_s, wd_s,
                    g1_ref, b1_ref, g2_ref, b2_ref, yp_ref, ys_ref,
                    accp_ref, accs_ref, pre_ref, x1b_ref, *, n_tiles):
    i = pl.program_id(0)

    @pl.when(i == 0)
    def _():
        pre_ref[...] = jnp.zeros_like(pre_ref)

    def finish_previous_tile(j, n):
        rows = yp_ref.shape[0] // n
        r = slice(j * rows, (j + 1) * rows)
        y = _layer_norm(pre_ref[r, :], g2_ref[...], b2_ref[...])
        yp_ref[r, :] = y
        top = jnp.max(y.reshape(rows // BF16_SUBLANES, BF16_SUBLANES, D_MODEL), axis=0)
        top = functools.reduce(jnp.maximum, [top[:, c:c + LANES] for c in range(0, D_MODEL, LANES)])
        return jnp.where(top > F32_MAX, top, 0.0)

    def rows(x_ref, a_ref, p_ref, acc_ref):
        m = x_ref.shape[0]
        return _ffn_rows(x_ref, a_ref, p_ref, wo_s, g1_ref[...], b1_ref[...], wg_s, wu_s, wd_s, acc_ref,
                         x1b_ref.at[0:m, :], finish_previous_tile)

    @pl.when(i < n_tiles)
    def _():
        pre_ref[...] = rows(xp_ref, ap_ref, pp_ref, accp_ref)

    @pl.when(i == n_tiles)
    def _():
        ys_ref[...] = _layer_norm(rows(xs_ref, as_ref, ps_ref, accs_ref), g2_ref[...], b2_ref[...])


def _out_ffn(xp, ap, pp, xs, a_s, ps, wo_b, wg_b, wu_b, wd_b, g1, b1, g2, b2, *, tm):
    Np, Ns = xp.shape[0], xs.shape[0]
    assert Np % tm == 0 and sum(FF_CHUNKS) == D_FF
    n_tiles = Np // tm
    tile = lambda i: jnp.clip(i, 0, n_tiles - 1)
    row = lambda width: pl.BlockSpec((tm, width), lambda i: (tile(i), 0))
    once = lambda *shape: pl.BlockSpec(shape, lambda i: (0,) * len(shape), pipeline_mode=pl.Buffered(1))
    kern = functools.partial(_out_ffn_kernel, n_tiles=n_tiles)
    return pl.pallas_call(
        kern,
        grid=(n_tiles + 1,),
        in_specs=[
            row(D_MODEL), row(SB_WIDTH), row(POOL_WIDTH),
            once(Ns, D_MODEL), once(Ns, SB_WIDTH), once(Ns, POOL_WIDTH),
            once(*wo_b.shape), once(*wg_b.shape), once(*wu_b.shape), once(*wd_b.shape),
            once(1, D_MODEL), once(1, D_MODEL), once(1, D_MODEL), once(1, D_MODEL),
        ],
        out_specs=[pl.BlockSpec((tm, D_MODEL), lambda i: (tile(i - 1), 0)),
                   pl.BlockSpec((Ns, D_MODEL), lambda i: (0, 0))],
        out_shape=[jax.ShapeDtypeStruct((Np, D_MODEL), F32), jax.ShapeDtypeStruct((Ns, D_MODEL), F32)],
        scratch_shapes=[pltpu.VMEM((tm, D_MODEL), F32), pltpu.VMEM((Ns, D_MODEL), F32),
                        pltpu.VMEM((tm, D_MODEL), F32), pltpu.VMEM((tm, D_MODEL), BF16)],
        compiler_params=pltpu.CompilerParams(
            dimension_semantics=("arbitrary",), vmem_limit_bytes=VMEM_LIMIT),
        name="out_proj_ffn",
    )(xp, ap, pp, xs, a_s, ps, wo_b, wg_b, wu_b, wd_b, g1, b1, g2, b2)


def kernel(x_prompt, x_sample, cache_k, cache_v, state_pool, w_in, pool_w, pool_scale, w_out,
           ln1_g, ln1_b, w_gate, w_up, w_down, ln2_g, ln2_b):
    Bp, Tp, _ = x_prompt.shape
    Bs, Ts, _ = x_sample.shape
    past = cache_k.shape[1]

    row2 = lambda a: a.reshape(1, -1).astype(F32)
    pool_scale2 = row2(pool_scale)

    hist_s = jnp.concatenate([jnp.zeros((Bs, HIST_ROWS - POOL_HIST, POOL_WIDTH), F32), state_pool.astype(F32)], axis=1)
    feature_major = lambda c: jnp.transpose(c, (0, 2, 3, 1)).reshape(Bs, SB_WIDTH, past)
    k_s, v_s, mix_s, pool_s, a_s = _proj_attn_sample(x_sample, hist_s, w_in, pool_w, pool_scale2,
                                                     feature_major(cache_k), feature_major(cache_v))

    hist_p = jnp.zeros((Bp, HIST_ROWS, POOL_WIDTH), F32)
    kt_p, vt_p, mix_p, pool_p, a_p, wo_b, wg_b, wu_b, wd_b = _proj_attn_prompt(
        x_prompt, hist_p, w_in, pool_w, pool_scale2, ffn_weights=(w_out, w_gate, w_up, w_down), n_conv=16)
    position_major = lambda c: jnp.transpose(c.reshape(Bp, SB_HEADS, SB_HEAD_DIM, Tp), (0, 3, 1, 2))

    y_p, y_s = _out_ffn(
        x_prompt.reshape(Bp * Tp, D_MODEL), a_p.reshape(Bp * Tp, SB_WIDTH), mix_p.reshape(Bp * Tp, POOL_WIDTH),
        x_sample.reshape(Bs * Ts, D_MODEL), a_s, mix_s,
        wo_b, wg_b, wu_b, wd_b, row2(ln1_g), row2(ln1_b), row2(ln2_g), row2(ln2_b), tm=512)

    heads = lambda a: a.reshape(Bs, Ts, SB_HEADS, SB_HEAD_DIM)
    return (y_p.reshape(Bp, Tp, D_MODEL), y_s.reshape(Bs, Ts, D_MODEL),
            position_major(kt_p), position_major(vt_p), pool_p, heads(k_s), heads(v_s), pool_s)
```
